```python
import math
import jax, jax.numpy as jnp
from jax import lax
import numpy as np

D_MODEL = 1024
BATCH = 2
SEQ = 8192
DEPTH = 1

ATTN_HEADS = 8
HEAD_DIM = 64
ATTN_W = ATTN_HEADS * HEAD_DIM
MOBA_BLOCK = 256
MOBA_TOPK = 3
Q_CHUNK = 64
POOL_GROUPS = 4
POOL_GROUP_W = 128
POOL_W = POOL_GROUPS * POOL_GROUP_W
POOL_WINDOWS = (2, 4, 8, 16)
NUM_BUCKETS = 32
MAX_DISTANCE = 128
D_FF = 2816
CONV_W = 3
EPS = 1e-6
IN_W = 3 * ATTN_W + POOL_W + 2 * D_MODEL

kernel_name = "hybrid_moba_pool_convffn_block"


def rms_norm(x, gain):
    xf = x.astype(jnp.float32)
    y = xf * lax.rsqrt(jnp.mean(xf * xf, axis=-1, keepdims=True) + EPS)
    return (y * gain.astype(jnp.float32)).astype(x.dtype)


def t5_bucket(dist):
    max_exact = NUM_BUCKETS // 2
    n = jnp.maximum(dist, 0)
    nf = jnp.maximum(n, 1).astype(jnp.float32)
    large = max_exact + (jnp.log(nf / max_exact) / math.log(MAX_DISTANCE / max_exact)
                         * (NUM_BUCKETS - max_exact)).astype(jnp.int32)
    large = jnp.minimum(large, NUM_BUCKETS - 1)
    return jnp.where(n < max_exact, n, large)


def moba_attention(q, k, v, rel_bias):
    B, H, S, Dh = q.shape
    n_blocks = -(-S // MOBA_BLOCK)
    s_pad = n_blocks * MOBA_BLOCK
    pad = ((0, 0), (0, 0), (0, s_pad - S), (0, 0))
    q = jnp.pad(q, pad)
    k = jnp.pad(k, pad)
    v = jnp.pad(v, pad)
    k_blocks = k.reshape(B, H, n_blocks, MOBA_BLOCK, Dh)
    v_blocks = v.reshape(B, H, n_blocks, MOBA_BLOCK, Dh)
    k_mean = jnp.mean(k_blocks.astype(jnp.float32), axis=3).astype(k.dtype)

    pos = jnp.arange(s_pad)
    q_blk = pos // MOBA_BLOCK
    gate = jnp.einsum('bhsd,bhnd->bhsn', q, k_mean).astype(jnp.float32)
    past = jnp.arange(n_blocks)[None, :] < q_blk[:, None]
    gate = jnp.where(past[None, None], gate, -jnp.inf)
    k_sel = min(MOBA_TOPK, n_blocks)
    _, sel = lax.top_k(gate, k_sel)
    sel_valid = jnp.arange(k_sel)[None, :] < q_blk[:, None]

    n_chunks = s_pad // Q_CHUNK
    q_c = q.reshape(B, H, n_chunks, Q_CHUNK, Dh).transpose(2, 0, 1, 3, 4)
    sel_c = sel.reshape(B, H, n_chunks, Q_CHUNK, k_sel).transpose(2, 0, 1, 3, 4)
    valid_c = sel_valid.reshape(n_chunks, Q_CHUNK, k_sel)
    bias_t = rel_bias.T.astype(jnp.float32)
    b_idx = jnp.arange(B)[:, None, None, None]
    h_idx = jnp.arange(H)[None, :, None, None]
    offs = jnp.arange(MOBA_BLOCK)
    scale = HEAD_DIM ** -0.5
    n_sel = k_sel * MOBA_BLOCK

    def chunk_fn(args):
        ci, qc, selc, validc = args
        qpos = ci * Q_CHUNK + jnp.arange(Q_CHUNK)
        own = (ci * Q_CHUNK) // MOBA_BLOCK
        k_own = lax.dynamic_index_in_dim(k_blocks, own, axis=2, keepdims=False)
        v_own = lax.dynamic_index_in_dim(v_blocks, own, axis=2, keepdims=False)
        dist_own = qpos[:, None] - (own * MOBA_BLOCK + offs)[None, :]
        bias_own = bias_t[:, t5_bucket(dist_own)]
        logit_own = jnp.einsum('bhqd,bhkd->bhqk', qc, k_own).astype(jnp.float32) * scale + bias_own
        logit_own = jnp.where(dist_own >= 0, logit_own, -jnp.inf)
        k_g = k_blocks[b_idx, h_idx, selc]
        v_g = v_blocks[b_idx, h_idx, selc]
        kpos_sel = selc[..., None] * MOBA_BLOCK + offs
        dist_sel = qpos[None, None, :, None, None] - kpos_sel
        bias_sel = bias_t[h_idx[..., None], t5_bucket(dist_sel)]
        logit_sel = jnp.einsum('bhqd,bhqnkd->bhqnk', qc, k_g).astype(jnp.float32) * scale + bias_sel
        logit_sel = jnp.where(validc[None, None, :, :, None], logit_sel, -jnp.inf)
        logits = jnp.concatenate([logit_sel.reshape(B, H, Q_CHUNK, n_sel), logit_own], axis=-1)
        prob = jax.nn.softmax(logits, axis=-1)
        p_sel = prob[..., :n_sel].reshape(B, H, Q_CHUNK, k_sel, MOBA_BLOCK).astype(v.dtype)
        p_own = prob[..., n_sel:].astype(v.dtype)
        return (jnp.einsum('bhqnk,bhqnkd->bhqd', p_sel, v_g)
                + jnp.einsum('bhqk,bhkd->bhqd', p_own, v_own))

    out = lax.map(chunk_fn, (jnp.arange(n_chunks), q_c, sel_c, valid_c))
    out = out.transpose(1, 2, 0, 3, 4).reshape(B, H, s_pad, Dh)
    return out[:, :, :S]


def multiscale_pool(p):
    B, S, G, Cg = p.shape
    pf = p.astype(jnp.float32)
    cs = jnp.concatenate([jnp.zeros((B, 1, G, Cg), jnp.float32), jnp.cumsum(pf, axis=1)], axis=1)
    win = jnp.array(POOL_WINDOWS, jnp.int32)
    t = jnp.arange(S)
    lo = jnp.maximum(t[:, None] + 1 - win[None, :], 0)
    cs_lo = cs[:, lo, jnp.arange(G)[None, :], :]
    count = jnp.minimum(t[:, None] + 1, win[None, :]).astype(jnp.float32)
    mean = (cs[:, 1:] - cs_lo) / count[None, :, :, None]
    return (mean - pf).astype(p.dtype)


def causal_dwconv(u, w, b):
    C = u.shape[-1]
    y = lax.conv_general_dilated(u, w[:, None, :].astype(u.dtype), window_strides=(1,),
                                 padding=((CONV_W - 1, 0),),
                                 dimension_numbers=('NWC', 'WIO', 'NWC'),
                                 feature_group_count=C)
    return y + b


def setup_inputs(seed: int = 0) -> dict:
    key = jax.random.key(seed)
    ks = jax.random.split(key, 20)
    L, D = DEPTH, D_MODEL
    nrm = lambda k, shape, s: jax.random.normal(k, shape, jnp.float32) * s
    return {
        "x": nrm(ks[0], (BATCH, SEQ, D), 1.0),
        "c": nrm(ks[1], (BATCH, D), 1.0),
        "ada_w": nrm(ks[2], (L, D, 6 * D), 0.1 * D ** -0.5),
        "ada_b": nrm(ks[3], (L, 6 * D), 0.02),
        "norm1_g": 1.0 + nrm(ks[4], (L, D), 0.05),
        "w_in": nrm(ks[5], (L, D, IN_W), D ** -0.5),
        "q_norm_g": 1.0 + nrm(ks[6], (L, HEAD_DIM), 0.05),
        "k_norm_g": 1.0 + nrm(ks[7], (L, HEAD_DIM), 0.05),
        "rel_bias": nrm(ks[8], (NUM_BUCKETS, ATTN_HEADS), 0.5),
        "pool_w": nrm(ks[9], (L, POOL_GROUPS, POOL_GROUP_W, POOL_GROUP_W), POOL_GROUP_W ** -0.5),
        "pool_scale": 1.0 + nrm(ks[10], (L, POOL_W), 0.1),
        "w_branch_attn": nrm(ks[11], (L, ATTN_W, D), ATTN_W ** -0.5),
        "w_branch_pool": nrm(ks[12], (L, POOL_W, D), POOL_W ** -0.5),
        "w_out": nrm(ks[13], (L, D, D), D ** -0.5),
        "norm2_g": 1.0 + nrm(ks[14], (L, D), 0.05),
        "w_up": nrm(ks[15], (L, D, 2 * D_FF), D ** -0.5),
        "conv_w": nrm(ks[16], (L, CONV_W, 2 * D_FF), CONV_W ** -0.5),
        "conv_b": nrm(ks[17], (L, 2 * D_FF), 0.02),
        "w_down": nrm(ks[18], (L, D_FF, D), D_FF ** -0.5),
    }


def reference(x, c, ada_w, ada_b, norm1_g, w_in, q_norm_g, k_norm_g, rel_bias, pool_w,
              pool_scale, w_branch_attn, w_branch_pool, w_out, norm2_g, w_up, conv_w, conv_b,
              w_down):
    B, S, D = x.shape
    split_at = [ATTN_W, 2 * ATTN_W, 3 * ATTN_W, 3 * ATTN_W + POOL_W, 3 * ATTN_W + POOL_W + D_MODEL]
    c_act = jax.nn.silu(c)
    for l in range(DEPTH):
        mod = c_act @ ada_w[l] + ada_b[l]
        shift1, scale1, gate1, shift2, scale2, gate2 = [m[:, None, :] for m in jnp.split(mod, 6, axis=-1)]

        h = rms_norm(x, norm1_g[l]) * (1 + scale1) + shift1
        z = h @ w_in[l]
        q, k, v, p, g_attn, g_pool = jnp.split(z, split_at, axis=-1)
        heads = lambda t: t.reshape(B, S, ATTN_HEADS, HEAD_DIM).transpose(0, 2, 1, 3)
        q = rms_norm(heads(q), q_norm_g[l])
        k = rms_norm(heads(k), k_norm_g[l])
        attn = moba_attention(q, k, heads(v), rel_bias)
        attn = attn.transpose(0, 2, 1, 3).reshape(B, S, ATTN_W)
        pool = multiscale_pool(p.reshape(B, S, POOL_GROUPS, POOL_GROUP_W))
        pool = jnp.einsum('bsgc,gcd->bsgd', pool, pool_w[l]).reshape(B, S, POOL_W) * pool_scale[l]
        merged = (jax.nn.sigmoid(g_attn) * (attn @ w_branch_attn[l])
                  + jax.nn.sigmoid(g_pool) * (pool @ w_branch_pool[l]))
        x = x + gate1 * (merged @ w_out[l])

        h2 = rms_norm(x, norm2_g[l]) * (1 + scale2) + shift2
        u = causal_dwconv(h2 @ w_up[l], conv_w[l], conv_b[l])
        u_g, u_v = jnp.split(u, 2, axis=-1)
        x = x + gate2 * ((jax.nn.silu(u_g) * u_v) @ w_down[l])
    return x
```

```python
import functools
import math

import numpy as np
import jax
import jax.numpy as jnp
from jax import lax
from jax.experimental import pallas as pl
from jax.experimental.pallas import tpu as pltpu

F32 = jnp.float32
BF16 = jnp.bfloat16

D_MODEL = 1024
ATTN_HEADS = 8
HEAD_DIM = 64
ATTN_W = ATTN_HEADS * HEAD_DIM
MOBA_BLOCK = 256
MOBA_TOPK = 3
POOL_GROUPS = 4
POOL_GROUP_W = 128
POOL_W = POOL_GROUPS * POOL_GROUP_W
POOL_WINDOWS = (2, 4, 8, 16)
NUM_BUCKETS = 32
MAX_DISTANCE = 128
D_FF = 2816
CONV_W = 3
EPS = 1e-6
IN_W = 3 * ATTN_W + POOL_W + 2 * D_MODEL

HEADS_PER_STEP = 2
PAIR_W = HEADS_PER_STEP * HEAD_DIM
N_PAIRS = ATTN_HEADS // HEADS_PER_STEP
V_ROWS = HEAD_DIM + 16
MASK_BIG = 1e30
ROW_TILE = 256
POOL_HALO = 16
FFN_CHUNK = 256
CONV_HALO = 8
VMEM_LIMIT = 56 * 1024 * 1024


def _split_bf16(a):
    hi = a.astype(BF16)
    lo = (a - hi.astype(F32)).astype(BF16)
    return hi, lo


def _dot(a, b):
    return jnp.dot(a, b, preferred_element_type=F32)


def _dot3(a, b):
    ah, al = _split_bf16(a)
    bh, bl = _split_bf16(b)
    return _dot(ah, bh) + _dot(al, bh) + _dot(ah, bl)


def _ada_body(c_ref, w_ref, b_ref, o_ref):
    c = c_ref[...]
    c_act = c * (1.0 / (1.0 + jnp.exp(-c)))
    o_ref[...] = _dot3(c_act, w_ref[...]) + b_ref[...]


def _ada(c_pad, ada_w, ada_b):
    rows, d = c_pad.shape
    n = ada_w.shape[1]
    tn = 1024
    return pl.pallas_call(
        _ada_body,
        grid=(n // tn,),
        in_specs=[
            pl.BlockSpec((rows, d), lambda j: (0, 0)),
            pl.BlockSpec((d, tn), lambda j: (0, j)),
            pl.BlockSpec((1, tn), lambda j: (0, j)),
        ],
        out_specs=pl.BlockSpec((rows, tn), lambda j: (0, j)),
        out_shape=jax.ShapeDtypeStruct((rows, n), F32),
        compiler_params=pltpu.CompilerParams(
            dimension_semantics=("arbitrary",), vmem_limit_bytes=VMEM_LIMIT),
        name="ada",
    )(c_pad, ada_w, ada_b)


def _norm_mod(x, gain, scale, shift):
    ms = jnp.mean(x * x, axis=-1, keepdims=True)
    y = x * lax.rsqrt(ms + EPS) * gain
    return y * (1.0 + scale) + shift


def _inproj_body(x_ref, mod_ref, g_ref, w_ref, z_ref):
    h = _norm_mod(x_ref[0], g_ref[...], mod_ref[0, 1:2, :], mod_ref[0, 0:1, :]).astype(BF16)
    n_chunk = 1024
    for n in range(IN_W // n_chunk):
        cols = slice(n * n_chunk, (n + 1) * n_chunk)
        z_ref[0, :, cols] = _dot(h, w_ref[:, cols]).astype(BF16)


def _inproj(x, mod, g1, w_in_bf):
    b, s, d = x.shape
    return pl.pallas_call(
        _inproj_body,
        grid=(b, s // ROW_TILE),
        in_specs=[
            pl.BlockSpec((1, ROW_TILE, d), lambda bi, t: (bi, t, 0)),
            pl.BlockSpec((1, 6, d), lambda bi, t: (bi, 0, 0)),
            pl.BlockSpec((1, d), lambda bi, t: (0, 0)),
            pl.BlockSpec((d, IN_W), lambda bi, t: (0, 0)),
        ],
        out_specs=pl.BlockSpec((1, ROW_TILE, IN_W), lambda bi, t: (bi, t, 0)),
        out_shape=jax.ShapeDtypeStruct((b, s, IN_W), BF16),
        compiler_params=pltpu.CompilerParams(
            dimension_semantics=("arbitrary", "arbitrary"), vmem_limit_bytes=VMEM_LIMIT),
        name="inproj",
    )(x, mod, g1, w_in_bf)


def _t5_bucket_np(dist):
    max_exact = NUM_BUCKETS // 2
    n = np.maximum(dist, 0)
    nf = np.maximum(n, 1).astype(np.float64)
    large = max_exact + (np.log(nf / max_exact) / math.log(MAX_DISTANCE / max_exact)
                         * (NUM_BUCKETS - max_exact)).astype(np.int32)
    large = np.minimum(large, NUM_BUCKETS - 1)
    return np.where(n < max_exact, n, large).astype(np.int32)


def _bucket_tables():
    k = np.arange(MOBA_BLOCK)[:, None]
    q = np.arange(MOBA_BLOCK)[None, :]
    own = np.where(q - k >= 0, _t5_bucket_np(q - k), -1)
    prev = _t5_bucket_np(MOBA_BLOCK + q - k)
    return np.stack([own, prev]).astype(np.int32)


def _biasprep_body(rb_ref, bkt_ref, o_ref):
    h = pl.program_id(0)
    far = rb_ref[NUM_BUCKETS - 1, h]
    for t in range(2):
        bkt = bkt_ref[t]
        acc = jnp.where(bkt < 0, -MASK_BIG, 0.0).astype(F32)
        for bk in range(NUM_BUCKETS):
            acc = jnp.where(bkt == bk, rb_ref[bk, h] - far, acc)
        o_ref[0, t] = acc


def _biasprep(rel_bias):
    bkt = jnp.asarray(_bucket_tables())
    return pl.pallas_call(
        _biasprep_body,
        grid=(ATTN_HEADS,),
        in_specs=[
            pl.BlockSpec(memory_space=pltpu.SMEM),
            pl.BlockSpec((2, MOBA_BLOCK, MOBA_BLOCK), lambda h: (0, 0, 0)),
        ],
        out_specs=pl.BlockSpec((1, 2, MOBA_BLOCK, MOBA_BLOCK), lambda h: (h, 0, 0, 0)),
        out_shape=jax.ShapeDtypeStruct((ATTN_HEADS, 2, MOBA_BLOCK, MOBA_BLOCK), F32),
        compiler_params=pltpu.CompilerParams(dimension_semantics=("arbitrary",)),
        name="biasprep",
    )(rel_bias, bkt)


def _attn_body(zq_ref, zk_ref, zv_ref, gq_ref, gk_ref, bias_ref, o_ref,
               kn_ref, vta_ref, kmean_ref, pen_ref, *, n_blocks):
    i = pl.program_id(2)
    blk = MOBA_BLOCK

    @pl.when(i == 0)
    def _():
        kmean_ref[...] = jnp.zeros_like(kmean_ref)

    kraw = zk_ref[0].astype(F32)
    lane = lax.broadcasted_iota(jnp.int32, (blk, PAIR_W), 1)
    first = lane < HEAD_DIM
    sq = kraw * kraw
    ss_a = jnp.sum(jnp.where(first, sq, 0.0), axis=-1, keepdims=True)
    ss_b = jnp.sum(jnp.where(first, 0.0, sq), axis=-1, keepdims=True)
    rinv = lax.rsqrt(jnp.where(first, ss_a, ss_b) * (1.0 / HEAD_DIM) + EPS)
    kn = kraw * rinv * gk_ref[...]
    kn_bf = kn.astype(BF16)
    row0 = pl.multiple_of(i * blk, blk)
    kn_ref[pl.ds(row0, blk), :] = kn_bf
    blk_rows = lax.broadcasted_iota(jnp.int32, (n_blocks, PAIR_W), 0)
    kmean_new = jnp.mean(kn, axis=0, keepdims=True)
    kmean_prev = kmean_ref[...]
    kmean_ref[...] = jnp.where(blk_rows == i, kmean_new, kmean_prev)

    vt = zv_ref[0].astype(F32).T
    ones = jnp.ones((V_ROWS - HEAD_DIM, blk), BF16)
    for hd in range(HEADS_PER_STEP):
        vta_ref[i, hd, 0:HEAD_DIM, :] = vt[hd * HEAD_DIM:(hd + 1) * HEAD_DIM].astype(BF16)
        vta_ref[i, hd, HEAD_DIM:V_ROWS, :] = ones

    qt = zq_ref[0].astype(F32).T
    qsq = qt * qt
    zeros_half = jnp.zeros((HEAD_DIM, blk), F32)
    w_f32 = []
    for hd in range(HEADS_PER_STEP):
        rows = slice(hd * HEAD_DIM, (hd + 1) * HEAD_DIM)
        ss = jnp.sum(qsq[rows], axis=0, keepdims=True)
        r = lax.rsqrt(ss * (1.0 / HEAD_DIM) + EPS)
        qn = qt[rows] * r * gq_ref[rows, :] * (HEAD_DIM ** -0.5)
        parts = [zeros_half] * HEADS_PER_STEP
        parts[hd] = qn
        w_f32.append(jnp.concatenate(parts, axis=0))
    w_bf = [w.astype(BF16) for w in w_f32]

    kmean = kmean_prev
    blk_ids = lax.broadcasted_iota(jnp.int32, (n_blocks, blk), 0)
    for hd in range(HEADS_PER_STEP):
        gate = _dot3(kmean, w_f32[hd])
        gate = jnp.where(blk_ids < i, gate, -jnp.inf)
        sel = jnp.zeros((n_blocks, blk), jnp.bool_)
        for _ in range(MOBA_TOPK):
            mx = jnp.max(gate, axis=0, keepdims=True)
            cand = jnp.where(gate == mx, blk_ids, n_blocks)
            pick = (blk_ids == jnp.min(cand, axis=0, keepdims=True)) & (mx > -jnp.inf)
            sel = sel | pick
            gate = jnp.where(pick, -jnp.inf, gate)
        pen_ref[hd] = jnp.where(sel, 0.0, -MASK_BIG).astype(F32)

    def attend(carry, kj, j, bias_idx):
        out = []
        for hd in range(HEADS_PER_STEP):
            m, acc = carry[hd]
            s = _dot(kj, w_bf[hd])
            if bias_idx is not None:
                s = s + bias_ref[hd, bias_idx]
            pen = pen_ref[hd, pl.ds(j, 1), :]
            m_new = jnp.maximum(m, jnp.max(s, axis=0, keepdims=True) + pen)
            alpha = jnp.exp(m - m_new)
            p = jnp.exp(s - (m_new - pen)).astype(BF16)
            acc = acc * alpha + _dot(vta_ref[j, hd], p)
            out.append((m_new, acc))
        return tuple(out)

    carry = []
    for hd in range(HEADS_PER_STEP):
        s = _dot(kn_bf, w_bf[hd]) + bias_ref[hd, 0]
        m = jnp.max(s, axis=0, keepdims=True)
        p = jnp.exp(s - m).astype(BF16)
        carry.append((m, _dot(vta_ref[i, hd], p)))
    carry = tuple(carry)

    jp = jnp.maximum(i - 1, 0)
    kp = kn_ref[pl.ds(pl.multiple_of(jp * blk, blk), blk), :]
    carry = attend(carry, kp, jp, 1)

    def far_body(j, c):
        kj = kn_ref[pl.ds(pl.multiple_of(j * blk, blk), blk), :]
        return attend(c, kj, j, None)

    carry = lax.fori_loop(0, jnp.maximum(i - 1, 0), far_body, carry)

    outs = []
    for hd in range(HEADS_PER_STEP):
        _, acc = carry[hd]
        outs.append(acc[0:HEAD_DIM] / acc[HEAD_DIM:HEAD_DIM + 1])
    o_ref[0] = jnp.concatenate(outs, axis=0).T.astype(BF16)


def _attention(z, gq_t, gk_row, bias):
    b, s, _ = z.shape
    n_blocks = s // MOBA_BLOCK
    blk = MOBA_BLOCK
    return pl.pallas_call(
        functools.partial(_attn_body, n_blocks=n_blocks),
        grid=(b, N_PAIRS, n_blocks),
        in_specs=[
            pl.BlockSpec((1, blk, PAIR_W), lambda bi, hp, i: (bi, i, hp)),
            pl.BlockSpec((1, blk, PAIR_W), lambda bi, hp, i: (bi, i, N_PAIRS + hp)),
            pl.BlockSpec((1, blk, PAIR_W), lambda bi, hp, i: (bi, i, 2 * N_PAIRS + hp)),
            pl.BlockSpec((PAIR_W, blk), lambda bi, hp, i: (0, 0)),
            pl.BlockSpec((1, PAIR_W), lambda bi, hp, i: (0, 0)),
            pl.BlockSpec((HEADS_PER_STEP, 2, blk, blk), lambda bi, hp, i: (hp, 0, 0, 0)),
        ],
        out_specs=pl.BlockSpec((1, blk, PAIR_W), lambda bi, hp, i: (bi, i, hp)),
        out_shape=jax.ShapeDtypeStruct((b, s, ATTN_W), BF16),
        scratch_shapes=[
            pltpu.VMEM((s, PAIR_W), BF16),
            pltpu.VMEM((n_blocks, HEADS_PER_STEP, V_ROWS, blk), BF16),
            pltpu.VMEM((n_blocks, PAIR_W), F32),
            pltpu.VMEM((HEADS_PER_STEP, n_blocks, blk), F32),
        ],
        compiler_params=pltpu.CompilerParams(
            dimension_semantics=("arbitrary", "arbitrary", "arbitrary"), vmem_limit_bytes=VMEM_LIMIT),
        name="attn",
    )(z, z, z, gq_t, gk_row, bias)


def _sigmoid(v):
    return 1.0 / (1.0 + jnp.exp(-v))


def _mix_body(x_ref, attn_ref, p_ref, ga_ref, gp_ref, mod_ref, pw_ref, ps_ref, wba_ref, wbp_ref,
              wo_ref, g2_ref, x1_ref, h2_ref, pe_ref):
    t = pl.program_id(1)
    tm = ROW_TILE

    @pl.when(t == 0)
    def _():
        pe_ref[0:POOL_HALO, :] = jnp.zeros((POOL_HALO, POOL_W), F32)

    @pl.when(t > 0)
    def _():
        pe_ref[0:POOL_HALO, :] = pe_ref[tm:tm + POOL_HALO, :]

    pe_ref[POOL_HALO:POOL_HALO + tm, :] = p_ref[0].astype(F32)

    pos = t * tm + lax.broadcasted_iota(jnp.int32, (tm, POOL_GROUP_W), 0)
    pooled = []
    for g, win in enumerate(POOL_WINDOWS):
        cols = slice(g * POOL_GROUP_W, (g + 1) * POOL_GROUP_W)
        cur = pe_ref[POOL_HALO:POOL_HALO + tm, cols]
        tot = cur
        for dlt in range(1, win):
            tot = tot + pe_ref[POOL_HALO - dlt:POOL_HALO - dlt + tm, cols]
        count = jnp.minimum(pos + 1, win).astype(F32)
        pooled_g = (tot / count - cur).astype(BF16)
        pooled.append(_dot(pooled_g, pw_ref[g]) * ps_ref[:, cols])
    pool = jnp.concatenate(pooled, axis=-1).astype(BF16)

    a_proj = _dot(attn_ref[0], wba_ref[...])
    p_proj = _dot(pool, wbp_ref[...])
    merged = (_sigmoid(ga_ref[0].astype(F32)) * a_proj
              + _sigmoid(gp_ref[0].astype(F32)) * p_proj).astype(BF16)
    x1 = x_ref[0] + mod_ref[0, 2:3, :] * _dot(merged, wo_ref[...])
    x1_ref[0] = x1
    h2_ref[0] = _norm_mod(x1, g2_ref[...], mod_ref[0, 4:5, :], mod_ref[0, 3:4, :]).astype(BF16)


def _mix(x, attn, z, mod, pool_w_bf, pool_scale, wba_bf, wbp_bf, wo_bf, g2):
    b, s, d = x.shape
    tm = ROW_TILE
    const2 = lambda bi, t: (0, 0)
    return pl.pallas_call(
        _mix_body,
        grid=(b, s // tm),
        in_specs=[
            pl.BlockSpec((1, tm, d), lambda bi, t: (bi, t, 0)),
            pl.BlockSpec((1, tm, ATTN_W), lambda bi, t: (bi, t, 0)),
            pl.BlockSpec((1, tm, POOL_W), lambda bi, t: (bi, t, 3 * ATTN_W // POOL_W)),
            pl.BlockSpec((1, tm, d), lambda bi, t: (bi, t, (3 * ATTN_W + POOL_W) // d)),
            pl.BlockSpec((1, tm, d), lambda bi, t: (bi, t, (3 * ATTN_W + POOL_W) // d + 1)),
            pl.BlockSpec((1, 6, d), lambda bi, t: (bi, 0, 0)),
            pl.BlockSpec((POOL_GROUPS, POOL_GROUP_W, POOL_GROUP_W), lambda bi, t: (0, 0, 0)),
            pl.BlockSpec((1, POOL_W), const2),
            pl.BlockSpec((ATTN_W, d), const2),
            pl.BlockSpec((POOL_W, d), const2),
            pl.BlockSpec((d, d), const2),
            pl.BlockSpec((1, d), const2),
        ],
        out_specs=[
            pl.BlockSpec((1, tm, d), lambda bi, t: (bi, t, 0)),
            pl.BlockSpec((1, tm, d), lambda bi, t: (bi, t, 0)),
        ],
        out_shape=[
            jax.ShapeDtypeStruct((b, s, d), F32),
            jax.ShapeDtypeStruct((b, s, d), BF16),
        ],
        scratch_shapes=[pltpu.VMEM((tm + POOL_HALO, POOL_W), F32)],
        compiler_params=pltpu.CompilerParams(
            dimension_semantics=("arbitrary", "arbitrary"), vmem_limit_bytes=VMEM_LIMIT),
        name="mix",
    )(x, attn, z, z, z, mod, pool_w_bf, pool_scale, wba_bf, wbp_bf, wo_bf, g2)


def _ffn_body(x1_ref, h2_ref, mod_ref, wup_ref, cw_ref, cb_ref, wdn_ref, o_ref, hist_ref, carry_ref):
    t = pl.program_id(1)
    tm = ROW_TILE
    fc = FFN_CHUNK

    @pl.when(t == 0)
    def _():
        carry_ref[...] = jnp.zeros_like(carry_ref)

    h2 = h2_ref[0]
    acc = jnp.zeros((tm, D_MODEL), F32)
    for c in range(D_FF // fc):
        halves = []
        for half in range(2):
            cols = slice(half * D_FF + c * fc, half * D_FF + (c + 1) * fc)
            hcols = slice(half * fc, (half + 1) * fc)
            r = _dot(h2, wup_ref[:, cols])
            hist_ref[0:CONV_HALO, hcols] = carry_ref[:, cols]
            hist_ref[CONV_HALO:CONV_HALO + tm, hcols] = r
            carry_ref[:, cols] = r[tm - CONV_HALO:tm]
            u = cb_ref[:, cols] + cw_ref[CONV_W - 1:CONV_W, cols] * r
            for tap in range(CONV_W - 1):
                back = CONV_W - 1 - tap
                u = u + cw_ref[tap:tap + 1, cols] * hist_ref[CONV_HALO - back:CONV_HALO - back + tm, hcols]
            halves.append(u)
        u_g, u_v = halves
        act = (u_g * _sigmoid(u_g) * u_v).astype(BF16)
        acc = acc + _dot(act, wdn_ref[c * fc:(c + 1) * fc, :])
    o_ref[0] = x1_ref[0] + mod_ref[0, 5:6, :] * acc


def _ffn(x1, h2, mod, wup_bf, conv_w, conv_b, wdn_bf):
    b, s, d = x1.shape
    tm = ROW_TILE
    const2 = lambda bi, t: (0, 0)
    return pl.pallas_call(
        _ffn_body,
        grid=(b, s // tm),
        in_specs=[
            pl.BlockSpec((1, tm, d), lambda bi, t: (bi, t, 0)),
            pl.BlockSpec((1, tm, d), lambda bi, t: (bi, t, 0)),
            pl.BlockSpec((1, 6, d), lambda bi, t: (bi, 0, 0)),
            pl.BlockSpec((d, 2 * D_FF), const2),
            pl.BlockSpec((CONV_W, 2 * D_FF), const2),
            pl.BlockSpec((1, 2 * D_FF), const2),
            pl.BlockSpec((D_FF, d), const2),
        ],
        out_specs=pl.BlockSpec((1, tm, d), lambda bi, t: (bi, t, 0)),
        out_shape=jax.ShapeDtypeStruct((b, s, d), F32),
        scratch_shapes=[
            pltpu.VMEM((tm + CONV_HALO, 2 * FFN_CHUNK), F32),
            pltpu.VMEM((CONV_HALO, 2 * D_FF), F32),
        ],
        compiler_params=pltpu.CompilerParams(
            dimension_semantics=("arbitrary", "arbitrary"), vmem_limit_bytes=VMEM_LIMIT),
        name="ffn",
    )(x1, h2, mod, wup_bf, conv_w, conv_b, wdn_bf)


def _layer(x, c_pad, rel_bias, ada_w, ada_b, norm1_g, w_in, q_norm_g, k_norm_g, pool_w, pool_scale,
           w_branch_attn, w_branch_pool, w_out, norm2_g, w_up, conv_w, conv_b, w_down):
    b, s, d = x.shape
    mod = _ada(c_pad, ada_w, ada_b[None, :])[:b].reshape(b, 6, d)
    z = _inproj(x, mod, norm1_g[None, :], w_in.astype(BF16))
    gq_t = jnp.broadcast_to(jnp.tile(q_norm_g, HEADS_PER_STEP)[:, None], (PAIR_W, MOBA_BLOCK))
    gk_row = jnp.tile(k_norm_g, HEADS_PER_STEP)[None, :]
    attn = _attention(z, gq_t, gk_row, _biasprep(rel_bias))
    x1, h2 = _mix(x, attn, z, mod, pool_w.astype(BF16), pool_scale[None, :],
                  w_branch_attn.astype(BF16), w_branch_pool.astype(BF16), w_out.astype(BF16),
                  norm2_g[None, :])
    return _ffn(x1, h2, mod, w_up.astype(BF16), conv_w, conv_b[None, :], w_down.astype(BF16))


def kernel(x, c, ada_w, ada_b, norm1_g, w_in, q_norm_g, k_norm_g, rel_bias, pool_w, pool_scale,
           w_branch_attn, w_branch_pool, w_out, norm2_g, w_up, conv_w, conv_b, w_down):
    b, s, d = x.shape
    assert d == D_MODEL and s % MOBA_BLOCK == 0 and w_in.shape[-1] == IN_W
    c_pad = jnp.zeros((8, d), F32).at[:b].set(c)
    for l in range(ada_w.shape[0]):
        x = _layer(x, c_pad, rel_bias, ada_w[l], ada_b[l], norm1_g[l], w_in[l], q_norm_g[l],
                   k_norm_g[l], pool_w[l], pool_scale[l], w_branch_attn[l], w_branch_pool[l],
                   w_out[l], norm2_g[l], w_up[l], conv_w[l], conv_b[l], w_down[l])
    return x
```

```python
import functools
import math

import numpy as np
import jax
import jax.numpy as jnp
from jax import lax
from jax.experimental import pallas as pl
from jax.experimental.pallas import tpu as pltpu

F32 = jnp.float32
BF16 = jnp.bfloat16

D_MODEL = 1024
ATTN_HEADS = 8
HEAD_DIM = 64
ATTN_W = ATTN_HEADS * HEAD_DIM
MOBA_BLOCK = 256
MOBA_TOPK = 3
POOL_GROUPS = 4
POOL_GROUP_W = 128
POOL_W = POOL_GROUPS * POOL_GROUP_W
POOL_WINDOWS = (2, 4, 8, 16)
NUM_BUCKETS = 32
MAX_DISTANCE = 128
D_FF = 2816
CONV_W = 3
EPS = 1e-6
IN_W = 3 * ATTN_W + POOL_W + 2 * D_MODEL

HEADS_PER_STEP = 2
PAIR_W = HEADS_PER_STEP * HEAD_DIM
N_PAIRS = ATTN_HEADS // HEADS_PER_STEP
V_ROWS = HEAD_DIM + 16
LOG2E = math.log2(math.e)
FAR_GROUP = 4
MASK_BIG = 1e30
ROW_TILE = 256
POOL_HALO = 16
FFN_CHUNK = 256
CONV_HALO = 8
VMEM_LIMIT = 56 * 1024 * 1024


def _split_bf16(a):
    hi = a.astype(BF16)
    lo = (a - hi.astype(F32)).astype(BF16)
    return hi, lo


def _dot(a, b):
    return jnp.dot(a, b, preferred_element_type=F32)


def _dot3(a, b):
    ah, al = _split_bf16(a)
    bh, bl = _split_bf16(b)
    return _dot(ah, bh) + _dot(al, bh) + _dot(ah, bl)


def _ada_body(c_ref, w_ref, b_ref, o_ref):
    c = c_ref[...]
    c_act = c * (1.0 / (1.0 + jnp.exp(-c)))
    o_ref[...] = _dot3(c_act, w_ref[...]) + b_ref[...]


def _ada(c_pad, ada_w, ada_b):
    rows, d = c_pad.shape
    n = ada_w.shape[1]
    tn = 1024
    return pl.pallas_call(
        _ada_body,
        grid=(n // tn,),
        in_specs=[
            pl.BlockSpec((rows, d), lambda j: (0, 0)),
            pl.BlockSpec((d, tn), lambda j: (0, j)),
            pl.BlockSpec((1, tn), lambda j: (0, j)),
        ],
        out_specs=pl.BlockSpec((rows, tn), lambda j: (0, j)),
        out_shape=jax.ShapeDtypeStruct((rows, n), F32),
        compiler_params=pltpu.CompilerParams(
            dimension_semantics=("arbitrary",), vmem_limit_bytes=VMEM_LIMIT),
        name="ada",
    )(c_pad, ada_w, ada_b)


def _norm_mod(x, gain, scale, shift):
    ms = jnp.mean(x * x, axis=-1, keepdims=True)
    y = x * lax.rsqrt(ms + EPS) * gain
    return y * (1.0 + scale) + shift


def _inproj_body(x_ref, mod_ref, g_ref, w_ref, z_ref):
    h = _norm_mod(x_ref[0], g_ref[...], mod_ref[0, 1:2, :], mod_ref[0, 0:1, :]).astype(BF16)
    n_chunk = 1024
    for n in range(IN_W // n_chunk):
        cols = slice(n * n_chunk, (n + 1) * n_chunk)
        z_ref[0, :, cols] = _dot(h, w_ref[:, cols]).astype(BF16)


def _inproj(x, mod, g1, w_in_bf):
    b, s, d = x.shape
    return pl.pallas_call(
        _inproj_body,
        grid=(b, s // ROW_TILE),
        in_specs=[
            pl.BlockSpec((1, ROW_TILE, d), lambda bi, t: (bi, t, 0)),
            pl.BlockSpec((1, 6, d), lambda bi, t: (bi, 0, 0)),
            pl.BlockSpec((1, d), lambda bi, t: (0, 0)),
            pl.BlockSpec((d, IN_W), lambda bi, t: (0, 0)),
        ],
        out_specs=pl.BlockSpec((1, ROW_TILE, IN_W), lambda bi, t: (bi, t, 0)),
        out_shape=jax.ShapeDtypeStruct((b, s, IN_W), BF16),
        compiler_params=pltpu.CompilerParams(
            dimension_semantics=("arbitrary", "arbitrary"), vmem_limit_bytes=VMEM_LIMIT),
        name="inproj",
    )(x, mod, g1, w_in_bf)


def _t5_bucket_np(dist):
    max_exact = NUM_BUCKETS // 2
    n = np.maximum(dist, 0)
    nf = np.maximum(n, 1).astype(np.float64)
    large = max_exact + (np.log(nf / max_exact) / math.log(MAX_DISTANCE / max_exact)
                         * (NUM_BUCKETS - max_exact)).astype(np.int32)
    large = np.minimum(large, NUM_BUCKETS - 1)
    return np.where(n < max_exact, n, large).astype(np.int32)


def _bucket_tables():
    k = np.arange(MOBA_BLOCK)[:, None]
    q = np.arange(MOBA_BLOCK)[None, :]
    own = np.where(q - k >= 0, _t5_bucket_np(q - k), -1)
    prev = _t5_bucket_np(MOBA_BLOCK + q - k)
    return np.stack([own, prev]).astype(np.int32)


def _biasprep_body(rb_ref, bkt_ref, o_ref):
    h = pl.program_id(0)
    far = rb_ref[NUM_BUCKETS - 1, h]
    for t in range(2):
        bkt = bkt_ref[t]
        acc = jnp.where(bkt < 0, -MASK_BIG, 0.0).astype(F32)
        for bk in range(NUM_BUCKETS):
            acc = jnp.where(bkt == bk, (rb_ref[bk, h] - far) * LOG2E, acc)
        o_ref[0, t] = acc


def _biasprep(rel_bias):
    bkt = jnp.asarray(_bucket_tables())
    return pl.pallas_call(
        _biasprep_body,
        grid=(ATTN_HEADS,),
        in_specs=[
            pl.BlockSpec(memory_space=pltpu.SMEM),
            pl.BlockSpec((2, MOBA_BLOCK, MOBA_BLOCK), lambda h: (0, 0, 0)),
        ],
        out_specs=pl.BlockSpec((1, 2, MOBA_BLOCK, MOBA_BLOCK), lambda h: (h, 0, 0, 0)),
        out_shape=jax.ShapeDtypeStruct((ATTN_HEADS, 2, MOBA_BLOCK, MOBA_BLOCK), F32),
        compiler_params=pltpu.CompilerParams(dimension_semantics=("arbitrary",)),
        name="biasprep",
    )(rel_bias, bkt)


def _attn_body(zq_ref, zk_ref, zv_ref, gq_ref, gk_ref, bias_ref, o_ref,
               kn_ref, vta_ref, kmean_ref, pen_ref, penfar_ref, s_ref, *, n_blocks):
    i = pl.program_id(2)
    blk = MOBA_BLOCK

    @pl.when(i == 0)
    def _():
        kmean_ref[...] = jnp.zeros_like(kmean_ref)

    kraw = zk_ref[0].astype(F32)
    lane = lax.broadcasted_iota(jnp.int32, (blk, PAIR_W), 1)
    first = lane < HEAD_DIM
    sq = kraw * kraw
    ss_a = jnp.sum(jnp.where(first, sq, 0.0), axis=-1, keepdims=True)
    ss_b = jnp.sum(jnp.where(first, 0.0, sq), axis=-1, keepdims=True)
    rinv = lax.rsqrt(jnp.where(first, ss_a, ss_b) * (1.0 / HEAD_DIM) + EPS)
    kn = kraw * rinv * gk_ref[...]
    kn_bf = kn.astype(BF16)
    row0 = pl.multiple_of(i * blk, blk)
    kn_ref[pl.ds(row0, blk), :] = kn_bf
    blk_rows = lax.broadcasted_iota(jnp.int32, (n_blocks, PAIR_W), 0)
    kmean_new = jnp.mean(kn, axis=0, keepdims=True)
    kmean_prev = kmean_ref[...]
    kmean_ref[...] = jnp.where(blk_rows == i, kmean_new, kmean_prev)

    vt = zv_ref[0].astype(F32).T
    ones = jnp.ones((V_ROWS - HEAD_DIM, blk), BF16)
    for hd in range(HEADS_PER_STEP):
        vta_ref[i, hd, 0:HEAD_DIM, :] = vt[hd * HEAD_DIM:(hd + 1) * HEAD_DIM].astype(BF16)
        vta_ref[i, hd, HEAD_DIM:V_ROWS, :] = ones

    qt = zq_ref[0].astype(F32).T
    qsq = qt * qt
    zeros_half = jnp.zeros((HEAD_DIM, blk), F32)
    w_f32 = []
    for hd in range(HEADS_PER_STEP):
        rows = slice(hd * HEAD_DIM, (hd + 1) * HEAD_DIM)
        ss = jnp.sum(qsq[rows], axis=0, keepdims=True)
        r = lax.rsqrt(ss * (1.0 / HEAD_DIM) + EPS)
        qn = qt[rows] * r * gq_ref[rows, :] * (HEAD_DIM ** -0.5 * LOG2E)
        parts = [zeros_half] * HEADS_PER_STEP
        parts[hd] = qn
        w_f32.append(jnp.concatenate(parts, axis=0))
    w_bf = [w.astype(BF16) for w in w_f32]

    kmean = kmean_prev
    blk_ids = lax.broadcasted_iota(jnp.int32, (n_blocks, blk), 0)
    for hd in range(HEADS_PER_STEP):
        gate = _dot3(kmean, w_f32[hd])
        gate = jnp.where(blk_ids < i, gate, -jnp.inf)
        sel = jnp.zeros((n_blocks, blk), jnp.bool_)
        for _ in range(MOBA_TOPK):
            mx = jnp.max(gate, axis=0, keepdims=True)
            cand = jnp.where(gate == mx, blk_ids, n_blocks)
            pick = (blk_ids == jnp.min(cand, axis=0, keepdims=True)) & (mx > -jnp.inf)
            sel = sel | pick
            gate = jnp.where(pick, -jnp.inf, gate)
        pen = jnp.where(sel, 0.0, -MASK_BIG).astype(F32)
        pen_ref[hd] = pen
        penfar_ref[hd] = jnp.where(blk_ids < i - 1, pen, -MASK_BIG)

    def group_update(hd, carry, blocks):
        scores, tmax = [], []
        for kj, _, pen, bias in blocks:
            s = _dot(kj, w_bf[hd])
            if bias is not None:
                s = s + bias
            t = jnp.max(s, axis=0, keepdims=True)
            tmax.append(t if pen is None else t + pen)
            scores.append(s)
        m_new = functools.reduce(jnp.maximum, tmax)
        if carry is not None:
            m_new = jnp.maximum(m_new, carry[0])
        acc = None
        for (_, jv, pen, _), s in zip(blocks, scores):
            p = jnp.exp2(s - (m_new if pen is None else m_new - pen)).astype(BF16)
            pv = _dot(vta_ref[jv, hd], p)
            acc = pv if acc is None else acc + pv
        if carry is not None:
            acc = acc + carry[1] * jnp.exp2(carry[0] - m_new)
        return m_new, acc

    jp = jnp.maximum(i - 1, 0)
    kp = kn_ref[pl.ds(pl.multiple_of(jp * blk, blk), blk), :]
    carry = tuple(
        group_update(hd, None, [(kn_bf, i, None, bias_ref[hd, 0]),
                                (kp, jp, pen_ref[hd, pl.ds(jp, 1), :], bias_ref[hd, 1])])
        for hd in range(HEADS_PER_STEP))

    n_far = jnp.maximum(i - 1, 0)

    def far_indices(g):
        js = [g * FAR_GROUP + u for u in range(FAR_GROUP)]
        return [(jnp.minimum(j, i), jnp.minimum(j, n_blocks - 1)) for j in js]

    def far_scores(g, slot):
        idx = far_indices(g)
        keys = [kn_ref[pl.ds(pl.multiple_of(jc * blk, blk), blk), :] for jc, _ in idx]
        gmax = []
        for hd in range(HEADS_PER_STEP):
            tmax = []
            for u, (_, jpen) in enumerate(idx):
                s = _dot(keys[u], w_bf[hd])
                s_ref[slot, hd, u] = s
                tmax.append(jnp.max(s, axis=0, keepdims=True) + penfar_ref[hd, pl.ds(jpen, 1), :])
            gmax.append(functools.reduce(jnp.maximum, tmax))
        return tuple(gmax)

    def far_body(g, c):
        state, gmax = c
        slot = g % 2
        idx = far_indices(g)
        new_state = []
        for hd in range(HEADS_PER_STEP):
            m, acc = state[hd]
            m_new = jnp.maximum(m, gmax[hd])
            acc = acc * jnp.exp2(m - m_new)
            for u, (jc, jpen) in enumerate(idx):
                shift = m_new - penfar_ref[hd, pl.ds(jpen, 1), :]
                p = jnp.exp2(s_ref[slot, hd, u] - shift).astype(BF16)
                acc = acc + _dot(vta_ref[jc, hd], p)
            new_state.append((m_new, acc))
        return tuple(new_state), far_scores(g + 1, 1 - slot)

    carry, _ = lax.fori_loop(0, (n_far + FAR_GROUP - 1) // FAR_GROUP, far_body,
                             (carry, far_scores(0, 0)))

    outs = []
    for hd in range(HEADS_PER_STEP):
        _, acc = carry[hd]
        outs.append(acc[0:HEAD_DIM] / acc[HEAD_DIM:HEAD_DIM + 1])
    o_ref[0] = jnp.concatenate(outs, axis=0).T.astype(BF16)


def _attention(z, gq_t, gk_row, bias):
    b, s, _ = z.shape
    n_blocks = s // MOBA_BLOCK
    blk = MOBA_BLOCK
    return pl.pallas_call(
        functools.partial(_attn_body, n_blocks=n_blocks),
        grid=(b, N_PAIRS, n_blocks),
        in_specs=[
            pl.BlockSpec((1, blk, PAIR_W), lambda bi, hp, i: (bi, i, hp)),
            pl.BlockSpec((1, blk, PAIR_W), lambda bi, hp, i: (bi, i, N_PAIRS + hp)),
            pl.BlockSpec((1, blk, PAIR_W), lambda bi, hp, i: (bi, i, 2 * N_PAIRS + hp)),
            pl.BlockSpec((PAIR_W, blk), lambda bi, hp, i: (0, 0)),
            pl.BlockSpec((1, PAIR_W), lambda bi, hp, i: (0, 0)),
            pl.BlockSpec((HEADS_PER_STEP, 2, blk, blk), lambda bi, hp, i: (hp, 0, 0, 0)),
        ],
        out_specs=pl.BlockSpec((1, blk, PAIR_W), lambda bi, hp, i: (bi, i, hp)),
        out_shape=jax.ShapeDtypeStruct((b, s, ATTN_W), BF16),
        scratch_shapes=[
            pltpu.VMEM((s, PAIR_W), BF16),
            pltpu.VMEM((n_blocks, HEADS_PER_STEP, V_ROWS, blk), BF16),
            pltpu.VMEM((n_blocks, PAIR_W), F32),
            pltpu.VMEM((HEADS_PER_STEP, n_blocks, blk), F32),
            pltpu.VMEM((HEADS_PER_STEP, n_blocks, blk), F32),
            pltpu.VMEM((2, HEADS_PER_STEP, FAR_GROUP, blk, blk), F32),
        ],
        compiler_params=pltpu.CompilerParams(
            dimension_semantics=("arbitrary", "arbitrary", "arbitrary"), vmem_limit_bytes=VMEM_LIMIT),
        name="attn",
    )(z, z, z, gq_t, gk_row, bias)


def _sigmoid(v):
    return 1.0 / (1.0 + jnp.exp(-v))


def _mix_body(x_ref, attn_ref, p_ref, ga_ref, gp_ref, mod_ref, pw_ref, ps_ref, wba_ref, wbp_ref,
              wo_ref, g2_ref, x1_ref, h2_ref, pe_ref):
    t = pl.program_id(1)
    tm = ROW_TILE

    @pl.when(t == 0)
    def _():
        pe_ref[0:POOL_HALO, :] = jnp.zeros((POOL_HALO, POOL_W), F32)

    @pl.when(t > 0)
    def _():
        pe_ref[0:POOL_HALO, :] = pe_ref[tm:tm + POOL_HALO, :]

    pe_ref[POOL_HALO:POOL_HALO + tm, :] = p_ref[0].astype(F32)

    pos = t * tm + lax.broadcasted_iota(jnp.int32, (tm, POOL_GROUP_W), 0)
    pooled = []
    for g, win in enumerate(POOL_WINDOWS):
        cols = slice(g * POOL_GROUP_W, (g + 1) * POOL_GROUP_W)
        cur = pe_ref[POOL_HALO:POOL_HALO + tm, cols]
        tot = cur
        for dlt in range(1, win):
            tot = tot + pe_ref[POOL_HALO - dlt:POOL_HALO - dlt + tm, cols]
        count = jnp.minimum(pos + 1, win).astype(F32)
        pooled_g = (tot / count - cur).astype(BF16)
        pooled.append(_dot(pooled_g, pw_ref[g]) * ps_ref[:, cols])
    pool = jnp.concatenate(pooled, axis=-1).astype(BF16)

    a_proj = _dot(attn_ref[0], wba_ref[...])
    p_proj = _dot(pool, wbp_ref[...])
    merged = (_sigmoid(ga_ref[0].astype(F32)) * a_proj
              + _sigmoid(gp_ref[0].astype(F32)) * p_proj).astype(BF16)
    x1 = x_ref[0] + mod_ref[0, 2:3, :] * _dot(merged, wo_ref[...])
    x1_ref[0] = x1
    h2_ref[0] = _norm_mod(x1, g2_ref[...], mod_ref[0, 4:5, :], mod_ref[0, 3:4, :]).astype(BF16)


def _mix(x, attn, z, mod, pool_w_bf, pool_scale, wba_bf, wbp_bf, wo_bf, g2):
    b, s, d = x.shape
    tm = ROW_TILE
    const2 = lambda bi, t: (0, 0)
    return pl.pallas_call(
        _mix_body,
        grid=(b, s // tm),
        in_specs=[
            pl.BlockSpec((1, tm, d), lambda bi, t: (bi, t, 0)),
            pl.BlockSpec((1, tm, ATTN_W), lambda bi, t: (bi, t, 0)),
            pl.BlockSpec((1, tm, POOL_W), lambda bi, t: (bi, t, 3 * ATTN_W // POOL_W)),
            pl.BlockSpec((1, tm, d), lambda bi, t: (bi, t, (3 * ATTN_W + POOL_W) // d)),
            pl.BlockSpec((1, tm, d), lambda bi, t: (bi, t, (3 * ATTN_W + POOL_W) // d + 1)),
            pl.BlockSpec((1, 6, d), lambda bi, t: (bi, 0, 0)),
            pl.BlockSpec((POOL_GROUPS, POOL_GROUP_W, POOL_GROUP_W), lambda bi, t: (0, 0, 0)),
            pl.BlockSpec((1, POOL_W), const2),
            pl.BlockSpec((ATTN_W, d), const2),
            pl.BlockSpec((POOL_W, d), const2),
            pl.BlockSpec((d, d), const2),
            pl.BlockSpec((1, d), const2),
        ],
        out_specs=[
            pl.BlockSpec((1, tm, d), lambda bi, t: (bi, t, 0)),
            pl.BlockSpec((1, tm, d), lambda bi, t: (bi, t, 0)),
        ],
        out_shape=[
            jax.ShapeDtypeStruct((b, s, d), F32),
            jax.ShapeDtypeStruct((b, s, d), BF16),
        ],
        scratch_shapes=[pltpu.VMEM((tm + POOL_HALO, POOL_W), F32)],
        compiler_params=pltpu.CompilerParams(
            dimension_semantics=("arbitrary", "arbitrary"), vmem_limit_bytes=VMEM_LIMIT),
        name="mix",
    )(x, attn, z, z, z, mod, pool_w_bf, pool_scale, wba_bf, wbp_bf, wo_bf, g2)


def _ffn_body(x1_ref, h2_ref, mod_ref, wup_ref, cw_ref, cb_ref, wdn_ref, o_ref, hist_ref, carry_ref):
    t = pl.program_id(1)
    tm = ROW_TILE
    fc = FFN_CHUNK

    @pl.when(t == 0)
    def _():
        carry_ref[...] = jnp.zeros_like(carry_ref)

    h2 = h2_ref[0]
    acc = jnp.zeros((tm, D_MODEL), F32)
    for c in range(D_FF // fc):
        halves = []
        for half in range(2):
            cols = slice(half * D_FF + c * fc, half * D_FF + (c + 1) * fc)
            hcols = slice(half * fc, (half + 1) * fc)
            r = _dot(h2, wup_ref[:, cols])
            hist_ref[0:CONV_HALO, hcols] = carry_ref[:, cols]
            hist_ref[CONV_HALO:CONV_HALO + tm, hcols] = r
            carry_ref[:, cols] = r[tm - CONV_HALO:tm]
            u = cb_ref[:, cols] + cw_ref[CONV_W - 1:CONV_W, cols] * r
            for tap in range(CONV_W - 1):
                back = CONV_W - 1 - tap
                u = u + cw_ref[tap:tap + 1, cols] * hist_ref[CONV_HALO - back:CONV_HALO - back + tm, hcols]
            halves.append(u)
        u_g, u_v = halves
        act = (u_g * _sigmoid(u_g) * u_v).astype(BF16)
        acc = acc + _dot(act, wdn_ref[c * fc:(c + 1) * fc, :])
    o_ref[0] = x1_ref[0] + mod_ref[0, 5:6, :] * acc


def _ffn(x1, h2, mod, wup_bf, conv_w, conv_b, wdn_bf):
    b, s, d = x1.shape
    tm = ROW_TILE
    const2 = lambda bi, t: (0, 0)
    return pl.pallas_call(
        _ffn_body,
        grid=(b, s // tm),
        in_specs=[
            pl.BlockSpec((1, tm, d), lambda bi, t: (bi, t, 0)),
            pl.BlockSpec((1, tm, d), lambda bi, t: (bi, t, 0)),
            pl.BlockSpec((1, 6, d), lambda bi, t: (bi, 0, 0)),
            pl.BlockSpec((d, 2 * D_FF), const2),
            pl.BlockSpec((CONV_W, 2 * D_FF), const2),
            pl.BlockSpec((1, 2 * D_FF), const2),
            pl.BlockSpec((D_FF, d), const2),
        ],
        out_specs=pl.BlockSpec((1, tm, d), lambda bi, t: (bi, t, 0)),
        out_shape=jax.ShapeDtypeStruct((b, s, d), F32),
        scratch_shapes=[
            pltpu.VMEM((tm + CONV_HALO, 2 * FFN_CHUNK), F32),
            pltpu.VMEM((CONV_HALO, 2 * D_FF), F32),
        ],
        compiler_params=pltpu.CompilerParams(
            dimension_semantics=("arbitrary", "arbitrary"), vmem_limit_bytes=VMEM_LIMIT),
        name="ffn",
    )(x1, h2, mod, wup_bf, conv_w, conv_b, wdn_bf)


def _layer(x, c_pad, rel_bias, ada_w, ada_b, norm1_g, w_in, q_norm_g, k_norm_g, pool_w, pool_scale,
           w_branch_attn, w_branch_pool, w_out, norm2_g, w_up, conv_w, conv_b, w_down):
    b, s, d = x.shape
    mod = _ada(c_pad, ada_w, ada_b[None, :])[:b].reshape(b, 6, d)
    z = _inproj(x, mod, norm1_g[None, :], w_in.astype(BF16))
    gq_t = jnp.broadcast_to(jnp.tile(q_norm_g, HEADS_PER_STEP)[:, None], (PAIR_W, MOBA_BLOCK))
    gk_row = jnp.tile(k_norm_g, HEADS_PER_STEP)[None, :]
    attn = _attention(z, gq_t, gk_row, _biasprep(rel_bias))
    x1, h2 = _mix(x, attn, z, mod, pool_w.astype(BF16), pool_scale[None, :],
                  w_branch_attn.astype(BF16), w_branch_pool.astype(BF16), w_out.astype(BF16),
                  norm2_g[None, :])
    return _ffn(x1, h2, mod, w_up.astype(BF16), conv_w, conv_b[None, :], w_down.astype(BF16))


def kernel(x, c, ada_w, ada_b, norm1_g, w_in, q_norm_g, k_norm_g, rel_bias, pool_w, pool_scale,
           w_branch_attn, w_branch_pool, w_out, norm2_g, w_up, conv_w, conv_b, w_down):
    b, s, d = x.shape
    assert d == D_MODEL and s % MOBA_BLOCK == 0 and w_in.shape[-1] == IN_W
    c_pad = jnp.zeros((8, d), F32).at[:b].set(c)
    for l in range(ada_w.shape[0]):
        x = _layer(x, c_pad, rel_bias, ada_w[l], ada_b[l], norm1_g[l], w_in[l], q_norm_g[l],
                   k_norm_g[l], pool_w[l], pool_scale[l], w_branch_attn[l], w_branch_pool[l],
                   w_out[l], norm2_g[l], w_up[l], conv_w[l], conv_b[l], w_down[l])
    return x
```

```python
import functools
import math

import numpy as np
import jax
import jax.numpy as jnp
from jax import lax
from jax.experimental import pallas as pl
from jax.experimental.pallas import tpu as pltpu

F32 = jnp.float32
BF16 = jnp.bfloat16

D_MODEL = 1024
ATTN_HEADS = 8
HEAD_DIM = 64
ATTN_W = ATTN_HEADS * HEAD_DIM
MOBA_BLOCK = 256
MOBA_TOPK = 3
POOL_GROUPS = 4
POOL_GROUP_W = 128
POOL_W = POOL_GROUPS * POOL_GROUP_W
POOL_WINDOWS = (2, 4, 8, 16)
NUM_BUCKETS = 32
MAX_DISTANCE = 128
D_FF = 2816
CONV_W = 3
EPS = 1e-6
IN_W = 3 * ATTN_W + POOL_W + 2 * D_MODEL

HEADS_PER_STEP = 2
PAIR_W = HEADS_PER_STEP * HEAD_DIM
N_PAIRS = ATTN_HEADS // HEADS_PER_STEP
V_ROWS = HEAD_DIM + 16
LOG2E = math.log2(math.e)
MASK_BIG = 1e30
ROW_TILE = 256
POOL_HALO = 16
FFN_CHUNK = 256
CONV_HALO = 8
VMEM_LIMIT = 56 * 1024 * 1024


def _split_bf16(a):
    hi = a.astype(BF16)
    lo = (a - hi.astype(F32)).astype(BF16)
    return hi, lo


def _dot(a, b):
    return jnp.dot(a, b, preferred_element_type=F32)


def _dot3(a, b):
    ah, al = _split_bf16(a)
    bh, bl = _split_bf16(b)
    return _dot(ah, bh) + _dot(al, bh) + _dot(ah, bl)


def _ada_body(c_ref, w_ref, b_ref, o_ref):
    c = c_ref[...]
    c_act = c * (1.0 / (1.0 + jnp.exp(-c)))
    o_ref[...] = _dot3(c_act, w_ref[...]) + b_ref[...]


def _ada(c_pad, ada_w, ada_b):
    rows, d = c_pad.shape
    n = ada_w.shape[1]
    tn = 1024
    return pl.pallas_call(
        _ada_body,
        grid=(n // tn,),
        in_specs=[
            pl.BlockSpec((rows, d), lambda j: (0, 0)),
            pl.BlockSpec((d, tn), lambda j: (0, j)),
            pl.BlockSpec((1, tn), lambda j: (0, j)),
        ],
        out_specs=pl.BlockSpec((rows, tn), lambda j: (0, j)),
        out_shape=jax.ShapeDtypeStruct((rows, n), F32),
        compiler_params=pltpu.CompilerParams(
            dimension_semantics=("arbitrary",), vmem_limit_bytes=VMEM_LIMIT),
        name="ada",
    )(c_pad, ada_w, ada_b)


def _norm_mod(x, gain, scale, shift):
    ms = jnp.mean(x * x, axis=-1, keepdims=True)
    y = x * lax.rsqrt(ms + EPS) * gain
    return y * (1.0 + scale) + shift


def _inproj_body(x_ref, mod_ref, g_ref, w_ref, z_ref):
    h = _norm_mod(x_ref[0], g_ref[...], mod_ref[0, 1:2, :], mod_ref[0, 0:1, :]).astype(BF16)
    n_chunk = 1024
    for n in range(IN_W // n_chunk):
        cols = slice(n * n_chunk, (n + 1) * n_chunk)
        z_ref[0, :, cols] = _dot(h, w_ref[:, cols]).astype(BF16)


def _inproj(x, mod, g1, w_in_bf):
    b, s, d = x.shape
    return pl.pallas_call(
        _inproj_body,
        grid=(b, s // ROW_TILE),
        in_specs=[
            pl.BlockSpec((1, ROW_TILE, d), lambda bi, t: (bi, t, 0)),
            pl.BlockSpec((1, 6, d), lambda bi, t: (bi, 0, 0)),
            pl.BlockSpec((1, d), lambda bi, t: (0, 0)),
            pl.BlockSpec((d, IN_W), lambda bi, t: (0, 0)),
        ],
        out_specs=pl.BlockSpec((1, ROW_TILE, IN_W), lambda bi, t: (bi, t, 0)),
        out_shape=jax.ShapeDtypeStruct((b, s, IN_W), BF16),
        compiler_params=pltpu.CompilerParams(
            dimension_semantics=("arbitrary", "arbitrary"), vmem_limit_bytes=VMEM_LIMIT),
        name="inproj",
    )(x, mod, g1, w_in_bf)


def _t5_bucket_np(dist):
    max_exact = NUM_BUCKETS // 2
    n = np.maximum(dist, 0)
    nf = np.maximum(n, 1).astype(np.float64)
    large = max_exact + (np.log(nf / max_exact) / math.log(MAX_DISTANCE / max_exact)
                         * (NUM_BUCKETS - max_exact)).astype(np.int32)
    large = np.minimum(large, NUM_BUCKETS - 1)
    return np.where(n < max_exact, n, large).astype(np.int32)


def _bucket_tables():
    k = np.arange(MOBA_BLOCK)[:, None]
    q = np.arange(MOBA_BLOCK)[None, :]
    own = np.where(q - k >= 0, _t5_bucket_np(q - k), -1)
    prev = _t5_bucket_np(MOBA_BLOCK + q - k)
    return np.stack([own, prev]).astype(np.int32)


def _biasprep_body(rb_ref, bkt_ref, o_ref):
    h = pl.program_id(0)
    far = rb_ref[NUM_BUCKETS - 1, h]
    for t in range(2):
        bkt = bkt_ref[t]
        acc = jnp.where(bkt < 0, -MASK_BIG, 0.0).astype(F32)
        for bk in range(NUM_BUCKETS):
            acc = jnp.where(bkt == bk, (rb_ref[bk, h] - far) * LOG2E, acc)
        o_ref[0, t] = acc


def _biasprep(rel_bias):
    bkt = jnp.asarray(_bucket_tables())
    return pl.pallas_call(
        _biasprep_body,
        grid=(ATTN_HEADS,),
        in_specs=[
            pl.BlockSpec(memory_space=pltpu.SMEM),
            pl.BlockSpec((2, MOBA_BLOCK, MOBA_BLOCK), lambda h: (0, 0, 0)),
        ],
        out_specs=pl.BlockSpec((1, 2, MOBA_BLOCK, MOBA_BLOCK), lambda h: (h, 0, 0, 0)),
        out_shape=jax.ShapeDtypeStruct((ATTN_HEADS, 2, MOBA_BLOCK, MOBA_BLOCK), F32),
        compiler_params=pltpu.CompilerParams(dimension_semantics=("arbitrary",)),
        name="biasprep",
    )(rel_bias, bkt)


ITEM_FIELDS = 5


def _far_items(n_blocks):
    never = n_blocks - 1
    items = []
    for i in range(n_blocks):
        n_far = max(i - 1, 0)
        for j0 in range(0, n_far, 2):
            items.append((i, j0, j0, j0 + 1, j0 + 1) if j0 + 1 < n_far else (i, j0, j0, j0, never))
    dummy = (n_blocks - 1, 0, never, 0, never)
    if len(items) % 2:
        items.append(dummy)
    items.append(dummy)
    return np.asarray(items, np.int32)


def _attn_body(tbl_ref, zq_ref, zk_ref, zv_ref, gq_ref, gk_ref, bias_ref, o_ref,
               kn_ref, vta_ref, w_ref, pen_ref, acc_ref, m_ref, s0_ref, s1_ref, *, n_blocks, n_items):
    blk = MOBA_BLOCK
    heads = range(HEADS_PER_STEP)

    def rows_of(j):
        return pl.ds(pl.multiple_of(j * blk, blk), blk)

    lane = lax.broadcasted_iota(jnp.int32, (blk, PAIR_W), 1)
    first = lane < HEAD_DIM
    blk_rows = lax.broadcasted_iota(jnp.int32, (n_blocks, PAIR_W), 0)
    ones = jnp.ones((V_ROWS - HEAD_DIM, blk), BF16)

    def kv_prep(i, kmean):
        kraw = zk_ref[0, rows_of(i), :].astype(F32)
        sq = kraw * kraw
        ss_a = jnp.sum(jnp.where(first, sq, 0.0), axis=-1, keepdims=True)
        ss_b = jnp.sum(jnp.where(first, 0.0, sq), axis=-1, keepdims=True)
        rinv = lax.rsqrt(jnp.where(first, ss_a, ss_b) * (1.0 / HEAD_DIM) + EPS)
        kn = kraw * rinv * gk_ref[...]
        kn_ref[rows_of(i), :] = kn.astype(BF16)
        vt = zv_ref[0, rows_of(i), :].astype(F32).T
        for hd in heads:
            vta_ref[i, hd, 0:HEAD_DIM, :] = vt[hd * HEAD_DIM:(hd + 1) * HEAD_DIM].astype(BF16)
            vta_ref[i, hd, HEAD_DIM:V_ROWS, :] = ones
        return jnp.where(blk_rows == i, jnp.mean(kn, axis=0, keepdims=True), kmean)

    kmean = lax.fori_loop(0, n_blocks, kv_prep, jnp.zeros((n_blocks, PAIR_W), F32))

    blk_ids = lax.broadcasted_iota(jnp.int32, (n_blocks, blk), 0)
    zeros_half = jnp.zeros((HEAD_DIM, blk), F32)

    def q_prep(i, _):
        qt = zq_ref[0, rows_of(i), :].astype(F32).T
        qsq = qt * qt
        for hd in heads:
            rows = slice(hd * HEAD_DIM, (hd + 1) * HEAD_DIM)
            ss = jnp.sum(qsq[rows], axis=0, keepdims=True)
            r = lax.rsqrt(ss * (1.0 / HEAD_DIM) + EPS)
            qn = qt[rows] * r * gq_ref[rows, :] * (HEAD_DIM ** -0.5 * LOG2E)
            parts = [zeros_half] * HEADS_PER_STEP
            parts[hd] = qn
            w = jnp.concatenate(parts, axis=0)
            w_ref[i, hd] = w.astype(BF16)
            gate = jnp.where(blk_ids < i, _dot3(kmean, w), -jnp.inf)
            sel = jnp.zeros((n_blocks, blk), jnp.bool_)
            for _ in range(MOBA_TOPK):
                mx = jnp.max(gate, axis=0, keepdims=True)
                cand = jnp.where(gate == mx, blk_ids, n_blocks)
                pick = (blk_ids == jnp.min(cand, axis=0, keepdims=True)) & (mx > -jnp.inf)
                sel = sel | pick
                gate = jnp.where(pick, -jnp.inf, gate)
            pen_ref[hd, i] = jnp.where(sel, 0.0, -MASK_BIG).astype(F32)
        return 0

    lax.fori_loop(0, n_blocks, q_prep, 0)

    def score(i, blocks, s_ref, biased):
        keys = [kn_ref[rows_of(j), :] for j, _ in blocks]
        gmax = []
        for hd in heads:
            w = w_ref[i, hd]
            tmax = []
            for u, (_, jpen) in enumerate(blocks):
                s = _dot(keys[u], w)
                if biased:
                    s = s + bias_ref[hd, u]
                s_ref[hd, u] = s
                t = jnp.max(s, axis=0, keepdims=True)
                tmax.append(t if jpen is None else t + pen_ref[hd, i, pl.ds(jpen, 1), :])
            gmax.append(functools.reduce(jnp.maximum, tmax))
        return tuple(gmax)

    def attend(i, blocks, s_ref, gmax, first_visit):
        for hd in heads:
            if first_visit:
                m_new = gmax[hd]
                acc = None
            else:
                m_old = m_ref[i, hd]
                m_new = jnp.maximum(m_old, gmax[hd])
                acc = acc_ref[i, hd] * jnp.exp2(m_old - m_new)
            for u, (j, jpen) in enumerate(blocks):
                shift = m_new if jpen is None else m_new - pen_ref[hd, i, pl.ds(jpen, 1), :]
                p = jnp.exp2(s_ref[hd, u] - shift).astype(BF16)
                pv = _dot(vta_ref[j, hd], p)
                acc = pv if acc is None else acc + pv
            acc_ref[i, hd] = acc
            m_ref[i, hd] = m_new

    def near_blocks(i):
        jp = jnp.maximum(i - 1, 0)
        return ((i, None), (jp, jp))

    def near_body(t, g0):
        i0 = 2 * t
        i1 = i0 + 1
        i2 = jnp.minimum(i0 + 2, n_blocks - 1)
        g1 = score(i1, near_blocks(i1), s1_ref, True)
        attend(i0, near_blocks(i0), s0_ref, g0, True)
        g2 = score(i2, near_blocks(i2), s0_ref, True)
        attend(i1, near_blocks(i1), s1_ref, g1, True)
        return g2

    lax.fori_loop(0, n_blocks // 2, near_body, score(0, near_blocks(0), s0_ref, True))

    def item(k):
        base = k * ITEM_FIELDS
        i = tbl_ref[base]
        return i, ((tbl_ref[base + 1], tbl_ref[base + 2]), (tbl_ref[base + 3], tbl_ref[base + 4]))

    def far_body(t, g0):
        k0 = 2 * t
        g1 = score(*item(k0 + 1), s1_ref, False)
        attend(*item(k0), s0_ref, g0, False)
        g2 = score(*item(k0 + 2), s0_ref, False)
        attend(*item(k0 + 1), s1_ref, g1, False)
        return g2

    lax.fori_loop(0, n_items // 2, far_body, score(*item(0), s0_ref, False))

    def finish(i, _):
        outs = []
        for hd in heads:
            acc = acc_ref[i, hd]
            outs.append(acc[0:HEAD_DIM] / acc[HEAD_DIM:HEAD_DIM + 1])
        o_ref[0, rows_of(i), :] = jnp.concatenate(outs, axis=0).T.astype(BF16)
        return 0

    lax.fori_loop(0, n_blocks, finish, 0)


def _attention(z, gq_t, gk_row, bias):
    b, s, _ = z.shape
    n_blocks = s // MOBA_BLOCK
    assert n_blocks % 2 == 0
    blk = MOBA_BLOCK
    items = _far_items(n_blocks)
    n_items = items.shape[0] - 1
    grid_spec = pltpu.PrefetchScalarGridSpec(
        num_scalar_prefetch=1,
        grid=(b, N_PAIRS),
        in_specs=[
            pl.BlockSpec((1, s, PAIR_W), lambda bi, hp, tbl: (bi, 0, hp)),
            pl.BlockSpec((1, s, PAIR_W), lambda bi, hp, tbl: (bi, 0, N_PAIRS + hp)),
            pl.BlockSpec((1, s, PAIR_W), lambda bi, hp, tbl: (bi, 0, 2 * N_PAIRS + hp)),
            pl.BlockSpec((PAIR_W, blk), lambda bi, hp, tbl: (0, 0)),
            pl.BlockSpec((1, PAIR_W), lambda bi, hp, tbl: (0, 0)),
            pl.BlockSpec((HEADS_PER_STEP, 2, blk, blk), lambda bi, hp, tbl: (hp, 0, 0, 0)),
        ],
        out_specs=pl.BlockSpec((1, s, PAIR_W), lambda bi, hp, tbl: (bi, 0, hp)),
        scratch_shapes=[
            pltpu.VMEM((s, PAIR_W), BF16),
            pltpu.VMEM((n_blocks, HEADS_PER_STEP, V_ROWS, blk), BF16),
            pltpu.VMEM((n_blocks, HEADS_PER_STEP, PAIR_W, blk), BF16),
            pltpu.VMEM((HEADS_PER_STEP, n_blocks, n_blocks, blk), F32),
            pltpu.VMEM((n_blocks, HEADS_PER_STEP, V_ROWS, blk), F32),
            pltpu.VMEM((n_blocks, HEADS_PER_STEP, 1, blk), F32),
            pltpu.VMEM((HEADS_PER_STEP, 2, blk, blk), F32),
            pltpu.VMEM((HEADS_PER_STEP, 2, blk, blk), F32),
        ],
    )
    return pl.pallas_call(
        functools.partial(_attn_body, n_blocks=n_blocks, n_items=n_items),
        grid_spec=grid_spec,
        out_shape=jax.ShapeDtypeStruct((b, s, ATTN_W), BF16),
        compiler_params=pltpu.CompilerParams(
            dimension_semantics=("arbitrary", "arbitrary"), vmem_limit_bytes=VMEM_LIMIT),
        name="attn",
    )(jnp.asarray(items.reshape(-1)), z, z, z, gq_t, gk_row, bias)


def _sigmoid(v):
    return 1.0 / (1.0 + jnp.exp(-v))


def _mix_body(x_ref, attn_ref, p_ref, ga_ref, gp_ref, mod_ref, pw_ref, ps_ref, wba_ref, wbp_ref,
              wo_ref, g2_ref, x1_ref, h2_ref, pe_ref):
    t = pl.program_id(1)
    tm = ROW_TILE

    @pl.when(t == 0)
    def _():
        pe_ref[0:POOL_HALO, :] = jnp.zeros((POOL_HALO, POOL_W), F32)

    @pl.when(t > 0)
    def _():
        pe_ref[0:POOL_HALO, :] = pe_ref[tm:tm + POOL_HALO, :]

    pe_ref[POOL_HALO:POOL_HALO + tm, :] = p_ref[0].astype(F32)

    pos = t * tm + lax.broadcasted_iota(jnp.int32, (tm, POOL_GROUP_W), 0)
    pooled = []
    for g, win in enumerate(POOL_WINDOWS):
        cols = slice(g * POOL_GROUP_W, (g + 1) * POOL_GROUP_W)
        cur = pe_ref[POOL_HALO:POOL_HALO + tm, cols]
        tot = cur
        for dlt in range(1, win):
            tot = tot + pe_ref[POOL_HALO - dlt:POOL_HALO - dlt + tm, cols]
        count = jnp.minimum(pos + 1, win).astype(F32)
        pooled_g = (tot / count - cur).astype(BF16)
        pooled.append(_dot(pooled_g, pw_ref[g]) * ps_ref[:, cols])
    pool = jnp.concatenate(pooled, axis=-1).astype(BF16)

    a_proj = _dot(attn_ref[0], wba_ref[...])
    p_proj = _dot(pool, wbp_ref[...])
    merged = (_sigmoid(ga_ref[0].astype(F32)) * a_proj
              + _sigmoid(gp_ref[0].astype(F32)) * p_proj).astype(BF16)
    x1 = x_ref[0] + mod_ref[0, 2:3, :] * _dot(merged, wo_ref[...])
    x1_ref[0] = x1
    h2_ref[0] = _norm_mod(x1, g2_ref[...], mod_ref[0, 4:5, :], mod_ref[0, 3:4, :]).astype(BF16)


def _mix(x, attn, z, mod, pool_w_bf, pool_scale, wba_bf, wbp_bf, wo_bf, g2):
    b, s, d = x.shape
    tm = ROW_TILE
    const2 = lambda bi, t: (0, 0)
    return pl.pallas_call(
        _mix_body,
        grid=(b, s // tm),
        in_specs=[
            pl.BlockSpec((1, tm, d), lambda bi, t: (bi, t, 0)),
            pl.BlockSpec((1, tm, ATTN_W), lambda bi, t: (bi, t, 0)),
            pl.BlockSpec((1, tm, POOL_W), lambda bi, t: (bi, t, 3 * ATTN_W // POOL_W)),
            pl.BlockSpec((1, tm, d), lambda bi, t: (bi, t, (3 * ATTN_W + POOL_W) // d)),
            pl.BlockSpec((1, tm, d), lambda bi, t: (bi, t, (3 * ATTN_W + POOL_W) // d + 1)),
            pl.BlockSpec((1, 6, d), lambda bi, t: (bi, 0, 0)),
            pl.BlockSpec((POOL_GROUPS, POOL_GROUP_W, POOL_GROUP_W), lambda bi, t: (0, 0, 0)),
            pl.BlockSpec((1, POOL_W), const2),
            pl.BlockSpec((ATTN_W, d), const2),
            pl.BlockSpec((POOL_W, d), const2),
            pl.BlockSpec((d, d), const2),
            pl.BlockSpec((1, d), const2),
        ],
        out_specs=[
            pl.BlockSpec((1, tm, d), lambda bi, t: (bi, t, 0)),
            pl.BlockSpec((1, tm, d), lambda bi, t: (bi, t, 0)),
        ],
        out_shape=[
            jax.ShapeDtypeStruct((b, s, d), F32),
            jax.ShapeDtypeStruct((b, s, d), BF16),
        ],
        scratch_shapes=[pltpu.VMEM((tm + POOL_HALO, POOL_W), F32)],
        compiler_params=pltpu.CompilerParams(
            dimension_semantics=("arbitrary", "arbitrary"), vmem_limit_bytes=VMEM_LIMIT),
        name="mix",
    )(x, attn, z, z, z, mod, pool_w_bf, pool_scale, wba_bf, wbp_bf, wo_bf, g2)


def _ffn_body(x1_ref, h2_ref, mod_ref, wup_ref, cw_ref, cb_ref, wdn_ref, o_ref, hist_ref, carry_ref):
    t = pl.program_id(1)
    tm = ROW_TILE
    fc = FFN_CHUNK

    @pl.when(t == 0)
    def _():
        carry_ref[...] = jnp.zeros_like(carry_ref)

    h2 = h2_ref[0]
    acc = jnp.zeros((tm, D_MODEL), F32)
    for c in range(D_FF // fc):
        halves = []
        for half in range(2):
            cols = slice(half * D_FF + c * fc, half * D_FF + (c + 1) * fc)
            hcols = slice(half * fc, (half + 1) * fc)
            r = _dot(h2, wup_ref[:, cols])
            hist_ref[0:CONV_HALO, hcols] = carry_ref[:, cols]
            hist_ref[CONV_HALO:CONV_HALO + tm, hcols] = r
            carry_ref[:, cols] = r[tm - CONV_HALO:tm]
            u = cb_ref[:, cols] + cw_ref[CONV_W - 1:CONV_W, cols] * r
            for tap in range(CONV_W - 1):
                back = CONV_W - 1 - tap
                u = u + cw_ref[tap:tap + 1, cols] * hist_ref[CONV_HALO - back:CONV_HALO - back + tm, hcols]
            halves.append(u)
        u_g, u_v = halves
        act = (u_g * _sigmoid(u_g) * u_v).astype(BF16)
        acc = acc + _dot(act, wdn_ref[c * fc:(c + 1) * fc, :])
    o_ref[0] = x1_ref[0] + mod_ref[0, 5:6, :] * acc


def _ffn(x1, h2, mod, wup_bf, conv_w, conv_b, wdn_bf):
    b, s, d = x1.shape
    tm = ROW_TILE
    const2 = lambda bi, t: (0, 0)
    return pl.pallas_call(
        _ffn_body,
        grid=(b, s // tm),
        in_specs=[
            pl.BlockSpec((1, tm, d), lambda bi, t: (bi, t, 0)),
            pl.BlockSpec((1, tm, d), lambda bi, t: (bi, t, 0)),
            pl.BlockSpec((1, 6, d), lambda bi, t: (bi, 0, 0)),
            pl.BlockSpec((d, 2 * D_FF), const2),
            pl.BlockSpec((CONV_W, 2 * D_FF), const2),
            pl.BlockSpec((1, 2 * D_FF), const2),
            pl.BlockSpec((D_FF, d), const2),
        ],
        out_specs=pl.BlockSpec((1, tm, d), lambda bi, t: (bi, t, 0)),
        out_shape=jax.ShapeDtypeStruct((b, s, d), F32),
        scratch_shapes=[
            pltpu.VMEM((tm + CONV_HALO, 2 * FFN_CHUNK), F32),
            pltpu.VMEM((CONV_HALO, 2 * D_FF), F32),
        ],
        compiler_params=pltpu.CompilerParams(
            dimension_semantics=("arbitrary", "arbitrary"), vmem_limit_bytes=VMEM_LIMIT),
        name="ffn",
    )(x1, h2, mod, wup_bf, conv_w, conv_b, wdn_bf)


def _layer(x, c_pad, rel_bias, ada_w, ada_b, norm1_g, w_in, q_norm_g, k_norm_g, pool_w, pool_scale,
           w_branch_attn, w_branch_pool, w_out, norm2_g, w_up, conv_w, conv_b, w_down):
    b, s, d = x.shape
    mod = _ada(c_pad, ada_w, ada_b[None, :])[:b].reshape(b, 6, d)
    z = _inproj(x, mod, norm1_g[None, :], w_in.astype(BF16))
    gq_t = jnp.broadcast_to(jnp.tile(q_norm_g, HEADS_PER_STEP)[:, None], (PAIR_W, MOBA_BLOCK))
    gk_row = jnp.tile(k_norm_g, HEADS_PER_STEP)[None, :]
    attn = _attention(z, gq_t, gk_row, _biasprep(rel_bias))
    x1, h2 = _mix(x, attn, z, mod, pool_w.astype(BF16), pool_scale[None, :],
                  w_branch_attn.astype(BF16), w_branch_pool.astype(BF16), w_out.astype(BF16),
                  norm2_g[None, :])
    return _ffn(x1, h2, mod, w_up.astype(BF16), conv_w, conv_b[None, :], w_down.astype(BF16))


def kernel(x, c, ada_w, ada_b, norm1_g, w_in, q_norm_g, k_norm_g, rel_bias, pool_w, pool_scale,
           w_branch_attn, w_branch_pool, w_out, norm2_g, w_up, conv_w, conv_b, w_down):
    b, s, d = x.shape
    assert d == D_MODEL and s % MOBA_BLOCK == 0 and w_in.shape[-1] == IN_W
    c_pad = jnp.zeros((8, d), F32).at[:b].set(c)
    for l in range(ada_w.shape[0]):
        x = _layer(x, c_pad, rel_bias, ada_w[l], ada_b[l], norm1_g[l], w_in[l], q_norm_g[l],
                   k_norm_g[l], pool_w[l], pool_scale[l], w_branch_attn[l], w_branch_pool[l],
                   w_out[l], norm2_g[l], w_up[l], conv_w[l], conv_b[l], w_down[l])
    return x
```

```python
import functools
import math

import numpy as np
import jax
import jax.numpy as jnp
from jax import lax
from jax.experimental import pallas as pl
from jax.experimental.pallas import tpu as pltpu

F32 = jnp.float32
BF16 = jnp.bfloat16

D_MODEL = 1024
ATTN_HEADS = 8
HEAD_DIM = 64
ATTN_W = ATTN_HEADS * HEAD_DIM
MOBA_BLOCK = 256
MOBA_TOPK = 3
POOL_GROUPS = 4
POOL_GROUP_W = 128
POOL_W = POOL_GROUPS * POOL_GROUP_W
POOL_WINDOWS = (2, 4, 8, 16)
NUM_BUCKETS = 32
MAX_DISTANCE = 128
D_FF = 2816
CONV_W = 3
EPS = 1e-6
IN_W = 3 * ATTN_W + POOL_W + 2 * D_MODEL

HEADS_PER_STEP = 2
PAIR_W = HEADS_PER_STEP * HEAD_DIM
N_PAIRS = ATTN_HEADS // HEADS_PER_STEP
V_ROWS = HEAD_DIM + 16
LOG2E = math.log2(math.e)
MASK_BIG = 1e30
ROW_TILE = 256
POOL_HALO = 16
FFN_CHUNK = 256
CONV_HALO = 8
VMEM_LIMIT = 56 * 1024 * 1024


def _split_bf16(a):
    hi = a.astype(BF16)
    lo = (a - hi.astype(F32)).astype(BF16)
    return hi, lo


def _dot(a, b):
    return jnp.dot(a, b, preferred_element_type=F32)


def _dot3(a, b):
    ah, al = _split_bf16(a)
    bh, bl = _split_bf16(b)
    return _dot(ah, bh) + _dot(al, bh) + _dot(ah, bl)


def _ada_body(c_ref, w_ref, b_ref, o_ref):
    c = c_ref[...]
    c_act = c * (1.0 / (1.0 + jnp.exp(-c)))
    o_ref[...] = _dot3(c_act, w_ref[...]) + b_ref[...]


def _ada(c_pad, ada_w, ada_b):
    rows, d = c_pad.shape
    n = ada_w.shape[1]
    tn = 1024
    return pl.pallas_call(
        _ada_body,
        grid=(n // tn,),
        in_specs=[
            pl.BlockSpec((rows, d), lambda j: (0, 0)),
            pl.BlockSpec((d, tn), lambda j: (0, j)),
            pl.BlockSpec((1, tn), lambda j: (0, j)),
        ],
        out_specs=pl.BlockSpec((rows, tn), lambda j: (0, j)),
        out_shape=jax.ShapeDtypeStruct((rows, n), F32),
        compiler_params=pltpu.CompilerParams(
            dimension_semantics=("arbitrary",), vmem_limit_bytes=VMEM_LIMIT),
        name="ada",
    )(c_pad, ada_w, ada_b)


def _norm_mod(x, gain, scale, shift):
    ms = jnp.mean(x * x, axis=-1, keepdims=True)
    y = x * lax.rsqrt(ms + EPS) * gain
    return y * (1.0 + scale) + shift


def _inproj_body(x_ref, mod_ref, g_ref, w_ref, z_ref):
    h = _norm_mod(x_ref[0], g_ref[...], mod_ref[0, 1:2, :], mod_ref[0, 0:1, :]).astype(BF16)
    n_chunk = 1024
    for n in range(IN_W // n_chunk):
        cols = slice(n * n_chunk, (n + 1) * n_chunk)
        z_ref[0, :, cols] = _dot(h, w_ref[:, cols]).astype(BF16)


def _inproj(x, mod, g1, w_in_bf):
    b, s, d = x.shape
    return pl.pallas_call(
        _inproj_body,
        grid=(b, s // ROW_TILE),
        in_specs=[
            pl.BlockSpec((1, ROW_TILE, d), lambda bi, t: (bi, t, 0)),
            pl.BlockSpec((1, 6, d), lambda bi, t: (bi, 0, 0)),
            pl.BlockSpec((1, d), lambda bi, t: (0, 0)),
            pl.BlockSpec((d, IN_W), lambda bi, t: (0, 0)),
        ],
        out_specs=pl.BlockSpec((1, ROW_TILE, IN_W), lambda bi, t: (bi, t, 0)),
        out_shape=jax.ShapeDtypeStruct((b, s, IN_W), BF16),
        compiler_params=pltpu.CompilerParams(
            dimension_semantics=("arbitrary", "arbitrary"), vmem_limit_bytes=VMEM_LIMIT),
        name="inproj",
    )(x, mod, g1, w_in_bf)


def _t5_bucket_np(dist):
    max_exact = NUM_BUCKETS // 2
    n = np.maximum(dist, 0)
    nf = np.maximum(n, 1).astype(np.float64)
    large = max_exact + (np.log(nf / max_exact) / math.log(MAX_DISTANCE / max_exact)
                         * (NUM_BUCKETS - max_exact)).astype(np.int32)
    large = np.minimum(large, NUM_BUCKETS - 1)
    return np.where(n < max_exact, n, large).astype(np.int32)


def _bucket_tables():
    k = np.arange(MOBA_BLOCK)[:, None]
    q = np.arange(MOBA_BLOCK)[None, :]
    own = np.where(q - k >= 0, _t5_bucket_np(q - k), -1)
    prev = _t5_bucket_np(MOBA_BLOCK + q - k)
    return np.stack([own, prev]).astype(np.int32)


def _biasprep_body(rb_ref, bkt_ref, o_ref):
    h = pl.program_id(0)
    far = rb_ref[NUM_BUCKETS - 1, h]
    for t in range(2):
        bkt = bkt_ref[t]
        acc = jnp.where(bkt < 0, -MASK_BIG, 0.0).astype(F32)
        for bk in range(NUM_BUCKETS):
            acc = jnp.where(bkt == bk, (rb_ref[bk, h] - far) * LOG2E, acc)
        o_ref[0, t] = acc


def _biasprep(rel_bias):
    bkt = jnp.asarray(_bucket_tables())
    return pl.pallas_call(
        _biasprep_body,
        grid=(ATTN_HEADS,),
        in_specs=[
            pl.BlockSpec(memory_space=pltpu.SMEM),
            pl.BlockSpec((2, MOBA_BLOCK, MOBA_BLOCK), lambda h: (0, 0, 0)),
        ],
        out_specs=pl.BlockSpec((1, 2, MOBA_BLOCK, MOBA_BLOCK), lambda h: (h, 0, 0, 0)),
        out_shape=jax.ShapeDtypeStruct((ATTN_HEADS, 2, MOBA_BLOCK, MOBA_BLOCK), F32),
        compiler_params=pltpu.CompilerParams(dimension_semantics=("arbitrary",)),
        name="biasprep",
    )(rel_bias, bkt)


ITEM_FIELDS = 5
PIPE_UNROLL = 8


def _far_items(n_blocks):
    never = n_blocks - 1
    items = []
    for i in range(n_blocks):
        n_far = max(i - 1, 0)
        for j0 in range(0, n_far, 2):
            items.append((i, j0, j0, j0 + 1, j0 + 1) if j0 + 1 < n_far else (i, j0, j0, j0, never))
    dummy = (n_blocks - 1, 0, never, 0, never)
    items.extend([dummy] * (-len(items) % PIPE_UNROLL))
    items.append(dummy)
    return np.asarray(items, np.int32)


def _attn_body(tbl_ref, zq_ref, zk_ref, zv_ref, gq_ref, gk_ref, bias_ref, o_ref,
               kn_ref, vta_ref, w_ref, pen_ref, acc_ref, m_ref, s0_ref, s1_ref, *, n_blocks, n_items):
    blk = MOBA_BLOCK
    heads = range(HEADS_PER_STEP)

    def rows_of(j):
        return pl.ds(pl.multiple_of(j * blk, blk), blk)

    lane = lax.broadcasted_iota(jnp.int32, (blk, PAIR_W), 1)
    first = lane < HEAD_DIM
    blk_rows = lax.broadcasted_iota(jnp.int32, (n_blocks, PAIR_W), 0)
    ones = jnp.ones((V_ROWS - HEAD_DIM, blk), BF16)

    def kv_prep(i, kmean):
        kraw = zk_ref[0, rows_of(i), :].astype(F32)
        sq = kraw * kraw
        ss_a = jnp.sum(jnp.where(first, sq, 0.0), axis=-1, keepdims=True)
        ss_b = jnp.sum(jnp.where(first, 0.0, sq), axis=-1, keepdims=True)
        rinv = lax.rsqrt(jnp.where(first, ss_a, ss_b) * (1.0 / HEAD_DIM) + EPS)
        kn = kraw * rinv * gk_ref[...]
        kn_ref[rows_of(i), :] = kn.astype(BF16)
        vt = zv_ref[0, rows_of(i), :].astype(F32).T
        for hd in heads:
            vta_ref[i, hd, 0:HEAD_DIM, :] = vt[hd * HEAD_DIM:(hd + 1) * HEAD_DIM].astype(BF16)
            vta_ref[i, hd, HEAD_DIM:V_ROWS, :] = ones
        return jnp.where(blk_rows == i, jnp.mean(kn, axis=0, keepdims=True), kmean)

    kmean = lax.fori_loop(0, n_blocks, kv_prep, jnp.zeros((n_blocks, PAIR_W), F32), unroll=2)

    blk_ids = lax.broadcasted_iota(jnp.int32, (n_blocks, blk), 0)
    zeros_half = jnp.zeros((HEAD_DIM, blk), F32)

    def q_prep(i, _):
        qt = zq_ref[0, rows_of(i), :].astype(F32).T
        qsq = qt * qt
        for hd in heads:
            rows = slice(hd * HEAD_DIM, (hd + 1) * HEAD_DIM)
            ss = jnp.sum(qsq[rows], axis=0, keepdims=True)
            r = lax.rsqrt(ss * (1.0 / HEAD_DIM) + EPS)
            qn = qt[rows] * r * gq_ref[rows, :] * (HEAD_DIM ** -0.5 * LOG2E)
            parts = [zeros_half] * HEADS_PER_STEP
            parts[hd] = qn
            w = jnp.concatenate(parts, axis=0)
            w_ref[i, hd] = w.astype(BF16)
            gate = jnp.where(blk_ids < i, _dot3(kmean, w), -jnp.inf)
            sel = jnp.zeros((n_blocks, blk), jnp.bool_)
            for _ in range(MOBA_TOPK):
                mx = jnp.max(gate, axis=0, keepdims=True)
                cand = jnp.where(gate == mx, blk_ids, n_blocks)
                pick = (blk_ids == jnp.min(cand, axis=0, keepdims=True)) & (mx > -jnp.inf)
                sel = sel | pick
                gate = jnp.where(pick, -jnp.inf, gate)
            pen_ref[hd, i] = jnp.where(sel, 0.0, -MASK_BIG).astype(F32)
        return 0

    lax.fori_loop(0, n_blocks, q_prep, 0, unroll=2)

    def score(i, blocks, s_ref, biased):
        keys = [kn_ref[rows_of(j), :] for j, _ in blocks]
        gmax = []
        for hd in heads:
            w = w_ref[i, hd]
            tmax = []
            for u, (_, jpen) in enumerate(blocks):
                s = _dot(keys[u], w)
                if biased:
                    s = s + bias_ref[hd, u]
                s_ref[hd, u] = s
                t = jnp.max(s, axis=0, keepdims=True)
                tmax.append(t if jpen is None else t + pen_ref[hd, i, pl.ds(jpen, 1), :])
            gmax.append(functools.reduce(jnp.maximum, tmax))
        return tuple(gmax)

    def attend(i, blocks, s_ref, gmax, first_visit):
        for hd in heads:
            if first_visit:
                m_new = gmax[hd]
                acc = None
            else:
                m_old = m_ref[i, hd]
                m_new = jnp.maximum(m_old, gmax[hd])
                acc = acc_ref[i, hd] * jnp.exp2(m_old - m_new)
            for u, (j, jpen) in enumerate(blocks):
                shift = m_new if jpen is None else m_new - pen_ref[hd, i, pl.ds(jpen, 1), :]
                p = jnp.exp2(s_ref[hd, u] - shift).astype(BF16)
                pv = _dot(vta_ref[j, hd], p)
                acc = pv if acc is None else acc + pv
            acc_ref[i, hd] = acc
            m_ref[i, hd] = m_new

    def near_blocks(i):
        jp = jnp.maximum(i - 1, 0)
        return ((i, None), (jp, jp))

    slots = (s0_ref, s1_ref)

    def near_body(t, g):
        for u in range(PIPE_UNROLL):
            i_cur = PIPE_UNROLL * t + u
            i_nxt = jnp.minimum(i_cur + 1, n_blocks - 1)
            g_nxt = score(i_nxt, near_blocks(i_nxt), slots[(u + 1) % 2], True)
            attend(i_cur, near_blocks(i_cur), slots[u % 2], g, True)
            g = g_nxt
        return g

    lax.fori_loop(0, n_blocks // PIPE_UNROLL, near_body, score(0, near_blocks(0), s0_ref, True))

    def item(k):
        base = k * ITEM_FIELDS
        i = tbl_ref[base]
        return i, ((tbl_ref[base + 1], tbl_ref[base + 2]), (tbl_ref[base + 3], tbl_ref[base + 4]))

    def far_body(t, g):
        for u in range(PIPE_UNROLL):
            k = PIPE_UNROLL * t + u
            g_nxt = score(*item(k + 1), slots[(u + 1) % 2], False)
            attend(*item(k), slots[u % 2], g, False)
            g = g_nxt
        return g

    lax.fori_loop(0, n_items // PIPE_UNROLL, far_body, score(*item(0), s0_ref, False))

    def finish(i, _):
        outs = []
        for hd in heads:
            acc = acc_ref[i, hd]
            outs.append(acc[0:HEAD_DIM] / acc[HEAD_DIM:HEAD_DIM + 1])
        o_ref[0, rows_of(i), :] = jnp.concatenate(outs, axis=0).T.astype(BF16)
        return 0

    lax.fori_loop(0, n_blocks, finish, 0)


def _attention(z, gq_t, gk_row, bias):
    b, s, _ = z.shape
    n_blocks = s // MOBA_BLOCK
    assert n_blocks % PIPE_UNROLL == 0 and PIPE_UNROLL % 2 == 0
    blk = MOBA_BLOCK
    items = _far_items(n_blocks)
    n_items = items.shape[0] - 1
    grid_spec = pltpu.PrefetchScalarGridSpec(
        num_scalar_prefetch=1,
        grid=(b, N_PAIRS),
        in_specs=[
            pl.BlockSpec((1, s, PAIR_W), lambda bi, hp, tbl: (bi, 0, hp)),
            pl.BlockSpec((1, s, PAIR_W), lambda bi, hp, tbl: (bi, 0, N_PAIRS + hp)),
            pl.BlockSpec((1, s, PAIR_W), lambda bi, hp, tbl: (bi, 0, 2 * N_PAIRS + hp)),
            pl.BlockSpec((PAIR_W, blk), lambda bi, hp, tbl: (0, 0)),
            pl.BlockSpec((1, PAIR_W), lambda bi, hp, tbl: (0, 0)),
            pl.BlockSpec((HEADS_PER_STEP, 2, blk, blk), lambda bi, hp, tbl: (hp, 0, 0, 0)),
        ],
        out_specs=pl.BlockSpec((1, s, PAIR_W), lambda bi, hp, tbl: (bi, 0, hp)),
        scratch_shapes=[
            pltpu.VMEM((s, PAIR_W), BF16),
            pltpu.VMEM((n_blocks, HEADS_PER_STEP, V_ROWS, blk), BF16),
            pltpu.VMEM((n_blocks, HEADS_PER_STEP, PAIR_W, blk), BF16),
            pltpu.VMEM((HEADS_PER_STEP, n_blocks, n_blocks, blk), F32),
            pltpu.VMEM((n_blocks, HEADS_PER_STEP, V_ROWS, blk), F32),
            pltpu.VMEM((n_blocks, HEADS_PER_STEP, 1, blk), F32),
            pltpu.VMEM((HEADS_PER_STEP, 2, blk, blk), F32),
            pltpu.VMEM((HEADS_PER_STEP, 2, blk, blk), F32),
        ],
    )
    return pl.pallas_call(
        functools.partial(_attn_body, n_blocks=n_blocks, n_items=n_items),
        grid_spec=grid_spec,
        out_shape=jax.ShapeDtypeStruct((b, s, ATTN_W), BF16),
        compiler_params=pltpu.CompilerParams(
            dimension_semantics=("arbitrary", "arbitrary"), vmem_limit_bytes=VMEM_LIMIT),
        name="attn",
    )(jnp.asarray(items.reshape(-1)), z, z, z, gq_t, gk_row, bias)


def _sigmoid(v):
    return 1.0 / (1.0 + jnp.exp(-v))


def _mix_body(x_ref, attn_ref, p_ref, ga_ref, gp_ref, mod_ref, pw_ref, ps_ref, wba_ref, wbp_ref,
              wo_ref, g2_ref, x1_ref, h2_ref, pe_ref):
    t = pl.program_id(1)
    tm = ROW_TILE

    @pl.when(t == 0)
    def _():
        pe_ref[0:POOL_HALO, :] = jnp.zeros((POOL_HALO, POOL_W), F32)

    @pl.when(t > 0)
    def _():
        pe_ref[0:POOL_HALO, :] = pe_ref[tm:tm + POOL_HALO, :]

    pe_ref[POOL_HALO:POOL_HALO + tm, :] = p_ref[0].astype(F32)

    pos = t * tm + lax.broadcasted_iota(jnp.int32, (tm, POOL_GROUP_W), 0)
    pooled = []
    for g, win in enumerate(POOL_WINDOWS):
        cols = slice(g * POOL_GROUP_W, (g + 1) * POOL_GROUP_W)
        cur = pe_ref[POOL_HALO:POOL_HALO + tm, cols]
        tot = cur
        for dlt in range(1, win):
            tot = tot + pe_ref[POOL_HALO - dlt:POOL_HALO - dlt + tm, cols]
        count = jnp.minimum(pos + 1, win).astype(F32)
        pooled_g = (tot / count - cur).astype(BF16)
        pooled.append(_dot(pooled_g, pw_ref[g]) * ps_ref[:, cols])
    pool = jnp.concatenate(pooled, axis=-1).astype(BF16)

    a_proj = _dot(attn_ref[0], wba_ref[...])
    p_proj = _dot(pool, wbp_ref[...])
    merged = (_sigmoid(ga_ref[0].astype(F32)) * a_proj
              + _sigmoid(gp_ref[0].astype(F32)) * p_proj).astype(BF16)
    x1 = x_ref[0] + mod_ref[0, 2:3, :] * _dot(merged, wo_ref[...])
    x1_ref[0] = x1
    h2_ref[0] = _norm_mod(x1, g2_ref[...], mod_ref[0, 4:5, :], mod_ref[0, 3:4, :]).astype(BF16)


def _mix(x, attn, z, mod, pool_w_bf, pool_scale, wba_bf, wbp_bf, wo_bf, g2):
    b, s, d = x.shape
    tm = ROW_TILE
    const2 = lambda bi, t: (0, 0)
    return pl.pallas_call(
        _mix_body,
        grid=(b, s // tm),
        in_specs=[
            pl.BlockSpec((1, tm, d), lambda bi, t: (bi, t, 0)),
            pl.BlockSpec((1, tm, ATTN_W), lambda bi, t: (bi, t, 0)),
            pl.BlockSpec((1, tm, POOL_W), lambda bi, t: (bi, t, 3 * ATTN_W // POOL_W)),
            pl.BlockSpec((1, tm, d), lambda bi, t: (bi, t, (3 * ATTN_W + POOL_W) // d)),
            pl.BlockSpec((1, tm, d), lambda bi, t: (bi, t, (3 * ATTN_W + POOL_W) // d + 1)),
            pl.BlockSpec((1, 6, d), lambda bi, t: (bi, 0, 0)),
            pl.BlockSpec((POOL_GROUPS, POOL_GROUP_W, POOL_GROUP_W), lambda bi, t: (0, 0, 0)),
            pl.BlockSpec((1, POOL_W), const2),
            pl.BlockSpec((ATTN_W, d), const2),
            pl.BlockSpec((POOL_W, d), const2),
            pl.BlockSpec((d, d), const2),
            pl.BlockSpec((1, d), const2),
        ],
        out_specs=[
            pl.BlockSpec((1, tm, d), lambda bi, t: (bi, t, 0)),
            pl.BlockSpec((1, tm, d), lambda bi, t: (bi, t, 0)),
        ],
        out_shape=[
            jax.ShapeDtypeStruct((b, s, d), F32),
            jax.ShapeDtypeStruct((b, s, d), BF16),
        ],
        scratch_shapes=[pltpu.VMEM((tm + POOL_HALO, POOL_W), F32)],
        compiler_params=pltpu.CompilerParams(
            dimension_semantics=("arbitrary", "arbitrary"), vmem_limit_bytes=VMEM_LIMIT),
        name="mix",
    )(x, attn, z, z, z, mod, pool_w_bf, pool_scale, wba_bf, wbp_bf, wo_bf, g2)


def _ffn_body(x1_ref, h2_ref, mod_ref, wup_ref, cw_ref, cb_ref, wdn_ref, o_ref,
              hist0_ref, hist1_ref, carry_ref):
    t = pl.program_id(1)
    tm = ROW_TILE
    fc = FFN_CHUNK
    n_chunks = D_FF // fc
    hist = (hist0_ref, hist1_ref)

    @pl.when(t == 0)
    def _():
        carry_ref[...] = jnp.zeros_like(carry_ref)

    h2 = h2_ref[0]

    def up(c):
        out = []
        for half in range(2):
            cols = slice(half * D_FF + c * fc, half * D_FF + (c + 1) * fc)
            hcols = slice(half * fc, (half + 1) * fc)
            r = _dot(h2, wup_ref[:, cols])
            hist[c % 2][0:CONV_HALO, hcols] = carry_ref[:, cols]
            hist[c % 2][CONV_HALO:CONV_HALO + tm, hcols] = r
            carry_ref[:, cols] = r[tm - CONV_HALO:tm]
            out.append(r)
        return out

    acc = jnp.zeros((tm, D_MODEL), F32)
    r_next = up(0)
    act_prev = None
    for c in range(n_chunks):
        r_cur = r_next
        if c + 1 < n_chunks:
            r_next = up(c + 1)
        if act_prev is not None:
            acc = acc + _dot(act_prev, wdn_ref[(c - 1) * fc:c * fc, :])
        halves = []
        for half in range(2):
            cols = slice(half * D_FF + c * fc, half * D_FF + (c + 1) * fc)
            hcols = slice(half * fc, (half + 1) * fc)
            u = cb_ref[:, cols] + cw_ref[CONV_W - 1:CONV_W, cols] * r_cur[half]
            for tap in range(CONV_W - 1):
                back = CONV_W - 1 - tap
                u = u + cw_ref[tap:tap + 1, cols] * hist[c % 2][CONV_HALO - back:CONV_HALO - back + tm, hcols]
            halves.append(u)
        u_g, u_v = halves
        act_prev = (u_g * _sigmoid(u_g) * u_v).astype(BF16)
    acc = acc + _dot(act_prev, wdn_ref[(n_chunks - 1) * fc:n_chunks * fc, :])
    o_ref[0] = x1_ref[0] + mod_ref[0, 5:6, :] * acc


def _ffn(x1, h2, mod, wup_bf, conv_w, conv_b, wdn_bf):
    b, s, d = x1.shape
    tm = ROW_TILE
    const2 = lambda bi, t: (0, 0)
    return pl.pallas_call(
        _ffn_body,
        grid=(b, s // tm),
        in_specs=[
            pl.BlockSpec((1, tm, d), lambda bi, t: (bi, t, 0)),
            pl.BlockSpec((1, tm, d), lambda bi, t: (bi, t, 0)),
            pl.BlockSpec((1, 6, d), lambda bi, t: (bi, 0, 0)),
            pl.BlockSpec((d, 2 * D_FF), const2),
            pl.BlockSpec((CONV_W, 2 * D_FF), const2),
            pl.BlockSpec((1, 2 * D_FF), const2),
            pl.BlockSpec((D_FF, d), const2),
        ],
        out_specs=pl.BlockSpec((1, tm, d), lambda bi, t: (bi, t, 0)),
        out_shape=jax.ShapeDtypeStruct((b, s, d), F32),
        scratch_shapes=[
            pltpu.VMEM((tm + CONV_HALO, 2 * FFN_CHUNK), F32),
            pltpu.VMEM((tm + CONV_HALO, 2 * FFN_CHUNK), F32),
            pltpu.VMEM((CONV_HALO, 2 * D_FF), F32),
        ],
        compiler_params=pltpu.CompilerParams(
            dimension_semantics=("arbitrary", "arbitrary"), vmem_limit_bytes=VMEM_LIMIT),
        name="ffn",
    )(x1, h2, mod, wup_bf, conv_w, conv_b, wdn_bf)


def _layer(x, c_pad, rel_bias, ada_w, ada_b, norm1_g, w_in, q_norm_g, k_norm_g, pool_w, pool_scale,
           w_branch_attn, w_branch_pool, w_out, norm2_g, w_up, conv_w, conv_b, w_down):
    b, s, d = x.shape
    mod = _ada(c_pad, ada_w, ada_b[None, :])[:b].reshape(b, 6, d)
    z = _inproj(x, mod, norm1_g[None, :], w_in.astype(BF16))
    gq_t = jnp.broadcast_to(jnp.tile(q_norm_g, HEADS_PER_STEP)[:, None], (PAIR_W, MOBA_BLOCK))
    gk_row = jnp.tile(k_norm_g, HEADS_PER_STEP)[None, :]
    attn = _attention(z, gq_t, gk_row, _biasprep(rel_bias))
    x1, h2 = _mix(x, attn, z, mod, pool_w.astype(BF16), pool_scale[None, :],
                  w_branch_attn.astype(BF16), w_branch_pool.astype(BF16), w_out.astype(BF16),
                  norm2_g[None, :])
    return _ffn(x1, h2, mod, w_up.astype(BF16), conv_w, conv_b[None, :], w_down.astype(BF16))


def kernel(x, c, ada_w, ada_b, norm1_g, w_in, q_norm_g, k_norm_g, rel_bias, pool_w, pool_scale,
           w_branch_attn, w_branch_pool, w_out, norm2_g, w_up, conv_w, conv_b, w_down):
    b, s, d = x.shape
    assert d == D_MODEL and s % MOBA_BLOCK == 0 and w_in.shape[-1] == IN_W
    c_pad = jnp.zeros((8, d), F32).at[:b].set(c)
    for l in range(ada_w.shape[0]):
        x = _layer(x, c_pad, rel_bias, ada_w[l], ada_b[l], norm1_g[l], w_in[l], q_norm_g[l],
                   k_norm_g[l], pool_w[l], pool_scale[l], w_branch_attn[l], w_branch_pool[l],
                   w_out[l], norm2_g[l], w_up[l], conv_w[l], conv_b[l], w_down[l])
    return x
```

```python
import functools
import math

import numpy as np
import jax
import jax.numpy as jnp
from jax import lax
from jax.experimental import pallas as pl
from jax.experimental.pallas import tpu as pltpu

F32 = jnp.float32
BF16 = jnp.bfloat16

D_MODEL = 1024
ATTN_HEADS = 8
HEAD_DIM = 64
ATTN_W = ATTN_HEADS * HEAD_DIM
MOBA_BLOCK = 256
MOBA_TOPK = 3
POOL_GROUPS = 4
POOL_GROUP_W = 128
POOL_W = POOL_GROUPS * POOL_GROUP_W
POOL_WINDOWS = (2, 4, 8, 16)
NUM_BUCKETS = 32
MAX_DISTANCE = 128
D_FF = 2816
CONV_W = 3
EPS = 1e-6
IN_W = 3 * ATTN_W + POOL_W + 2 * D_MODEL

HEADS_PER_STEP = 2
PAIR_W = HEADS_PER_STEP * HEAD_DIM
N_PAIRS = ATTN_HEADS // HEADS_PER_STEP
V_ROWS = HEAD_DIM + 16
LOG2E = math.log2(math.e)
MASK_BIG = 1e30
ROW_TILE = 256
POOL_HALO = 16
MIX_TILE = 512
MIX_SPLIT = 2
FFN_CHUNK = 256
FFN_TILE = 512
FFN_SPLIT = 2
CONV_HALO = 8
VMEM_LIMIT = 56 * 1024 * 1024


def _split_bf16(a):
    hi = a.astype(BF16)
    lo = (a - hi.astype(F32)).astype(BF16)
    return hi, lo


def _dot(a, b):
    return jnp.dot(a, b, preferred_element_type=F32)


def _dot3(a, b):
    ah, al = _split_bf16(a)
    bh, bl = _split_bf16(b)
    return _dot(ah, bh) + _dot(al, bh) + _dot(ah, bl)


def _ada_body(c_ref, w_ref, b_ref, o_ref):
    c = c_ref[...]
    c_act = c * (1.0 / (1.0 + jnp.exp(-c)))
    o_ref[...] = _dot3(c_act, w_ref[...]) + b_ref[...]


def _ada(c_pad, ada_w, ada_b):
    rows, d = c_pad.shape
    n = ada_w.shape[1]
    tn = 1024
    return pl.pallas_call(
        _ada_body,
        grid=(n // tn,),
        in_specs=[
            pl.BlockSpec((rows, d), lambda j: (0, 0)),
            pl.BlockSpec((d, tn), lambda j: (0, j)),
            pl.BlockSpec((1, tn), lambda j: (0, j)),
        ],
        out_specs=pl.BlockSpec((rows, tn), lambda j: (0, j)),
        out_shape=jax.ShapeDtypeStruct((rows, n), F32),
        compiler_params=pltpu.CompilerParams(
            dimension_semantics=("arbitrary",), vmem_limit_bytes=VMEM_LIMIT),
        name="ada",
    )(c_pad, ada_w, ada_b)


def _norm_mod(x, gain, scale, shift):
    ms = jnp.mean(x * x, axis=-1, keepdims=True)
    y = x * lax.rsqrt(ms + EPS) * gain
    return y * (1.0 + scale) + shift


def _inproj_body(x_ref, mod_ref, g_ref, w_ref, z_ref):
    h = _norm_mod(x_ref[0], g_ref[...], mod_ref[0, 1:2, :], mod_ref[0, 0:1, :]).astype(BF16)
    n_chunk = 1024
    for n in range(IN_W // n_chunk):
        cols = slice(n * n_chunk, (n + 1) * n_chunk)
        z_ref[0, :, cols] = _dot(h, w_ref[:, cols]).astype(BF16)


def _inproj(x, mod, g1, w_in_bf):
    b, s, d = x.shape
    return pl.pallas_call(
        _inproj_body,
        grid=(b, s // ROW_TILE),
        in_specs=[
            pl.BlockSpec((1, ROW_TILE, d), lambda bi, t: (bi, t, 0)),
            pl.BlockSpec((1, 6, d), lambda bi, t: (bi, 0, 0)),
            pl.BlockSpec((1, d), lambda bi, t: (0, 0)),
            pl.BlockSpec((d, IN_W), lambda bi, t: (0, 0)),
        ],
        out_specs=pl.BlockSpec((1, ROW_TILE, IN_W), lambda bi, t: (bi, t, 0)),
        out_shape=jax.ShapeDtypeStruct((b, s, IN_W), BF16),
        compiler_params=pltpu.CompilerParams(
            dimension_semantics=("arbitrary", "arbitrary"), vmem_limit_bytes=VMEM_LIMIT),
        name="inproj",
    )(x, mod, g1, w_in_bf)


def _t5_bucket_np(dist):
    max_exact = NUM_BUCKETS // 2
    n = np.maximum(dist, 0)
    nf = np.maximum(n, 1).astype(np.float64)
    large = max_exact + (np.log(nf / max_exact) / math.log(MAX_DISTANCE / max_exact)
                         * (NUM_BUCKETS - max_exact)).astype(np.int32)
    large = np.minimum(large, NUM_BUCKETS - 1)
    return np.where(n < max_exact, n, large).astype(np.int32)


def _bucket_tables():
    k = np.arange(MOBA_BLOCK)[:, None]
    q = np.arange(MOBA_BLOCK)[None, :]
    own = np.where(q - k >= 0, _t5_bucket_np(q - k), -1)
    prev = _t5_bucket_np(MOBA_BLOCK + q - k)
    return np.stack([own, prev]).astype(np.int32)


def _biasprep_body(rb_ref, bkt_ref, o_ref):
    h = pl.program_id(0)
    far = rb_ref[NUM_BUCKETS - 1, h]
    for t in range(2):
        bkt = bkt_ref[t]
        acc = jnp.where(bkt < 0, -MASK_BIG, 0.0).astype(F32)
        for bk in range(NUM_BUCKETS):
            acc = jnp.where(bkt == bk, (rb_ref[bk, h] - far) * LOG2E, acc)
        o_ref[0, t] = acc


def _biasprep(rel_bias):
    bkt = jnp.asarray(_bucket_tables())
    return pl.pallas_call(
        _biasprep_body,
        grid=(ATTN_HEADS,),
        in_specs=[
            pl.BlockSpec(memory_space=pltpu.SMEM),
            pl.BlockSpec((2, MOBA_BLOCK, MOBA_BLOCK), lambda h: (0, 0, 0)),
        ],
        out_specs=pl.BlockSpec((1, 2, MOBA_BLOCK, MOBA_BLOCK), lambda h: (h, 0, 0, 0)),
        out_shape=jax.ShapeDtypeStruct((ATTN_HEADS, 2, MOBA_BLOCK, MOBA_BLOCK), F32),
        compiler_params=pltpu.CompilerParams(dimension_semantics=("arbitrary",)),
        name="biasprep",
    )(rel_bias, bkt)


PREP_UNROLL = 4
ITEM_FIELDS = 5
PIPE_UNROLL = 8


def _far_items(n_blocks):
    never = n_blocks - 1
    items = []
    for i in range(n_blocks):
        n_far = max(i - 1, 0)
        for j0 in range(0, n_far, 2):
            items.append((i, j0, j0, j0 + 1, j0 + 1) if j0 + 1 < n_far else (i, j0, j0, j0, never))
    dummy = (n_blocks - 1, 0, never, 0, never)
    items.extend([dummy] * (-len(items) % PIPE_UNROLL))
    items.append(dummy)
    return np.asarray(items, np.int32)


def _attn_body(tbl_ref, zq_ref, zk_ref, zv_ref, gq_ref, gk_ref, bias_ref, o_ref,
               kn_ref, vta_ref, w_ref, pen_ref, acc_ref, m_ref, s0_ref, s1_ref, *, n_blocks, n_items):
    blk = MOBA_BLOCK
    heads = range(HEADS_PER_STEP)

    def rows_of(j):
        return pl.ds(pl.multiple_of(j * blk, blk), blk)

    blk_rows = lax.broadcasted_iota(jnp.int32, (n_blocks, PAIR_W), 0)
    ones = jnp.ones((V_ROWS - HEAD_DIM, blk), BF16)
    same_head = (lax.broadcasted_iota(jnp.int32, (PAIR_W, PAIR_W), 0) // HEAD_DIM
                 == lax.broadcasted_iota(jnp.int32, (PAIR_W, PAIR_W), 1) // HEAD_DIM)
    head_sum = jnp.where(same_head, 1.0, 0.0).astype(BF16)

    def kv_prep(i, kmean):
        kraw = zk_ref[0, rows_of(i), :].astype(F32)
        sq_hi, sq_lo = _split_bf16(kraw * kraw)
        ss = _dot(sq_hi, head_sum) + _dot(sq_lo, head_sum)
        rinv = lax.rsqrt(ss * (1.0 / HEAD_DIM) + EPS)
        kn = kraw * rinv * gk_ref[...]
        kn_ref[rows_of(i), :] = kn.astype(BF16)
        vt = zv_ref[0, rows_of(i), :].astype(F32).T
        for hd in heads:
            vta_ref[i, hd, 0:HEAD_DIM, :] = vt[hd * HEAD_DIM:(hd + 1) * HEAD_DIM].astype(BF16)
            vta_ref[i, hd, HEAD_DIM:V_ROWS, :] = ones
        return jnp.where(blk_rows == i, jnp.mean(kn, axis=0, keepdims=True), kmean)

    kmean = lax.fori_loop(0, n_blocks, kv_prep, jnp.zeros((n_blocks, PAIR_W), F32), unroll=PREP_UNROLL)

    blk_ids = lax.broadcasted_iota(jnp.int32, (n_blocks, blk), 0)
    zeros_half = jnp.zeros((HEAD_DIM, blk), F32)

    def q_prep(i, _):
        qt = zq_ref[0, rows_of(i), :].astype(F32).T
        qsq = qt * qt
        for hd in heads:
            rows = slice(hd * HEAD_DIM, (hd + 1) * HEAD_DIM)
            ss = jnp.sum(qsq[rows], axis=0, keepdims=True)
            r = lax.rsqrt(ss * (1.0 / HEAD_DIM) + EPS)
            qn = qt[rows] * r * gq_ref[rows, :] * (HEAD_DIM ** -0.5 * LOG2E)
            parts = [zeros_half] * HEADS_PER_STEP
            parts[hd] = qn
            w = jnp.concatenate(parts, axis=0)
            w_ref[i, hd] = w.astype(BF16)
            gate = jnp.where(blk_ids < i, _dot3(kmean, w), -jnp.inf)
            sel = jnp.zeros((n_blocks, blk), jnp.bool_)
            for _ in range(MOBA_TOPK):
                mx = jnp.max(gate, axis=0, keepdims=True)
                cand = jnp.where(gate == mx, blk_ids, n_blocks)
                pick = (blk_ids == jnp.min(cand, axis=0, keepdims=True)) & (mx > -jnp.inf)
                sel = sel | pick
                gate = jnp.where(pick, -jnp.inf, gate)
            pen_ref[hd, i] = jnp.where(sel, 0.0, -MASK_BIG).astype(F32)
        return 0

    lax.fori_loop(0, n_blocks, q_prep, 0, unroll=PREP_UNROLL)

    def score(i, blocks, s_ref, biased):
        keys = [kn_ref[rows_of(j), :] for j, _ in blocks]
        gmax = []
        for hd in heads:
            w = w_ref[i, hd]
            tmax = []
            for u, (_, jpen) in enumerate(blocks):
                s = _dot(keys[u], w)
                if biased:
                    s = s + bias_ref[hd, u]
                s_ref[hd, u] = s
                t = jnp.max(s, axis=0, keepdims=True)
                tmax.append(t if jpen is None else t + pen_ref[hd, i, pl.ds(jpen, 1), :])
            gmax.append(functools.reduce(jnp.maximum, tmax))
        return tuple(gmax)

    def attend(i, blocks, s_ref, gmax, first_visit):
        for hd in heads:
            if first_visit:
                m_new = gmax[hd]
                acc = None
            else:
                m_old = m_ref[i, hd]
                m_new = jnp.maximum(m_old, gmax[hd])
                acc = acc_ref[i, hd] * jnp.exp2(m_old - m_new)
            for u, (j, jpen) in enumerate(blocks):
                shift = m_new if jpen is None else m_new - pen_ref[hd, i, pl.ds(jpen, 1), :]
                p = jnp.exp2(s_ref[hd, u] - shift).astype(BF16)
                pv = _dot(vta_ref[j, hd], p)
                acc = pv if acc is None else acc + pv
            acc_ref[i, hd] = acc
            m_ref[i, hd] = m_new

    def near_blocks(i):
        jp = jnp.maximum(i - 1, 0)
        return ((i, None), (jp, jp))

    slots = (s0_ref, s1_ref)

    def near_body(t, g):
        for u in range(PIPE_UNROLL):
            i_cur = PIPE_UNROLL * t + u
            i_nxt = jnp.minimum(i_cur + 1, n_blocks - 1)
            g_nxt = score(i_nxt, near_blocks(i_nxt), slots[(u + 1) % 2], True)
            attend(i_cur, near_blocks(i_cur), slots[u % 2], g, True)
            g = g_nxt
        return g

    lax.fori_loop(0, n_blocks // PIPE_UNROLL, near_body, score(0, near_blocks(0), s0_ref, True))

    def item(k):
        base = k * ITEM_FIELDS
        i = tbl_ref[base]
        return i, ((tbl_ref[base + 1], tbl_ref[base + 2]), (tbl_ref[base + 3], tbl_ref[base + 4]))

    def far_body(t, g):
        for u in range(PIPE_UNROLL):
            k = PIPE_UNROLL * t + u
            g_nxt = score(*item(k + 1), slots[(u + 1) % 2], False)
            attend(*item(k), slots[u % 2], g, False)
            g = g_nxt
        return g

    lax.fori_loop(0, n_items // PIPE_UNROLL, far_body, score(*item(0), s0_ref, False))

    def finish(i, _):
        outs = []
        for hd in heads:
            acc = acc_ref[i, hd]
            outs.append(acc[0:HEAD_DIM] / acc[HEAD_DIM:HEAD_DIM + 1])
        o_ref[0, rows_of(i), :] = jnp.concatenate(outs, axis=0).T.astype(BF16)
        return 0

    lax.fori_loop(0, n_blocks, finish, 0, unroll=PREP_UNROLL)


def _attention(z, gq_t, gk_row, bias):
    b, s, _ = z.shape
    n_blocks = s // MOBA_BLOCK
    assert n_blocks % PIPE_UNROLL == 0 and PIPE_UNROLL % 2 == 0
    blk = MOBA_BLOCK
    items = _far_items(n_blocks)
    n_items = items.shape[0] - 1
    grid_spec = pltpu.PrefetchScalarGridSpec(
        num_scalar_prefetch=1,
        grid=(b, N_PAIRS),
        in_specs=[
            pl.BlockSpec((1, s, PAIR_W), lambda bi, hp, tbl: (bi, 0, hp)),
            pl.BlockSpec((1, s, PAIR_W), lambda bi, hp, tbl: (bi, 0, N_PAIRS + hp)),
            pl.BlockSpec((1, s, PAIR_W), lambda bi, hp, tbl: (bi, 0, 2 * N_PAIRS + hp)),
            pl.BlockSpec((PAIR_W, blk), lambda bi, hp, tbl: (0, 0)),
            pl.BlockSpec((1, PAIR_W), lambda bi, hp, tbl: (0, 0)),
            pl.BlockSpec((HEADS_PER_STEP, 2, blk, blk), lambda bi, hp, tbl: (hp, 0, 0, 0)),
        ],
        out_specs=pl.BlockSpec((1, s, PAIR_W), lambda bi, hp, tbl: (bi, 0, hp)),
        scratch_shapes=[
            pltpu.VMEM((s, PAIR_W), BF16),
            pltpu.VMEM((n_blocks, HEADS_PER_STEP, V_ROWS, blk), BF16),
            pltpu.VMEM((n_blocks, HEADS_PER_STEP, PAIR_W, blk), BF16),
            pltpu.VMEM((HEADS_PER_STEP, n_blocks, n_blocks, blk), F32),
            pltpu.VMEM((n_blocks, HEADS_PER_STEP, V_ROWS, blk), F32),
            pltpu.VMEM((n_blocks, HEADS_PER_STEP, 1, blk), F32),
            pltpu.VMEM((HEADS_PER_STEP, 2, blk, blk), F32),
            pltpu.VMEM((HEADS_PER_STEP, 2, blk, blk), F32),
        ],
    )
    return pl.pallas_call(
        functools.partial(_attn_body, n_blocks=n_blocks, n_items=n_items),
        grid_spec=grid_spec,
        out_shape=jax.ShapeDtypeStruct((b, s, ATTN_W), BF16),
        compiler_params=pltpu.CompilerParams(
            dimension_semantics=("arbitrary", "arbitrary"), vmem_limit_bytes=VMEM_LIMIT),
        name="attn",
    )(jnp.asarray(items.reshape(-1)), z, z, z, gq_t, gk_row, bias)


def _sigmoid(v):
    return 1.0 / (1.0 + jnp.exp2(v * (-LOG2E)))


def _mix_body(x_ref, attn_ref, p_ref, ga_ref, gp_ref, mod_ref, pw_ref, ps_ref, wba_ref, wbp_ref,
              wo_ref, g2_ref, x1_ref, h2_ref, pe_ref):
    t = pl.program_id(1)
    tm = MIX_TILE
    sub = MIX_TILE // MIX_SPLIT

    @pl.when(t == 0)
    def _():
        pe_ref[0:POOL_HALO, :] = jnp.zeros((POOL_HALO, POOL_W), F32)

    @pl.when(t > 0)
    def _():
        pe_ref[0:POOL_HALO, :] = pe_ref[tm:tm + POOL_HALO, :]

    pe_ref[POOL_HALO:POOL_HALO + tm, :] = p_ref[0].astype(F32)

    for part in range(MIX_SPLIT):
        r0 = part * sub
        rows = slice(r0, r0 + sub)
        pos = t * tm + r0 + lax.broadcasted_iota(jnp.int32, (sub, POOL_GROUP_W), 0)
        pooled = []
        for g, win in enumerate(POOL_WINDOWS):
            cols = slice(g * POOL_GROUP_W, (g + 1) * POOL_GROUP_W)
            ext = pe_ref[r0:r0 + POOL_HALO + sub, cols]
            tot = ext
            span = 1
            while span < win:
                tot = tot + pltpu.roll(tot, span, 0)
                span *= 2
            count = jnp.minimum(pos + 1, win).astype(F32)
            pooled_g = (tot[POOL_HALO:] / count - ext[POOL_HALO:]).astype(BF16)
            pooled.append(_dot(pooled_g, pw_ref[g]) * ps_ref[:, cols])
        pool = jnp.concatenate(pooled, axis=-1).astype(BF16)

        a_proj = _dot(attn_ref[0, rows, :], wba_ref[...])
        p_proj = _dot(pool, wbp_ref[...])
        merged = (_sigmoid(ga_ref[0, rows, :].astype(F32)) * a_proj
                  + _sigmoid(gp_ref[0, rows, :].astype(F32)) * p_proj).astype(BF16)
        x1 = x_ref[0, rows, :] + mod_ref[0, 2:3, :] * _dot(merged, wo_ref[...])
        x1_ref[0, rows, :] = x1
        h2_ref[0, rows, :] = _norm_mod(x1, g2_ref[...], mod_ref[0, 4:5, :], mod_ref[0, 3:4, :]).astype(BF16)


def _mix(x, attn, z, mod, pool_w_bf, pool_scale, wba_bf, wbp_bf, wo_bf, g2):
    b, s, d = x.shape
    tm = MIX_TILE
    const2 = lambda bi, t: (0, 0)
    return pl.pallas_call(
        _mix_body,
        grid=(b, s // tm),
        in_specs=[
            pl.BlockSpec((1, tm, d), lambda bi, t: (bi, t, 0)),
            pl.BlockSpec((1, tm, ATTN_W), lambda bi, t: (bi, t, 0)),
            pl.BlockSpec((1, tm, POOL_W), lambda bi, t: (bi, t, 3 * ATTN_W // POOL_W)),
            pl.BlockSpec((1, tm, d), lambda bi, t: (bi, t, (3 * ATTN_W + POOL_W) // d)),
            pl.BlockSpec((1, tm, d), lambda bi, t: (bi, t, (3 * ATTN_W + POOL_W) // d + 1)),
            pl.BlockSpec((1, 6, d), lambda bi, t: (bi, 0, 0)),
            pl.BlockSpec((POOL_GROUPS, POOL_GROUP_W, POOL_GROUP_W), lambda bi, t: (0, 0, 0)),
            pl.BlockSpec((1, POOL_W), const2),
            pl.BlockSpec((ATTN_W, d), const2),
            pl.BlockSpec((POOL_W, d), const2),
            pl.BlockSpec((d, d), const2),
            pl.BlockSpec((1, d), const2),
        ],
        out_specs=[
            pl.BlockSpec((1, tm, d), lambda bi, t: (bi, t, 0)),
            pl.BlockSpec((1, tm, d), lambda bi, t: (bi, t, 0)),
        ],
        out_shape=[
            jax.ShapeDtypeStruct((b, s, d), F32),
            jax.ShapeDtypeStruct((b, s, d), BF16),
        ],
        scratch_shapes=[pltpu.VMEM((tm + POOL_HALO, POOL_W), F32)],
        compiler_params=pltpu.CompilerParams(
            dimension_semantics=("arbitrary", "arbitrary"), vmem_limit_bytes=VMEM_LIMIT),
        name="mix",
    )(x, attn, z, z, z, mod, pool_w_bf, pool_scale, wba_bf, wbp_bf, wo_bf, g2)


def _ffn_body(x1_ref, h2_ref, mod_ref, wup_ref, cw_ref, cb_ref, wdn_ref, o_ref,
              hist0_ref, hist1_ref, carry_ref):
    t = pl.program_id(1)
    sub = FFN_TILE // FFN_SPLIT
    fc = FFN_CHUNK
    n_chunks = D_FF // fc
    hist = (hist0_ref, hist1_ref)

    @pl.when(t == 0)
    def _():
        carry_ref[...] = jnp.zeros_like(carry_ref)

    units = [(part, c) for part in range(FFN_SPLIT) for c in range(n_chunks)]
    h2 = [h2_ref[0, part * sub:(part + 1) * sub, :] for part in range(FFN_SPLIT)]

    def up(k):
        part, c = units[k]
        out = []
        for half in range(2):
            cols = slice(half * D_FF + c * fc, half * D_FF + (c + 1) * fc)
            hcols = slice(half * fc, (half + 1) * fc)
            r = _dot(h2[part], wup_ref[:, cols])
            hist[k % 2][0:CONV_HALO, hcols] = carry_ref[:, cols]
            hist[k % 2][CONV_HALO:CONV_HALO + sub, hcols] = r
            carry_ref[:, cols] = r[sub - CONV_HALO:sub]
            out.append(r)
        return out

    def down(acc, k, act):
        part, c = units[k]
        acc[part] = acc[part] + _dot(act, wdn_ref[c * fc:(c + 1) * fc, :])

    acc = [jnp.zeros((sub, D_MODEL), F32) for _ in range(FFN_SPLIT)]
    r_next = up(0)
    act_prev = None
    for k, (part, c) in enumerate(units):
        r_cur = r_next
        if k + 1 < len(units):
            r_next = up(k + 1)
        if act_prev is not None:
            down(acc, k - 1, act_prev)
        halves = []
        for half in range(2):
            cols = slice(half * D_FF + c * fc, half * D_FF + (c + 1) * fc)
            hcols = slice(half * fc, (half + 1) * fc)
            u = cb_ref[:, cols] + cw_ref[CONV_W - 1:CONV_W, cols] * r_cur[half]
            for tap in range(CONV_W - 1):
                back = CONV_W - 1 - tap
                u = u + cw_ref[tap:tap + 1, cols] * hist[k % 2][CONV_HALO - back:CONV_HALO - back + sub, hcols]
            halves.append(u)
        u_g, u_v = halves
        act_prev = (u_g * _sigmoid(u_g) * u_v).astype(BF16)
    down(acc, len(units) - 1, act_prev)
    for part in range(FFN_SPLIT):
        rows = slice(part * sub, (part + 1) * sub)
        o_ref[0, rows, :] = x1_ref[0, rows, :] + mod_ref[0, 5:6, :] * acc[part]


def _ffn(x1, h2, mod, wup_bf, conv_w, conv_b, wdn_bf):
    b, s, d = x1.shape
    tm = FFN_TILE
    const2 = lambda bi, t: (0, 0)
    return pl.pallas_call(
        _ffn_body,
        grid=(b, s // tm),
        in_specs=[
            pl.BlockSpec((1, tm, d), lambda bi, t: (bi, t, 0)),
            pl.BlockSpec((1, tm, d), lambda bi, t: (bi, t, 0)),
            pl.BlockSpec((1, 6, d), lambda bi, t: (bi, 0, 0)),
            pl.BlockSpec((d, 2 * D_FF), const2),
            pl.BlockSpec((CONV_W, 2 * D_FF), const2),
            pl.BlockSpec((1, 2 * D_FF), const2),
            pl.BlockSpec((D_FF, d), const2),
        ],
        out_specs=pl.BlockSpec((1, tm, d), lambda bi, t: (bi, t, 0)),
        out_shape=jax.ShapeDtypeStruct((b, s, d), F32),
        scratch_shapes=[
            pltpu.VMEM((tm // FFN_SPLIT + CONV_HALO, 2 * FFN_CHUNK), F32),
            pltpu.VMEM((tm // FFN_SPLIT + CONV_HALO, 2 * FFN_CHUNK), F32),
            pltpu.VMEM((CONV_HALO, 2 * D_FF), F32),
        ],
        compiler_params=pltpu.CompilerParams(
            dimension_semantics=("arbitrary", "arbitrary"), vmem_limit_bytes=VMEM_LIMIT),
        name="ffn",
    )(x1, h2, mod, wup_bf, conv_w, conv_b, wdn_bf)


def _layer(x, c_pad, rel_bias, ada_w, ada_b, norm1_g, w_in, q_norm_g, k_norm_g, pool_w, pool_scale,
           w_branch_attn, w_branch_pool, w_out, norm2_g, w_up, conv_w, conv_b, w_down):
    b, s, d = x.shape
    mod = _ada(c_pad, ada_w, ada_b[None, :])[:b].reshape(b, 6, d)
    z = _inproj(x, mod, norm1_g[None, :], w_in.astype(BF16))
    gq_t = jnp.broadcast_to(jnp.tile(q_norm_g, HEADS_PER_STEP)[:, None], (PAIR_W, MOBA_BLOCK))
    gk_row = jnp.tile(k_norm_g, HEADS_PER_STEP)[None, :]
    attn = _attention(z, gq_t, gk_row, _biasprep(rel_bias))
    x1, h2 = _mix(x, attn, z, mod, pool_w.astype(BF16), pool_scale[None, :],
                  w_branch_attn.astype(BF16), w_branch_pool.astype(BF16), w_out.astype(BF16),
                  norm2_g[None, :])
    return _ffn(x1, h2, mod, w_up.astype(BF16), conv_w, conv_b[None, :], w_down.astype(BF16))


def kernel(x, c, ada_w, ada_b, norm1_g, w_in, q_norm_g, k_norm_g, rel_bias, pool_w, pool_scale,
           w_branch_attn, w_branch_pool, w_out, norm2_g, w_up, conv_w, conv_b, w_down):
    b, s, d = x.shape
    assert d == D_MODEL and s % MOBA_BLOCK == 0 and w_in.shape[-1] == IN_W
    c_pad = jnp.zeros((8, d), F32).at[:b].set(c)
    for l in range(ada_w.shape[0]):
        x = _layer(x, c_pad, rel_bias, ada_w[l], ada_b[l], norm1_g[l], w_in[l], q_norm_g[l],
                   k_norm_g[l], pool_w[l], pool_scale[l], w_branch_attn[l], w_branch_pool[l],
                   w_out[l], norm2_g[l], w_up[l], conv_w[l], conv_b[l], w_down[l])
    return x
```

```python
import functools
import math

import numpy as np
import jax
import jax.numpy as jnp
from jax import lax
from jax.experimental import pallas as pl
from jax.experimental.pallas import tpu as pltpu

F32 = jnp.float32
BF16 = jnp.bfloat16

D_MODEL = 1024
ATTN_HEADS = 8
HEAD_DIM = 64
ATTN_W = ATTN_HEADS * HEAD_DIM
MOBA_BLOCK = 256
MOBA_TOPK = 3
POOL_GROUPS = 4
POOL_GROUP_W = 128
POOL_W = POOL_GROUPS * POOL_GROUP_W
POOL_WINDOWS = (2, 4, 8, 16)
NUM_BUCKETS = 32
MAX_DISTANCE = 128
D_FF = 2816
CONV_W = 3
EPS = 1e-6
IN_W = 3 * ATTN_W + POOL_W + 2 * D_MODEL

HEADS_PER_STEP = 2
PAIR_W = HEADS_PER_STEP * HEAD_DIM
N_PAIRS = ATTN_HEADS // HEADS_PER_STEP
V_ROWS = HEAD_DIM + 16
LOG2E = math.log2(math.e)
MASK_BIG = 1e30
INPROJ_TILE = 512
INPROJ_SPLIT = 2
POOL_HALO = 16
MIX_TILE = 512
MIX_SPLIT = 2
FFN_CHUNK = 256
FFN_TILE = 512
FFN_SPLIT = 2
CONV_HALO = 8
VMEM_LIMIT = 56 * 1024 * 1024


def _split_bf16(a):
    hi = a.astype(BF16)
    lo = (a - hi.astype(F32)).astype(BF16)
    return hi, lo


def _dot(a, b):
    return jnp.dot(a, b, preferred_element_type=F32)


def _dot3(a, b):
    ah, al = _split_bf16(a)
    bh, bl = _split_bf16(b)
    return _dot(ah, bh) + _dot(al, bh) + _dot(ah, bl)


def _sigmoid(v):
    return 1.0 / (1.0 + jnp.exp2(v * (-LOG2E)))


def _cast_weight_once(src_ref, dst_ref, col_chunk=512):
    first = functools.reduce(jnp.logical_and, [pl.program_id(a) == 0 for a in range(2)])

    @pl.when(first)
    def _():
        n = src_ref.shape[-1]
        for c0 in range(0, n, col_chunk):
            dst_ref[..., c0:c0 + col_chunk] = src_ref[..., c0:c0 + col_chunk].astype(BF16)


def _ada_body(c_ref, w_ref, b_ref, o_ref):
    n_batch = c_ref.shape[0]
    tn = w_ref.shape[1]
    lanes = c_ref.shape[2]
    for bi in range(n_batch):
        c = c_ref[bi]
        act = c * _sigmoid(c)
        cols = [jnp.sum(act * w_ref[:, g * lanes:(g + 1) * lanes], axis=0, keepdims=True)
                for g in range(tn // lanes)]
        o_ref[bi:bi + 1, :] = jnp.concatenate(cols, axis=-1) + b_ref[...]


def _ada(c_lanes, ada_w, ada_b):
    n_batch, d, lanes = c_lanes.shape
    n = ada_w.shape[1]
    tn = 1024
    return pl.pallas_call(
        _ada_body,
        grid=(n // tn,),
        in_specs=[
            pl.BlockSpec((n_batch, d, lanes), lambda j: (0, 0, 0)),
            pl.BlockSpec((d, tn), lambda j: (0, j)),
            pl.BlockSpec((1, tn), lambda j: (0, j)),
        ],
        out_specs=pl.BlockSpec((n_batch, tn), lambda j: (0, j)),
        out_shape=jax.ShapeDtypeStruct((n_batch, n), F32),
        compiler_params=pltpu.CompilerParams(
            dimension_semantics=("arbitrary",), vmem_limit_bytes=VMEM_LIMIT),
        name="ada",
    )(c_lanes, ada_w, ada_b)


def _norm_mod(x, gain, scale, shift):
    ms = jnp.mean(x * x, axis=-1, keepdims=True)
    y = x * lax.rsqrt(ms + EPS) * gain
    return y * (1.0 + scale) + shift


def _inproj_body(x_ref, mod_ref, g_ref, w32_ref, z_ref, w_ref):
    _cast_weight_once(w32_ref, w_ref)
    sub = INPROJ_TILE // INPROJ_SPLIT
    n_chunk = 1024
    for part in range(INPROJ_SPLIT):
        rows = slice(part * sub, (part + 1) * sub)
        h = _norm_mod(x_ref[0, rows, :], g_ref[...], mod_ref[0, 1:2, :], mod_ref[0, 0:1, :]).astype(BF16)
        for n in range(IN_W // n_chunk):
            cols = slice(n * n_chunk, (n + 1) * n_chunk)
            z_ref[0, rows, cols] = _dot(h, w_ref[:, cols]).astype(BF16)


def _inproj(x, mod, g1, w_in):
    b, s, d = x.shape
    return pl.pallas_call(
        _inproj_body,
        grid=(b, s // INPROJ_TILE),
        in_specs=[
            pl.BlockSpec((1, INPROJ_TILE, d), lambda bi, t: (bi, t, 0)),
            pl.BlockSpec((1, 6, d), lambda bi, t: (bi, 0, 0)),
            pl.BlockSpec((1, d), lambda bi, t: (0, 0)),
            pl.BlockSpec((d, IN_W), lambda bi, t: (0, 0)),
        ],
        out_specs=pl.BlockSpec((1, INPROJ_TILE, IN_W), lambda bi, t: (bi, t, 0)),
        out_shape=jax.ShapeDtypeStruct((b, s, IN_W), BF16),
        scratch_shapes=[pltpu.VMEM((d, IN_W), BF16)],
        compiler_params=pltpu.CompilerParams(
            dimension_semantics=("arbitrary", "arbitrary"), vmem_limit_bytes=VMEM_LIMIT),
        name="inproj",
    )(x, mod, g1, w_in)


def _t5_bucket_np(dist):
    max_exact = NUM_BUCKETS // 2
    n = np.maximum(dist, 0)
    nf = np.maximum(n, 1).astype(np.float64)
    large = max_exact + (np.log(nf / max_exact) / math.log(MAX_DISTANCE / max_exact)
                         * (NUM_BUCKETS - max_exact)).astype(np.int32)
    large = np.minimum(large, NUM_BUCKETS - 1)
    return np.where(n < max_exact, n, large).astype(np.int32)


def _bucket_tables():
    k = np.arange(MOBA_BLOCK)[:, None]
    q = np.arange(MOBA_BLOCK)[None, :]
    own = np.where(q - k >= 0, _t5_bucket_np(q - k), -1)
    prev = _t5_bucket_np(MOBA_BLOCK + q - k)
    return np.stack([own, prev]).astype(np.int32)


def _biasprep_body(rb_ref, bkt_ref, o_ref):
    h = pl.program_id(0)
    far = rb_ref[NUM_BUCKETS - 1, h]
    for t in range(2):
        bkt = bkt_ref[t]
        acc = jnp.where(bkt < 0, -MASK_BIG, 0.0).astype(F32)
        for bk in range(NUM_BUCKETS):
            acc = jnp.where(bkt == bk, (rb_ref[bk, h] - far) * LOG2E, acc)
        o_ref[0, t] = acc


def _biasprep(rel_bias):
    bkt = jnp.asarray(_bucket_tables())
    return pl.pallas_call(
        _biasprep_body,
        grid=(ATTN_HEADS,),
        in_specs=[
            pl.BlockSpec(memory_space=pltpu.SMEM),
            pl.BlockSpec((2, MOBA_BLOCK, MOBA_BLOCK), lambda h: (0, 0, 0)),
        ],
        out_specs=pl.BlockSpec((1, 2, MOBA_BLOCK, MOBA_BLOCK), lambda h: (h, 0, 0, 0)),
        out_shape=jax.ShapeDtypeStruct((ATTN_HEADS, 2, MOBA_BLOCK, MOBA_BLOCK), F32),
        compiler_params=pltpu.CompilerParams(dimension_semantics=("arbitrary",)),
        name="biasprep",
    )(rel_bias, bkt)


PREP_UNROLL = 4
ITEM_FIELDS = 5
PIPE_UNROLL = 16


def _far_items(n_blocks):
    never = n_blocks - 1
    items = []
    for i in range(n_blocks):
        n_far = max(i - 1, 0)
        for j0 in range(0, n_far, 2):
            items.append((i, j0, j0, j0 + 1, j0 + 1) if j0 + 1 < n_far else (i, j0, j0, j0, never))
    dummy = (n_blocks - 1, 0, never, 0, never)
    items.extend([dummy] * (-len(items) % PIPE_UNROLL))
    items.append(dummy)
    return np.asarray(items, np.int32)


def _attn_body(tbl_ref, zq_ref, zk_ref, zv_ref, gq_ref, gk_ref, bias_ref, o_ref,
               kn_ref, vta_ref, w_ref, pen_ref, acc_ref, m_ref, s0_ref, s1_ref, *, n_blocks, n_items):
    blk = MOBA_BLOCK
    heads = range(HEADS_PER_STEP)

    def rows_of(j):
        return pl.ds(pl.multiple_of(j * blk, blk), blk)

    blk_rows = lax.broadcasted_iota(jnp.int32, (n_blocks, PAIR_W), 0)
    ones = jnp.ones((V_ROWS - HEAD_DIM, blk), BF16)
    same_head = (lax.broadcasted_iota(jnp.int32, (PAIR_W, PAIR_W), 0) // HEAD_DIM
                 == lax.broadcasted_iota(jnp.int32, (PAIR_W, PAIR_W), 1) // HEAD_DIM)
    head_sum = jnp.where(same_head, 1.0, 0.0).astype(BF16)

    def kv_prep(i, kmean):
        kraw = zk_ref[0, rows_of(i), :].astype(F32)
        sq_hi, sq_lo = _split_bf16(kraw * kraw)
        ss = _dot(sq_hi, head_sum) + _dot(sq_lo, head_sum)
        rinv = lax.rsqrt(ss * (1.0 / HEAD_DIM) + EPS)
        kn = kraw * rinv * gk_ref[...]
        kn_ref[rows_of(i), :] = kn.astype(BF16)
        vt = zv_ref[0, rows_of(i), :].astype(F32).T
        for hd in heads:
            vta_ref[i, hd, 0:HEAD_DIM, :] = vt[hd * HEAD_DIM:(hd + 1) * HEAD_DIM].astype(BF16)
            vta_ref[i, hd, HEAD_DIM:V_ROWS, :] = ones
        return jnp.where(blk_rows == i, jnp.mean(kn, axis=0, keepdims=True), kmean)

    kmean = lax.fori_loop(0, n_blocks, kv_prep, jnp.zeros((n_blocks, PAIR_W), F32), unroll=PREP_UNROLL)

    blk_ids = lax.broadcasted_iota(jnp.int32, (n_blocks, blk), 0)
    zeros_half = jnp.zeros((HEAD_DIM, blk), F32)

    def q_prep(i, _):
        qt = zq_ref[0, rows_of(i), :].astype(F32).T
        qsq = qt * qt
        for hd in heads:
            rows = slice(hd * HEAD_DIM, (hd + 1) * HEAD_DIM)
            ss = jnp.sum(qsq[rows], axis=0, keepdims=True)
            r = lax.rsqrt(ss * (1.0 / HEAD_DIM) + EPS)
            qn = qt[rows] * r * gq_ref[rows, :] * (HEAD_DIM ** -0.5 * LOG2E)
            parts = [zeros_half] * HEADS_PER_STEP
            parts[hd] = qn
            w = jnp.concatenate(parts, axis=0)
            w_ref[i, hd] = w.astype(BF16)
            gate = jnp.where(blk_ids < i, _dot3(kmean, w), -jnp.inf)
            sel = jnp.zeros((n_blocks, blk), jnp.bool_)
            for _ in range(MOBA_TOPK):
                mx = jnp.max(gate, axis=0, keepdims=True)
                cand = jnp.where(gate == mx, blk_ids, n_blocks)
                pick = (blk_ids == jnp.min(cand, axis=0, keepdims=True)) & (mx > -jnp.inf)
                sel = sel | pick
                gate = jnp.where(pick, -jnp.inf, gate)
            pen_ref[hd, i] = jnp.where(sel, 0.0, -MASK_BIG).astype(F32)
        return 0

    lax.fori_loop(0, n_blocks, q_prep, 0, unroll=PREP_UNROLL)

    def score(i, blocks, s_ref, biased):
        keys = [kn_ref[rows_of(j), :] for j, _ in blocks]
        gmax = []
        for hd in heads:
            w = w_ref[i, hd]
            tmax = []
            for u, (_, jpen) in enumerate(blocks):
                s = _dot(keys[u], w)
                if biased:
                    s = s + bias_ref[hd, u]
                s_ref[hd, u] = s
                t = jnp.max(s, axis=0, keepdims=True)
                tmax.append(t if jpen is None else t + pen_ref[hd, i, pl.ds(jpen, 1), :])
            gmax.append(functools.reduce(jnp.maximum, tmax))
        return tuple(gmax)

    def attend(i, blocks, s_ref, gmax, first_visit):
        for hd in heads:
            if first_visit:
                m_new = gmax[hd]
                acc = None
            else:
                m_old = m_ref[i, hd]
                m_new = jnp.maximum(m_old, gmax[hd])
                acc = acc_ref[i, hd] * jnp.exp2(m_old - m_new)
            for u, (j, jpen) in enumerate(blocks):
                shift = m_new if jpen is None else m_new - pen_ref[hd, i, pl.ds(jpen, 1), :]
                p = jnp.exp2(s_ref[hd, u] - shift).astype(BF16)
                pv = _dot(vta_ref[j, hd], p)
                acc = pv if acc is None else acc + pv
            acc_ref[i, hd] = acc
            m_ref[i, hd] = m_new

    def near_blocks(i):
        jp = jnp.maximum(i - 1, 0)
        return ((i, None), (jp, jp))

    slots = (s0_ref, s1_ref)

    def near_body(t, g):
        for u in range(PIPE_UNROLL):
            i_cur = PIPE_UNROLL * t + u
            i_nxt = jnp.minimum(i_cur + 1, n_blocks - 1)
            g_nxt = score(i_nxt, near_blocks(i_nxt), slots[(u + 1) % 2], True)
            attend(i_cur, near_blocks(i_cur), slots[u % 2], g, True)
            g = g_nxt
        return g

    lax.fori_loop(0, n_blocks // PIPE_UNROLL, near_body, score(0, near_blocks(0), s0_ref, True))

    def item(k):
        base = k * ITEM_FIELDS
        i = tbl_ref[base]
        return i, ((tbl_ref[base + 1], tbl_ref[base + 2]), (tbl_ref[base + 3], tbl_ref[base + 4]))

    def far_body(t, g):
        for u in range(PIPE_UNROLL):
            k = PIPE_UNROLL * t + u
            g_nxt = score(*item(k + 1), slots[(u + 1) % 2], False)
            attend(*item(k), slots[u % 2], g, False)
            g = g_nxt
        return g

    lax.fori_loop(0, n_items // PIPE_UNROLL, far_body, score(*item(0), s0_ref, False))

    def finish(i, _):
        outs = []
        for hd in heads:
            acc = acc_ref[i, hd]
            outs.append(acc[0:HEAD_DIM] / acc[HEAD_DIM:HEAD_DIM + 1])
        o_ref[0, rows_of(i), :] = jnp.concatenate(outs, axis=0).T.astype(BF16)
        return 0

    lax.fori_loop(0, n_blocks, finish, 0, unroll=PREP_UNROLL)


def _attention(z, gq_t, gk_row, bias):
    b, s, _ = z.shape
    n_blocks = s // MOBA_BLOCK
    assert n_blocks % PIPE_UNROLL == 0 and PIPE_UNROLL % 2 == 0
    blk = MOBA_BLOCK
    items = _far_items(n_blocks)
    n_items = items.shape[0] - 1
    grid_spec = pltpu.PrefetchScalarGridSpec(
        num_scalar_prefetch=1,
        grid=(b, N_PAIRS),
        in_specs=[
            pl.BlockSpec((1, s, PAIR_W), lambda bi, hp, tbl: (bi, 0, hp)),
            pl.BlockSpec((1, s, PAIR_W), lambda bi, hp, tbl: (bi, 0, N_PAIRS + hp)),
            pl.BlockSpec((1, s, PAIR_W), lambda bi, hp, tbl: (bi, 0, 2 * N_PAIRS + hp)),
            pl.BlockSpec((PAIR_W, blk), lambda bi, hp, tbl: (0, 0)),
            pl.BlockSpec((1, PAIR_W), lambda bi, hp, tbl: (0, 0)),
            pl.BlockSpec((HEADS_PER_STEP, 2, blk, blk), lambda bi, hp, tbl: (hp, 0, 0, 0)),
        ],
        out_specs=pl.BlockSpec((1, s, PAIR_W), lambda bi, hp, tbl: (bi, 0, hp)),
        scratch_shapes=[
            pltpu.VMEM((s, PAIR_W), BF16),
            pltpu.VMEM((n_blocks, HEADS_PER_STEP, V_ROWS, blk), BF16),
            pltpu.VMEM((n_blocks, HEADS_PER_STEP, PAIR_W, blk), BF16),
            pltpu.VMEM((HEADS_PER_STEP, n_blocks, n_blocks, blk), F32),
            pltpu.VMEM((n_blocks, HEADS_PER_STEP, V_ROWS, blk), F32),
            pltpu.VMEM((n_blocks, HEADS_PER_STEP, 1, blk), F32),
            pltpu.VMEM((HEADS_PER_STEP, 2, blk, blk), F32),
            pltpu.VMEM((HEADS_PER_STEP, 2, blk, blk), F32),
        ],
    )
    return pl.pallas_call(
        functools.partial(_attn_body, n_blocks=n_blocks, n_items=n_items),
        grid_spec=grid_spec,
        out_shape=jax.ShapeDtypeStruct((b, s, ATTN_W), BF16),
        compiler_params=pltpu.CompilerParams(
            dimension_semantics=("arbitrary", "arbitrary"), vmem_limit_bytes=VMEM_LIMIT),
        name="attn",
    )(jnp.asarray(items.reshape(-1)), z, z, z, gq_t, gk_row, bias)


def _mix_body(x_ref, attn_ref, p_ref, ga_ref, gp_ref, mod_ref, pw32_ref, ps_ref, wba32_ref, wbp32_ref,
              wo32_ref, g2_ref, x1_ref, h2_ref, pe_ref, pw_ref, wba_ref, wbp_ref, wo_ref):
    for w32, wbf in ((pw32_ref, pw_ref), (wba32_ref, wba_ref), (wbp32_ref, wbp_ref), (wo32_ref, wo_ref)):
        _cast_weight_once(w32, wbf)
    t = pl.program_id(1)
    tm = MIX_TILE
    sub = MIX_TILE // MIX_SPLIT

    @pl.when(t == 0)
    def _():
        pe_ref[0:POOL_HALO, :] = jnp.zeros((POOL_HALO, POOL_W), F32)

    @pl.when(t > 0)
    def _():
        pe_ref[0:POOL_HALO, :] = pe_ref[tm:tm + POOL_HALO, :]

    pe_ref[POOL_HALO:POOL_HALO + tm, :] = p_ref[0].astype(F32)

    for part in range(MIX_SPLIT):
        r0 = part * sub
        rows = slice(r0, r0 + sub)
        pos = t * tm + r0 + lax.broadcasted_iota(jnp.int32, (sub, POOL_GROUP_W), 0)
        pooled = []
        for g, win in enumerate(POOL_WINDOWS):
            cols = slice(g * POOL_GROUP_W, (g + 1) * POOL_GROUP_W)
            ext = pe_ref[r0:r0 + POOL_HALO + sub, cols]
            tot = ext
            span = 1
            while span < win:
                tot = tot + pltpu.roll(tot, span, 0)
                span *= 2
            count = jnp.minimum(pos + 1, win).astype(F32)
            pooled_g = (tot[POOL_HALO:] / count - ext[POOL_HALO:]).astype(BF16)
            pooled.append(_dot(pooled_g, pw_ref[g]) * ps_ref[:, cols])
        pool = jnp.concatenate(pooled, axis=-1).astype(BF16)

        a_proj = _dot(attn_ref[0, rows, :], wba_ref[...])
        p_proj = _dot(pool, wbp_ref[...])
        merged = (_sigmoid(ga_ref[0, rows, :].astype(F32)) * a_proj
                  + _sigmoid(gp_ref[0, rows, :].astype(F32)) * p_proj).astype(BF16)
        x1 = x_ref[0, rows, :] + mod_ref[0, 2:3, :] * _dot(merged, wo_ref[...])
        x1_ref[0, rows, :] = x1
        h2_ref[0, rows, :] = _norm_mod(x1, g2_ref[...], mod_ref[0, 4:5, :], mod_ref[0, 3:4, :]).astype(BF16)


def _mix(x, attn, z, mod, pool_w, pool_scale, wba, wbp, wo, g2):
    b, s, d = x.shape
    tm = MIX_TILE
    const2 = lambda bi, t: (0, 0)
    return pl.pallas_call(
        _mix_body,
        grid=(b, s // tm),
        in_specs=[
            pl.BlockSpec((1, tm, d), lambda bi, t: (bi, t, 0)),
            pl.BlockSpec((1, tm, ATTN_W), lambda bi, t: (bi, t, 0)),
            pl.BlockSpec((1, tm, POOL_W), lambda bi, t: (bi, t, 3 * ATTN_W // POOL_W)),
            pl.BlockSpec((1, tm, d), lambda bi, t: (bi, t, (3 * ATTN_W + POOL_W) // d)),
            pl.BlockSpec((1, tm, d), lambda bi, t: (bi, t, (3 * ATTN_W + POOL_W) // d + 1)),
            pl.BlockSpec((1, 6, d), lambda bi, t: (bi, 0, 0)),
            pl.BlockSpec((POOL_GROUPS, POOL_GROUP_W, POOL_GROUP_W), lambda bi, t: (0, 0, 0)),
            pl.BlockSpec((1, POOL_W), const2),
            pl.BlockSpec((ATTN_W, d), const2),
            pl.BlockSpec((POOL_W, d), const2),
            pl.BlockSpec((d, d), const2),
            pl.BlockSpec((1, d), const2),
        ],
        out_specs=[
            pl.BlockSpec((1, tm, d), lambda bi, t: (bi, t, 0)),
            pl.BlockSpec((1, tm, d), lambda bi, t: (bi, t, 0)),
        ],
        out_shape=[
            jax.ShapeDtypeStruct((b, s, d), F32),
            jax.ShapeDtypeStruct((b, s, d), BF16),
        ],
        scratch_shapes=[
            pltpu.VMEM((tm + POOL_HALO, POOL_W), F32),
            pltpu.VMEM(pool_w.shape, BF16),
            pltpu.VMEM(wba.shape, BF16),
            pltpu.VMEM(wbp.shape, BF16),
            pltpu.VMEM(wo.shape, BF16),
        ],
        compiler_params=pltpu.CompilerParams(
            dimension_semantics=("arbitrary", "arbitrary"), vmem_limit_bytes=VMEM_LIMIT),
        name="mix",
    )(x, attn, z, z, z, mod, pool_w, pool_scale, wba, wbp, wo, g2)


def _ffn_body(x1_ref, h2_ref, mod_ref, wup_ref, cw_ref, cb_ref, wdn32_ref, o_ref,
              hist0_ref, hist1_ref, carry_ref, wdn_ref):
    _cast_weight_once(wdn32_ref, wdn_ref)
    t = pl.program_id(1)
    sub = FFN_TILE // FFN_SPLIT
    fc = FFN_CHUNK
    n_chunks = D_FF // fc
    hist = (hist0_ref, hist1_ref)

    @pl.when(t == 0)
    def _():
        carry_ref[...] = jnp.zeros_like(carry_ref)

    units = [(part, c) for part in range(FFN_SPLIT) for c in range(n_chunks)]
    h2 = [h2_ref[0, part * sub:(part + 1) * sub, :] for part in range(FFN_SPLIT)]

    def up(k):
        part, c = units[k]
        out = []
        for half in range(2):
            cols = slice(half * D_FF + c * fc, half * D_FF + (c + 1) * fc)
            hcols = slice(half * fc, (half + 1) * fc)
            r = _dot(h2[part], wup_ref[:, cols])
            hist[k % 2][0:CONV_HALO, hcols] = carry_ref[:, cols]
            hist[k % 2][CONV_HALO:CONV_HALO + sub, hcols] = r
            carry_ref[:, cols] = r[sub - CONV_HALO:sub]
            out.append(r)
        return out

    def down(acc, k, act):
        part, c = units[k]
        acc[part] = acc[part] + _dot(act, wdn_ref[c * fc:(c + 1) * fc, :])

    acc = [jnp.zeros((sub, D_MODEL), F32) for _ in range(FFN_SPLIT)]
    r_next = up(0)
    act_prev = None
    for k, (part, c) in enumerate(units):
        r_cur = r_next
        if k + 1 < len(units):
            r_next = up(k + 1)
        if act_prev is not None:
            down(acc, k - 1, act_prev)
        halves = []
        for half in range(2):
            cols = slice(half * D_FF + c * fc, half * D_FF + (c + 1) * fc)
            hcols = slice(half * fc, (half + 1) * fc)
            u = cb_ref[:, cols] + cw_ref[CONV_W - 1:CONV_W, cols] * r_cur[half]
            for tap in range(CONV_W - 1):
                back = CONV_W - 1 - tap
                u = u + cw_ref[tap:tap + 1, cols] * hist[k % 2][CONV_HALO - back:CONV_HALO - back + sub, hcols]
            halves.append(u)
        u_g, u_v = halves
        act_prev = (u_g * _sigmoid(u_g) * u_v).astype(BF16)
    down(acc, len(units) - 1, act_prev)
    for part in range(FFN_SPLIT):
        rows = slice(part * sub, (part + 1) * sub)
        o_ref[0, rows, :] = x1_ref[0, rows, :] + mod_ref[0, 5:6, :] * acc[part]


def _ffn(x1, h2, mod, wup_bf, conv_w, conv_b, w_down):
    b, s, d = x1.shape
    tm = FFN_TILE
    const2 = lambda bi, t: (0, 0)
    return pl.pallas_call(
        _ffn_body,
        grid=(b, s // tm),
        in_specs=[
            pl.BlockSpec((1, tm, d), lambda bi, t: (bi, t, 0)),
            pl.BlockSpec((1, tm, d), lambda bi, t: (bi, t, 0)),
            pl.BlockSpec((1, 6, d), lambda bi, t: (bi, 0, 0)),
            pl.BlockSpec((d, 2 * D_FF), const2),
            pl.BlockSpec((CONV_W, 2 * D_FF), const2),
            pl.BlockSpec((1, 2 * D_FF), const2),
            pl.BlockSpec((D_FF, d), const2),
        ],
        out_specs=pl.BlockSpec((1, tm, d), lambda bi, t: (bi, t, 0)),
        out_shape=jax.ShapeDtypeStruct((b, s, d), F32),
        scratch_shapes=[
            pltpu.VMEM((tm // FFN_SPLIT + CONV_HALO, 2 * FFN_CHUNK), F32),
            pltpu.VMEM((tm // FFN_SPLIT + CONV_HALO, 2 * FFN_CHUNK), F32),
            pltpu.VMEM((CONV_HALO, 2 * D_FF), F32),
            pltpu.VMEM((D_FF, d), BF16),
        ],
        compiler_params=pltpu.CompilerParams(
            dimension_semantics=("arbitrary", "arbitrary"), vmem_limit_bytes=VMEM_LIMIT),
        name="ffn",
    )(x1, h2, mod, wup_bf, conv_w, conv_b, w_down)


def _layer(x, c_lanes, rel_bias, ada_w, ada_b, norm1_g, w_in, q_norm_g, k_norm_g, pool_w, pool_scale,
           w_branch_attn, w_branch_pool, w_out, norm2_g, w_up, conv_w, conv_b, w_down):
    b, s, d = x.shape
    mod = _ada(c_lanes, ada_w, ada_b[None, :]).reshape(b, 6, d)
    z = _inproj(x, mod, norm1_g[None, :], w_in)
    gq_t = jnp.broadcast_to(jnp.tile(q_norm_g, HEADS_PER_STEP)[:, None], (PAIR_W, MOBA_BLOCK))
    gk_row = jnp.tile(k_norm_g, HEADS_PER_STEP)[None, :]
    attn = _attention(z, gq_t, gk_row, _biasprep(rel_bias))
    x1, h2 = _mix(x, attn, z, mod, pool_w, pool_scale[None, :], w_branch_attn, w_branch_pool, w_out,
                  norm2_g[None, :])
    return _ffn(x1, h2, mod, w_up.astype(BF16), conv_w, conv_b[None, :], w_down)


def kernel(x, c, ada_w, ada_b, norm1_g, w_in, q_norm_g, k_norm_g, rel_bias, pool_w, pool_scale,
           w_branch_attn, w_branch_pool, w_out, norm2_g, w_up, conv_w, conv_b, w_down):
    b, s, d = x.shape
    assert d == D_MODEL and s % MOBA_BLOCK == 0 and w_in.shape[-1] == IN_W
    c_lanes = jnp.broadcast_to(c[:, :, None], (b, d, 128))
    for l in range(ada_w.shape[0]):
        x = _layer(x, c_lanes, rel_bias, ada_w[l], ada_b[l], norm1_g[l], w_in[l], q_norm_g[l],
                   k_norm_g[l], pool_w[l], pool_scale[l], w_branch_attn[l], w_branch_pool[l],
                   w_out[l], norm2_g[l], w_up[l], conv_w[l], conv_b[l], w_down[l])
    return x
```

```python
import functools
import math

import numpy as np
import jax
import jax.numpy as jnp
from jax import lax
from jax.experimental import pallas as pl
from jax.experimental.pallas import tpu as pltpu

F32 = jnp.float32
BF16 = jnp.bfloat16

D_MODEL = 1024
ATTN_HEADS = 8
HEAD_DIM = 64
ATTN_W = ATTN_HEADS * HEAD_DIM
MOBA_BLOCK = 256
MOBA_TOPK = 3
POOL_GROUPS = 4
POOL_GROUP_W = 128
POOL_W = POOL_GROUPS * POOL_GROUP_W
POOL_WINDOWS = (2, 4, 8, 16)
NUM_BUCKETS = 32
MAX_DISTANCE = 128
D_FF = 2816
CONV_W = 3
EPS = 1e-6
IN_W = 3 * ATTN_W + POOL_W + 2 * D_MODEL

HEADS_PER_STEP = 2
PAIR_W = HEADS_PER_STEP * HEAD_DIM
N_PAIRS = ATTN_HEADS // HEADS_PER_STEP
V_ROWS = HEAD_DIM + 16
LOG2E = math.log2(math.e)
MASK_BIG = 1e30
INPROJ_TILE = 512
INPROJ_SPLIT = 2
POOL_HALO = 16
MIX_TILE = 512
MIX_SPLIT = 2
FFN_CHUNK = 256
FFN_TILE = 512
FFN_SPLIT = 2
CONV_HALO = 8
VMEM_LIMIT = 56 * 1024 * 1024


def _split_bf16(a):
    hi = a.astype(BF16)
    lo = (a - hi.astype(F32)).astype(BF16)
    return hi, lo


def _dot(a, b):
    return jnp.dot(a, b, preferred_element_type=F32)


def _dot3(a, b):
    ah, al = _split_bf16(a)
    bh, bl = _split_bf16(b)
    return _dot(ah, bh) + _dot(al, bh) + _dot(ah, bl)


def _sigmoid(v):
    return 1.0 / (1.0 + jnp.exp2(v * (-LOG2E)))


def _cast_weight_once(src_ref, dst_ref, col_chunk=512):
    first = functools.reduce(jnp.logical_and, [pl.program_id(a) == 0 for a in range(2)])

    @pl.when(first)
    def _():
        n = src_ref.shape[-1]
        for c0 in range(0, n, col_chunk):
            dst_ref[..., c0:c0 + col_chunk] = src_ref[..., c0:c0 + col_chunk].astype(BF16)


def _ada_body(c_ref, w_ref, b_ref, o_ref):
    n_batch, _, lanes = c_ref.shape
    n = w_ref.shape[1]

    @pl.when(pl.program_id(0) == 0)
    def _():
        o_ref[...] = jnp.broadcast_to(b_ref[...], o_ref.shape)

    for bi in range(n_batch):
        c = c_ref[bi]
        act = c * _sigmoid(c)
        cols = [jnp.sum(act * w_ref[:, g * lanes:(g + 1) * lanes], axis=0, keepdims=True)
                for g in range(n // lanes)]
        o_ref[bi:bi + 1, :] += jnp.concatenate(cols, axis=-1)


def _ada(c_lanes, ada_w, ada_b):
    n_batch, d, lanes = c_lanes.shape
    n = ada_w.shape[1]
    tk = 256
    return pl.pallas_call(
        _ada_body,
        grid=(d // tk,),
        in_specs=[
            pl.BlockSpec((n_batch, tk, lanes), lambda k: (0, k, 0)),
            pl.BlockSpec((tk, n), lambda k: (k, 0)),
            pl.BlockSpec((1, n), lambda k: (0, 0)),
        ],
        out_specs=pl.BlockSpec((n_batch, n), lambda k: (0, 0)),
        out_shape=jax.ShapeDtypeStruct((n_batch, n), F32),
        compiler_params=pltpu.CompilerParams(
            dimension_semantics=("arbitrary",), vmem_limit_bytes=VMEM_LIMIT),
        name="ada",
    )(c_lanes, ada_w, ada_b)


def _norm_mod(x, gain, scale, shift):
    ms = jnp.mean(x * x, axis=-1, keepdims=True)
    return x * lax.rsqrt(ms + EPS) * (gain * (1.0 + scale)) + shift


def _inproj_body(x_ref, mod_ref, g_ref, w32_ref, z_ref, w_ref):
    _cast_weight_once(w32_ref, w_ref)
    sub = INPROJ_TILE // INPROJ_SPLIT
    n_chunk = 1024
    for part in range(INPROJ_SPLIT):
        rows = slice(part * sub, (part + 1) * sub)
        h = _norm_mod(x_ref[0, rows, :], g_ref[...], mod_ref[0, 1:2, :], mod_ref[0, 0:1, :]).astype(BF16)
        for n in range(IN_W // n_chunk):
            cols = slice(n * n_chunk, (n + 1) * n_chunk)
            z_ref[0, rows, cols] = _dot(h, w_ref[:, cols]).astype(BF16)


def _inproj(x, mod, g1, w_in):
    b, s, d = x.shape
    return pl.pallas_call(
        _inproj_body,
        grid=(b, s // INPROJ_TILE),
        in_specs=[
            pl.BlockSpec((1, INPROJ_TILE, d), lambda bi, t: (bi, t, 0)),
            pl.BlockSpec((1, 6, d), lambda bi, t: (bi, 0, 0)),
            pl.BlockSpec((1, d), lambda bi, t: (0, 0)),
            pl.BlockSpec((d, IN_W), lambda bi, t: (0, 0)),
        ],
        out_specs=pl.BlockSpec((1, INPROJ_TILE, IN_W), lambda bi, t: (bi, t, 0)),
        out_shape=jax.ShapeDtypeStruct((b, s, IN_W), BF16),
        scratch_shapes=[pltpu.VMEM((d, IN_W), BF16)],
        compiler_params=pltpu.CompilerParams(
            dimension_semantics=("arbitrary", "arbitrary"), vmem_limit_bytes=VMEM_LIMIT),
        name="inproj",
    )(x, mod, g1, w_in)


def _t5_bucket_np(dist):
    max_exact = NUM_BUCKETS // 2
    n = np.maximum(dist, 0)
    nf = np.maximum(n, 1).astype(np.float64)
    large = max_exact + (np.log(nf / max_exact) / math.log(MAX_DISTANCE / max_exact)
                         * (NUM_BUCKETS - max_exact)).astype(np.int32)
    large = np.minimum(large, NUM_BUCKETS - 1)
    return np.where(n < max_exact, n, large).astype(np.int32)


BIAS_ROW_W = 4 * MOBA_BLOCK


def _bucket_row():
    d = np.arange(BIAS_ROW_W) - MOBA_BLOCK
    return np.where(d >= 0, _t5_bucket_np(d), -1).astype(np.int32)[None, :]


def _biasprep_body(rb_ref, bkt_ref, o_ref):
    h = pl.program_id(0)
    far = rb_ref[NUM_BUCKETS - 1, h]
    bkt = bkt_ref[...]
    row = jnp.where(bkt < 0, -MASK_BIG, 0.0).astype(F32)
    for bk in range(NUM_BUCKETS):
        row = jnp.where(bkt == bk, (rb_ref[bk, h] - far) * LOG2E, row)
    rows = jnp.broadcast_to(row, (MOBA_BLOCK, BIAS_ROW_W))
    rolled = pltpu.roll(rows, 0, 1, stride=1, stride_axis=0)
    o_ref[0, 0] = rolled[:, MOBA_BLOCK:2 * MOBA_BLOCK]
    o_ref[0, 1] = rolled[:, 2 * MOBA_BLOCK:3 * MOBA_BLOCK]


def _biasprep(rel_bias):
    bkt = jnp.asarray(_bucket_row())
    return pl.pallas_call(
        _biasprep_body,
        grid=(ATTN_HEADS,),
        in_specs=[
            pl.BlockSpec(memory_space=pltpu.SMEM),
            pl.BlockSpec((1, BIAS_ROW_W), lambda h: (0, 0)),
        ],
        out_specs=pl.BlockSpec((1, 2, MOBA_BLOCK, MOBA_BLOCK), lambda h: (h, 0, 0, 0)),
        out_shape=jax.ShapeDtypeStruct((ATTN_HEADS, 2, MOBA_BLOCK, MOBA_BLOCK), F32),
        compiler_params=pltpu.CompilerParams(dimension_semantics=("arbitrary",)),
        name="biasprep",
    )(rel_bias, bkt)


PREP_UNROLL = 4
ITEM_FIELDS = 5
PIPE_UNROLL = 16


def _far_items(n_blocks):
    never = n_blocks - 1
    items = []
    for i in range(n_blocks):
        n_far = max(i - 1, 0)
        for j0 in range(0, n_far, 2):
            items.append((i, j0, j0, j0 + 1, j0 + 1) if j0 + 1 < n_far else (i, j0, j0, j0, never))
    dummy = (n_blocks - 1, 0, never, 0, never)
    items.extend([dummy] * (-len(items) % PIPE_UNROLL))
    items.append(dummy)
    return np.asarray(items, np.int32)


def _attn_body(tbl_ref, zq_ref, zk_ref, zv_ref, gq_ref, gk_ref, bias_ref, o_ref,
               kn_ref, vta_ref, w_ref, pen_ref, acc_ref, m_ref, s0_ref, s1_ref, *, n_blocks, n_items):
    blk = MOBA_BLOCK
    heads = range(HEADS_PER_STEP)

    def rows_of(j):
        return pl.ds(pl.multiple_of(j * blk, blk), blk)

    blk_rows = lax.broadcasted_iota(jnp.int32, (n_blocks, PAIR_W), 0)
    ones = jnp.ones((V_ROWS - HEAD_DIM, blk), BF16)
    same_head = (lax.broadcasted_iota(jnp.int32, (PAIR_W, PAIR_W), 0) // HEAD_DIM
                 == lax.broadcasted_iota(jnp.int32, (PAIR_W, PAIR_W), 1) // HEAD_DIM)
    head_sum = jnp.where(same_head, 1.0, 0.0).astype(BF16)

    def kv_prep(i, kmean):
        kraw = zk_ref[0, rows_of(i), :].astype(F32)
        sq_hi, sq_lo = _split_bf16(kraw * kraw)
        ss = _dot(sq_hi, head_sum) + _dot(sq_lo, head_sum)
        rinv = lax.rsqrt(ss * (1.0 / HEAD_DIM) + EPS)
        kn = kraw * rinv * gk_ref[...]
        kn_ref[rows_of(i), :] = kn.astype(BF16)
        vt = zv_ref[0, rows_of(i), :].astype(F32).T
        for hd in heads:
            vta_ref[i, hd, 0:HEAD_DIM, :] = vt[hd * HEAD_DIM:(hd + 1) * HEAD_DIM].astype(BF16)
            vta_ref[i, hd, HEAD_DIM:V_ROWS, :] = ones
        return jnp.where(blk_rows == i, jnp.mean(kn, axis=0, keepdims=True), kmean)

    kmean = lax.fori_loop(0, n_blocks, kv_prep, jnp.zeros((n_blocks, PAIR_W), F32), unroll=PREP_UNROLL)

    blk_ids = lax.broadcasted_iota(jnp.int32, (n_blocks, blk), 0)
    zeros_half = jnp.zeros((HEAD_DIM, blk), F32)

    def q_prep(i, _):
        qt = zq_ref[0, rows_of(i), :].astype(F32).T
        qsq = qt * qt
        for hd in heads:
            rows = slice(hd * HEAD_DIM, (hd + 1) * HEAD_DIM)
            ss = jnp.sum(qsq[rows], axis=0, keepdims=True)
            r = lax.rsqrt(ss * (1.0 / HEAD_DIM) + EPS)
            qn = qt[rows] * r * gq_ref[rows, :] * (HEAD_DIM ** -0.5 * LOG2E)
            parts = [zeros_half] * HEADS_PER_STEP
            parts[hd] = qn
            w = jnp.concatenate(parts, axis=0)
            w_ref[i, hd] = w.astype(BF16)
            gate = jnp.where(blk_ids < i, _dot3(kmean, w), -jnp.inf)
            sel = jnp.zeros((n_blocks, blk), jnp.bool_)
            for _ in range(MOBA_TOPK):
                mx = jnp.max(gate, axis=0, keepdims=True)
                cand = jnp.where(gate == mx, blk_ids, n_blocks)
                pick = (blk_ids == jnp.min(cand, axis=0, keepdims=True)) & (mx > -jnp.inf)
                sel = sel | pick
                gate = jnp.where(pick, -jnp.inf, gate)
            pen_ref[hd, i] = jnp.where(sel, 0.0, -MASK_BIG).astype(F32)
        return 0

    lax.fori_loop(0, n_blocks, q_prep, 0, unroll=PREP_UNROLL)

    def score(i, blocks, s_ref, biased):
        keys = [kn_ref[rows_of(j), :] for j, _ in blocks]
        gmax = []
        for hd in heads:
            w = w_ref[i, hd]
            tmax = []
            for u, (_, jpen) in enumerate(blocks):
                s = _dot(keys[u], w)
                if biased:
                    s = s + bias_ref[hd, u]
                s_ref[hd, u] = s
                t = jnp.max(s, axis=0, keepdims=True)
                tmax.append(t if jpen is None else t + pen_ref[hd, i, pl.ds(jpen, 1), :])
            gmax.append(functools.reduce(jnp.maximum, tmax))
        return tuple(gmax)

    def attend(i, blocks, s_ref, gmax, first_visit):
        for hd in heads:
            if first_visit:
                m_new = gmax[hd]
                acc = None
            else:
                m_old = m_ref[i, hd]
                m_new = jnp.maximum(m_old, gmax[hd])
                acc = acc_ref[i, hd] * jnp.exp2(m_old - m_new)
            for u, (j, jpen) in enumerate(blocks):
                shift = m_new if jpen is None else m_new - pen_ref[hd, i, pl.ds(jpen, 1), :]
                p = jnp.exp2(s_ref[hd, u] - shift).astype(BF16)
                pv = _dot(vta_ref[j, hd], p)
                acc = pv if acc is None else acc + pv
            acc_ref[i, hd] = acc
            m_ref[i, hd] = m_new

    def near_blocks(i):
        jp = jnp.maximum(i - 1, 0)
        return ((i, None), (jp, jp))

    slots = (s0_ref, s1_ref)

    def near_body(t, g):
        for u in range(PIPE_UNROLL):
            i_cur = PIPE_UNROLL * t + u
            i_nxt = jnp.minimum(i_cur + 1, n_blocks - 1)
            g_nxt = score(i_nxt, near_blocks(i_nxt), slots[(u + 1) % 2], True)
            attend(i_cur, near_blocks(i_cur), slots[u % 2], g, True)
            g = g_nxt
        return g

    lax.fori_loop(0, n_blocks // PIPE_UNROLL, near_body, score(0, near_blocks(0), s0_ref, True))

    def item(k):
        base = k * ITEM_FIELDS
        i = tbl_ref[base]
        return i, ((tbl_ref[base + 1], tbl_ref[base + 2]), (tbl_ref[base + 3], tbl_ref[base + 4]))

    def far_body(t, g):
        for u in range(PIPE_UNROLL):
            k = PIPE_UNROLL * t + u
            g_nxt = score(*item(k + 1), slots[(u + 1) % 2], False)
            attend(*item(k), slots[u % 2], g, False)
            g = g_nxt
        return g

    lax.fori_loop(0, n_items // PIPE_UNROLL, far_body, score(*item(0), s0_ref, False))

    def finish(i, _):
        outs = []
        for hd in heads:
            acc = acc_ref[i, hd]
            outs.append(acc[0:HEAD_DIM] / acc[HEAD_DIM:HEAD_DIM + 1])
        o_ref[0, rows_of(i), :] = jnp.concatenate(outs, axis=0).T.astype(BF16)
        return 0

    lax.fori_loop(0, n_blocks, finish, 0, unroll=PREP_UNROLL)


def _attention(z, gq_t, gk_row, bias):
    b, s, _ = z.shape
    n_blocks = s // MOBA_BLOCK
    assert n_blocks % PIPE_UNROLL == 0 and PIPE_UNROLL % 2 == 0
    blk = MOBA_BLOCK
    items = _far_items(n_blocks)
    n_items = items.shape[0] - 1
    grid_spec = pltpu.PrefetchScalarGridSpec(
        num_scalar_prefetch=1,
        grid=(b, N_PAIRS),
        in_specs=[
            pl.BlockSpec((1, s, PAIR_W), lambda bi, hp, tbl: (bi, 0, hp)),
            pl.BlockSpec((1, s, PAIR_W), lambda bi, hp, tbl: (bi, 0, N_PAIRS + hp)),
            pl.BlockSpec((1, s, PAIR_W), lambda bi, hp, tbl: (bi, 0, 2 * N_PAIRS + hp)),
            pl.BlockSpec((PAIR_W, blk), lambda bi, hp, tbl: (0, 0)),
            pl.BlockSpec((1, PAIR_W), lambda bi, hp, tbl: (0, 0)),
            pl.BlockSpec((HEADS_PER_STEP, 2, blk, blk), lambda bi, hp, tbl: (hp, 0, 0, 0)),
        ],
        out_specs=pl.BlockSpec((1, s, PAIR_W), lambda bi, hp, tbl: (bi, 0, hp)),
        scratch_shapes=[
            pltpu.VMEM((s, PAIR_W), BF16),
            pltpu.VMEM((n_blocks, HEADS_PER_STEP, V_ROWS, blk), BF16),
            pltpu.VMEM((n_blocks, HEADS_PER_STEP, PAIR_W, blk), BF16),
            pltpu.VMEM((HEADS_PER_STEP, n_blocks, n_blocks, blk), F32),
            pltpu.VMEM((n_blocks, HEADS_PER_STEP, V_ROWS, blk), F32),
            pltpu.VMEM((n_blocks, HEADS_PER_STEP, 1, blk), F32),
            pltpu.VMEM((HEADS_PER_STEP, 2, blk, blk), F32),
            pltpu.VMEM((HEADS_PER_STEP, 2, blk, blk), F32),
        ],
    )
    return pl.pallas_call(
        functools.partial(_attn_body, n_blocks=n_blocks, n_items=n_items),
        grid_spec=grid_spec,
        out_shape=jax.ShapeDtypeStruct((b, s, ATTN_W), BF16),
        compiler_params=pltpu.CompilerParams(
            dimension_semantics=("arbitrary", "arbitrary"), vmem_limit_bytes=VMEM_LIMIT),
        name="attn",
    )(jnp.asarray(items.reshape(-1)), z, z, z, gq_t, gk_row, bias)


def _mix_body(x_ref, attn_ref, p_ref, ga_ref, gp_ref, mod_ref, pw32_ref, ps_ref, wba32_ref, wbp32_ref,
              wo32_ref, g2_ref, x1_ref, h2_ref, pe_ref, pw_ref, wba_ref, wbp_ref, wo_ref):
    for w32, wbf in ((pw32_ref, pw_ref), (wba32_ref, wba_ref), (wbp32_ref, wbp_ref), (wo32_ref, wo_ref)):
        _cast_weight_once(w32, wbf)
    t = pl.program_id(1)
    tm = MIX_TILE
    sub = MIX_TILE // MIX_SPLIT

    @pl.when(t == 0)
    def _():
        pe_ref[0:POOL_HALO, :] = jnp.zeros((POOL_HALO, POOL_W), F32)

    @pl.when(t > 0)
    def _():
        pe_ref[0:POOL_HALO, :] = pe_ref[tm:tm + POOL_HALO, :]

    pe_ref[POOL_HALO:POOL_HALO + tm, :] = p_ref[0].astype(F32)

    for part in range(MIX_SPLIT):
        r0 = part * sub
        rows = slice(r0, r0 + sub)
        pos = t * tm + r0 + lax.broadcasted_iota(jnp.int32, (sub, POOL_GROUP_W), 0)
        pooled = []
        for g, win in enumerate(POOL_WINDOWS):
            cols = slice(g * POOL_GROUP_W, (g + 1) * POOL_GROUP_W)
            ext = pe_ref[r0:r0 + POOL_HALO + sub, cols]
            tot = ext
            span = 1
            while span < win:
                tot = tot + pltpu.roll(tot, span, 0)
                span *= 2
            count = jnp.minimum(pos + 1, win).astype(F32)
            pooled_g = (tot[POOL_HALO:] / count - ext[POOL_HALO:]).astype(BF16)
            pooled.append(_dot(pooled_g, pw_ref[g]) * ps_ref[:, cols])
        pool = jnp.concatenate(pooled, axis=-1).astype(BF16)

        a_proj = _dot(attn_ref[0, rows, :], wba_ref[...])
        p_proj = _dot(pool, wbp_ref[...])
        merged = (_sigmoid(ga_ref[0, rows, :].astype(F32)) * a_proj
                  + _sigmoid(gp_ref[0, rows, :].astype(F32)) * p_proj).astype(BF16)
        x1 = x_ref[0, rows, :] + mod_ref[0, 2:3, :] * _dot(merged, wo_ref[...])
        x1_ref[0, rows, :] = x1
        h2_ref[0, rows, :] = _norm_mod(x1, g2_ref[...], mod_ref[0, 4:5, :], mod_ref[0, 3:4, :]).astype(BF16)


def _mix(x, attn, z, mod, pool_w, pool_scale, wba, wbp, wo, g2):
    b, s, d = x.shape
    tm = MIX_TILE
    const2 = lambda bi, t: (0, 0)
    return pl.pallas_call(
        _mix_body,
        grid=(b, s // tm),
        in_specs=[
            pl.BlockSpec((1, tm, d), lambda bi, t: (bi, t, 0)),
            pl.BlockSpec((1, tm, ATTN_W), lambda bi, t: (bi, t, 0)),
            pl.BlockSpec((1, tm, POOL_W), lambda bi, t: (bi, t, 3 * ATTN_W // POOL_W)),
            pl.BlockSpec((1, tm, d), lambda bi, t: (bi, t, (3 * ATTN_W + POOL_W) // d)),
            pl.BlockSpec((1, tm, d), lambda bi, t: (bi, t, (3 * ATTN_W + POOL_W) // d + 1)),
            pl.BlockSpec((1, 6, d), lambda bi, t: (bi, 0, 0)),
            pl.BlockSpec((POOL_GROUPS, POOL_GROUP_W, POOL_GROUP_W), lambda bi, t: (0, 0, 0)),
            pl.BlockSpec((1, POOL_W), const2),
            pl.BlockSpec((ATTN_W, d), const2),
            pl.BlockSpec((POOL_W, d), const2),
            pl.BlockSpec((d, d), const2),
            pl.BlockSpec((1, d), const2),
        ],
        out_specs=[
            pl.BlockSpec((1, tm, d), lambda bi, t: (bi, t, 0)),
            pl.BlockSpec((1, tm, d), lambda bi, t: (bi, t, 0)),
        ],
        out_shape=[
            jax.ShapeDtypeStruct((b, s, d), F32),
            jax.ShapeDtypeStruct((b, s, d), BF16),
        ],
        scratch_shapes=[
            pltpu.VMEM((tm + POOL_HALO, POOL_W), F32),
            pltpu.VMEM(pool_w.shape, BF16),
            pltpu.VMEM(wba.shape, BF16),
            pltpu.VMEM(wbp.shape, BF16),
            pltpu.VMEM(wo.shape, BF16),
        ],
        compiler_params=pltpu.CompilerParams(
            dimension_semantics=("arbitrary", "arbitrary"), vmem_limit_bytes=VMEM_LIMIT),
        name="mix",
    )(x, attn, z, z, z, mod, pool_w, pool_scale, wba, wbp, wo, g2)


def _ffn_body(x1_ref, h2_ref, mod_ref, wup_ref, cw_ref, cb_ref, wdn32_ref, o_ref,
              hist0_ref, hist1_ref, carry_ref, wdn_ref):
    _cast_weight_once(wdn32_ref, wdn_ref)
    t = pl.program_id(1)
    sub = FFN_TILE // FFN_SPLIT
    fc = FFN_CHUNK
    n_chunks = D_FF // fc
    hist = (hist0_ref, hist1_ref)

    @pl.when(t == 0)
    def _():
        carry_ref[...] = jnp.zeros_like(carry_ref)

    units = [(part, c) for part in range(FFN_SPLIT) for c in range(n_chunks)]
    h2 = [h2_ref[0, part * sub:(part + 1) * sub, :] for part in range(FFN_SPLIT)]

    def up(k):
        part, c = units[k]
        out = []
        for half in range(2):
            cols = slice(half * D_FF + c * fc, half * D_FF + (c + 1) * fc)
            hcols = slice(half * fc, (half + 1) * fc)
            r = _dot(h2[part], wup_ref[:, cols])
            hist[k % 2][0:CONV_HALO, hcols] = carry_ref[:, cols]
            hist[k % 2][CONV_HALO:CONV_HALO + sub, hcols] = r
            carry_ref[:, cols] = r[sub - CONV_HALO:sub]
            out.append(r)
        return out

    def down(acc, k, act):
        part, c = units[k]
        acc[part] = acc[part] + _dot(act, wdn_ref[c * fc:(c + 1) * fc, :])

    acc = [jnp.zeros((sub, D_MODEL), F32) for _ in range(FFN_SPLIT)]
    r_next = up(0)
    act_prev = None
    for k, (part, c) in enumerate(units):
        r_cur = r_next
        if k + 1 < len(units):
            r_next = up(k + 1)
        if act_prev is not None:
            down(acc, k - 1, act_prev)
        halves = []
        for half in range(2):
            cols = slice(half * D_FF + c * fc, half * D_FF + (c + 1) * fc)
            hcols = slice(half * fc, (half + 1) * fc)
            u = cb_ref[:, cols] + cw_ref[CONV_W - 1:CONV_W, cols] * r_cur[half]
            for tap in range(CONV_W - 1):
                back = CONV_W - 1 - tap
                u = u + cw_ref[tap:tap + 1, cols] * hist[k % 2][CONV_HALO - back:CONV_HALO - back + sub, hcols]
            halves.append(u)
        u_g, u_v = halves
        act_prev = (u_g * _sigmoid(u_g) * u_v).astype(BF16)
    down(acc, len(units) - 1, act_prev)
    for part in range(FFN_SPLIT):
        rows = slice(part * sub, (part + 1) * sub)
        o_ref[0, rows, :] = x1_ref[0, rows, :] + mod_ref[0, 5:6, :] * acc[part]


def _ffn(x1, h2, mod, wup_bf, conv_w, conv_b, w_down):
    b, s, d = x1.shape
    tm = FFN_TILE
    const2 = lambda bi, t: (0, 0)
    return pl.pallas_call(
        _ffn_body,
        grid=(b, s // tm),
        in_specs=[
            pl.BlockSpec((1, tm, d), lambda bi, t: (bi, t, 0)),
            pl.BlockSpec((1, tm, d), lambda bi, t: (bi, t, 0)),
            pl.BlockSpec((1, 6, d), lambda bi, t: (bi, 0, 0)),
            pl.BlockSpec((d, 2 * D_FF), const2),
            pl.BlockSpec((CONV_W, 2 * D_FF), const2),
            pl.BlockSpec((1, 2 * D_FF), const2),
            pl.BlockSpec((D_FF, d), const2),
        ],
        out_specs=pl.BlockSpec((1, tm, d), lambda bi, t: (bi, t, 0)),
        out_shape=jax.ShapeDtypeStruct((b, s, d), F32),
        scratch_shapes=[
            pltpu.VMEM((tm // FFN_SPLIT + CONV_HALO, 2 * FFN_CHUNK), F32),
            pltpu.VMEM((tm // FFN_SPLIT + CONV_HALO, 2 * FFN_CHUNK), F32),
            pltpu.VMEM((CONV_HALO, 2 * D_FF), F32),
            pltpu.VMEM((D_FF, d), BF16),
        ],
        compiler_params=pltpu.CompilerParams(
            dimension_semantics=("arbitrary", "arbitrary"), vmem_limit_bytes=VMEM_LIMIT),
        name="ffn",
    )(x1, h2, mod, wup_bf, conv_w, conv_b, w_down)


def _layer(x, c_lanes, rel_bias, ada_w, ada_b, norm1_g, w_in, q_norm_g, k_norm_g, pool_w, pool_scale,
           w_branch_attn, w_branch_pool, w_out, norm2_g, w_up, conv_w, conv_b, w_down):
    b, s, d = x.shape
    mod = _ada(c_lanes, ada_w, ada_b[None, :]).reshape(b, 6, d)
    z = _inproj(x, mod, norm1_g[None, :], w_in)
    gq_t = jnp.broadcast_to(jnp.tile(q_norm_g, HEADS_PER_STEP)[:, None], (PAIR_W, MOBA_BLOCK))
    gk_row = jnp.tile(k_norm_g, HEADS_PER_STEP)[None, :]
    attn = _attention(z, gq_t, gk_row, _biasprep(rel_bias))
    x1, h2 = _mix(x, attn, z, mod, pool_w, pool_scale[None, :], w_branch_attn, w_branch_pool, w_out,
                  norm2_g[None, :])
    return _ffn(x1, h2, mod, w_up.astype(BF16), conv_w, conv_b[None, :], w_down)


def kernel(x, c, ada_w, ada_b, norm1_g, w_in, q_norm_g, k_norm_g, rel_bias, pool_w, pool_scale,
           w_branch_attn, w_branch_pool, w_out, norm2_g, w_up, conv_w, conv_b, w_down):
    b, s, d = x.shape
    assert d == D_MODEL and s % MOBA_BLOCK == 0 and w_in.shape[-1] == IN_W
    c_lanes = jnp.broadcast_to(c[:, :, None], (b, d, 128))
    for l in range(ada_w.shape[0]):
        x = _layer(x, c_lanes, rel_bias, ada_w[l], ada_b[l], norm1_g[l], w_in[l], q_norm_g[l],
                   k_norm_g[l], pool_w[l], pool_scale[l], w_branch_attn[l], w_branch_pool[l],
                   w_out[l], norm2_g[l], w_up[l], conv_w[l], conv_b[l], w_down[l])
    return x
```

```python
import functools
import math

import numpy as np
import jax
import jax.numpy as jnp
from jax import lax
from jax.experimental import pallas as pl
from jax.experimental.pallas import tpu as pltpu

F32 = jnp.float32
BF16 = jnp.bfloat16

D_MODEL = 1024
ATTN_HEADS = 8
HEAD_DIM = 64
ATTN_W = ATTN_HEADS * HEAD_DIM
MOBA_BLOCK = 256
MOBA_TOPK = 3
POOL_GROUPS = 4
POOL_GROUP_W = 128
POOL_W = POOL_GROUPS * POOL_GROUP_W
POOL_WINDOWS = (2, 4, 8, 16)
NUM_BUCKETS = 32
MAX_DISTANCE = 128
D_FF = 2816
CONV_W = 3
EPS = 1e-6
IN_W = 3 * ATTN_W + POOL_W + 2 * D_MODEL

HEADS_PER_STEP = 2
PAIR_W = HEADS_PER_STEP * HEAD_DIM
N_PAIRS = ATTN_HEADS // HEADS_PER_STEP
V_ROWS = HEAD_DIM + 16
LOG2E = math.log2(math.e)
MASK_BIG = 1e30
INPROJ_TILE = 512
INPROJ_SPLIT = 2
POOL_HALO = 16
MIX_TILE = 512
MIX_SPLIT = 2
FFN_CHUNK = 256
FFN_TILE = 512
FFN_SPLIT = 2
CONV_HALO = 8
VMEM_LIMIT = 56 * 1024 * 1024


def _split_bf16(a):
    hi = a.astype(BF16)
    lo = (a - hi.astype(F32)).astype(BF16)
    return hi, lo


def _dot(a, b):
    return jnp.dot(a, b, preferred_element_type=F32)


def _dot3(a, b):
    ah, al = _split_bf16(a)
    bh, bl = _split_bf16(b)
    return _dot(ah, bh) + _dot(al, bh) + _dot(ah, bl)


def _sigmoid(v):
    return 1.0 / (1.0 + jnp.exp2(v * (-LOG2E)))


def _cast_weight_once(src_ref, dst_ref, col_chunk=512):
    first = functools.reduce(jnp.logical_and, [pl.program_id(a) == 0 for a in range(2)])

    @pl.when(first)
    def _():
        n = src_ref.shape[-1]
        for c0 in range(0, n, col_chunk):
            dst_ref[..., c0:c0 + col_chunk] = src_ref[..., c0:c0 + col_chunk].astype(BF16)


def _ada_body(c_ref, w_ref, b_ref, o_ref):
    n_batch, _, lanes = c_ref.shape
    _, n_mod, dm = o_ref.shape

    @pl.when(pl.program_id(0) == 0)
    def _():
        for j in range(n_mod):
            o_ref[:, j, :] = jnp.broadcast_to(b_ref[:, j * dm:(j + 1) * dm], (n_batch, dm))

    for bi in range(n_batch):
        c = c_ref[bi]
        act = c * _sigmoid(c)
        for j in range(n_mod):
            cols = [jnp.sum(act * w_ref[:, j * dm + g * lanes:j * dm + (g + 1) * lanes], axis=0, keepdims=True)
                    for g in range(dm // lanes)]
            o_ref[bi, j:j + 1, :] += jnp.concatenate(cols, axis=-1)


def _ada(c_lanes, ada_w, ada_b):
    n_batch, d, lanes = c_lanes.shape
    n = ada_w.shape[1]
    n_mod = n // d
    tk = 256
    return pl.pallas_call(
        _ada_body,
        grid=(d // tk,),
        in_specs=[
            pl.BlockSpec((n_batch, tk, lanes), lambda k: (0, k, 0)),
            pl.BlockSpec((tk, n), lambda k: (k, 0)),
            pl.BlockSpec((1, n), lambda k: (0, 0)),
        ],
        out_specs=pl.BlockSpec((n_batch, n_mod, d), lambda k: (0, 0, 0)),
        out_shape=jax.ShapeDtypeStruct((n_batch, n_mod, d), F32),
        compiler_params=pltpu.CompilerParams(
            dimension_semantics=("arbitrary",), vmem_limit_bytes=VMEM_LIMIT),
        name="ada",
    )(c_lanes, ada_w, ada_b)


def _norm_mod(x, gain, scale, shift):
    ms = jnp.mean(x * x, axis=-1, keepdims=True)
    return x * lax.rsqrt(ms + EPS) * (gain * (1.0 + scale)) + shift


def _inproj_body(x_ref, mod_ref, g_ref, w32_ref, z_ref, w_ref):
    _cast_weight_once(w32_ref, w_ref)
    sub = INPROJ_TILE // INPROJ_SPLIT
    n_chunk = 1024
    for part in range(INPROJ_SPLIT):
        rows = slice(part * sub, (part + 1) * sub)
        h = _norm_mod(x_ref[0, rows, :], g_ref[...], mod_ref[0, 1:2, :], mod_ref[0, 0:1, :]).astype(BF16)
        for n in range(IN_W // n_chunk):
            cols = slice(n * n_chunk, (n + 1) * n_chunk)
            z_ref[0, rows, cols] = _dot(h, w_ref[:, cols]).astype(BF16)


def _inproj(x, mod, g1, w_in):
    b, s, d = x.shape
    return pl.pallas_call(
        _inproj_body,
        grid=(b, s // INPROJ_TILE),
        in_specs=[
            pl.BlockSpec((1, INPROJ_TILE, d), lambda bi, t: (bi, t, 0)),
            pl.BlockSpec((1, 6, d), lambda bi, t: (bi, 0, 0)),
            pl.BlockSpec((1, d), lambda bi, t: (0, 0)),
            pl.BlockSpec((d, IN_W), lambda bi, t: (0, 0)),
        ],
        out_specs=pl.BlockSpec((1, INPROJ_TILE, IN_W), lambda bi, t: (bi, t, 0)),
        out_shape=jax.ShapeDtypeStruct((b, s, IN_W), BF16),
        scratch_shapes=[pltpu.VMEM((d, IN_W), BF16)],
        compiler_params=pltpu.CompilerParams(
            dimension_semantics=("arbitrary", "arbitrary"), vmem_limit_bytes=VMEM_LIMIT),
        name="inproj",
    )(x, mod, g1, w_in)


def _t5_bucket_np(dist):
    max_exact = NUM_BUCKETS // 2
    n = np.maximum(dist, 0)
    nf = np.maximum(n, 1).astype(np.float64)
    large = max_exact + (np.log(nf / max_exact) / math.log(MAX_DISTANCE / max_exact)
                         * (NUM_BUCKETS - max_exact)).astype(np.int32)
    large = np.minimum(large, NUM_BUCKETS - 1)
    return np.where(n < max_exact, n, large).astype(np.int32)


BIAS_ROW_W = 4 * MOBA_BLOCK


def _bucket_row():
    d = np.arange(BIAS_ROW_W) - MOBA_BLOCK
    return np.where(d >= 0, _t5_bucket_np(d), -1).astype(np.int32)[None, :]


def _biasprep_body(rb_ref, bkt_ref, o_ref):
    h = pl.program_id(0)
    far = rb_ref[NUM_BUCKETS - 1, h]
    bkt = bkt_ref[...]
    row = jnp.where(bkt < 0, -MASK_BIG, 0.0).astype(F32)
    for bk in range(NUM_BUCKETS):
        row = jnp.where(bkt == bk, (rb_ref[bk, h] - far) * LOG2E, row)
    rows = jnp.broadcast_to(row, (MOBA_BLOCK, BIAS_ROW_W))
    rolled = pltpu.roll(rows, 0, 1, stride=1, stride_axis=0)
    o_ref[0, 0] = rolled[:, MOBA_BLOCK:2 * MOBA_BLOCK]
    o_ref[0, 1] = rolled[:, 2 * MOBA_BLOCK:3 * MOBA_BLOCK]


def _biasprep(rel_bias):
    bkt = jnp.asarray(_bucket_row())
    return pl.pallas_call(
        _biasprep_body,
        grid=(ATTN_HEADS,),
        in_specs=[
            pl.BlockSpec(memory_space=pltpu.SMEM),
            pl.BlockSpec((1, BIAS_ROW_W), lambda h: (0, 0)),
        ],
        out_specs=pl.BlockSpec((1, 2, MOBA_BLOCK, MOBA_BLOCK), lambda h: (h, 0, 0, 0)),
        out_shape=jax.ShapeDtypeStruct((ATTN_HEADS, 2, MOBA_BLOCK, MOBA_BLOCK), F32),
        compiler_params=pltpu.CompilerParams(dimension_semantics=("arbitrary",)),
        name="biasprep",
    )(rel_bias, bkt)


PREP_UNROLL = 4
ITEM_FIELDS = 5
PIPE_UNROLL = 16


def _far_items(n_blocks):
    never = n_blocks - 1
    items = []
    for i in range(n_blocks):
        n_far = max(i - 1, 0)
        for j0 in range(0, n_far, 2):
            items.append((i, j0, j0, j0 + 1, j0 + 1) if j0 + 1 < n_far else (i, j0, j0, j0, never))
    dummy = (n_blocks - 1, 0, never, 0, never)
    items.extend([dummy] * (-len(items) % PIPE_UNROLL))
    items.append(dummy)
    return np.asarray(items, np.int32)


def _attn_body(tbl_ref, zq_ref, zk_ref, zv_ref, gq_ref, gk_ref, bias_ref, wup32_ref, wdn32_ref,
               o_ref, wup_ref, wdn_ref,
               kn_ref, vta_ref, w_ref, pen_ref, acc_ref, m_ref, s0_ref, s1_ref, *, n_blocks, n_items):
    blk = MOBA_BLOCK
    heads = range(HEADS_PER_STEP)

    wup_ref[...] = wup32_ref[...].astype(BF16)
    wdn_ref[...] = wdn32_ref[...].astype(BF16)

    def rows_of(j):
        return pl.ds(pl.multiple_of(j * blk, blk), blk)

    blk_rows = lax.broadcasted_iota(jnp.int32, (n_blocks, PAIR_W), 0)
    ones = jnp.ones((V_ROWS - HEAD_DIM, blk), BF16)
    same_head = (lax.broadcasted_iota(jnp.int32, (PAIR_W, PAIR_W), 0) // HEAD_DIM
                 == lax.broadcasted_iota(jnp.int32, (PAIR_W, PAIR_W), 1) // HEAD_DIM)
    head_sum = jnp.where(same_head, 1.0, 0.0).astype(BF16)

    def kv_prep(i, kmean):
        kraw = zk_ref[0, rows_of(i), :].astype(F32)
        sq_hi, sq_lo = _split_bf16(kraw * kraw)
        ss = _dot(sq_hi, head_sum) + _dot(sq_lo, head_sum)
        rinv = lax.rsqrt(ss * (1.0 / HEAD_DIM) + EPS)
        kn = kraw * rinv * gk_ref[...]
        kn_ref[rows_of(i), :] = kn.astype(BF16)
        vt = zv_ref[0, rows_of(i), :].astype(F32).T
        for hd in heads:
            vta_ref[i, hd, 0:HEAD_DIM, :] = vt[hd * HEAD_DIM:(hd + 1) * HEAD_DIM].astype(BF16)
            vta_ref[i, hd, HEAD_DIM:V_ROWS, :] = ones
        return jnp.where(blk_rows == i, jnp.mean(kn, axis=0, keepdims=True), kmean)

    kmean = lax.fori_loop(0, n_blocks, kv_prep, jnp.zeros((n_blocks, PAIR_W), F32), unroll=PREP_UNROLL)

    blk_ids = lax.broadcasted_iota(jnp.int32, (n_blocks, blk), 0)
    zeros_half = jnp.zeros((HEAD_DIM, blk), F32)

    def q_prep(i, _):
        qt = zq_ref[0, rows_of(i), :].astype(F32).T
        qsq = qt * qt
        for hd in heads:
            rows = slice(hd * HEAD_DIM, (hd + 1) * HEAD_DIM)
            ss = jnp.sum(qsq[rows], axis=0, keepdims=True)
            r = lax.rsqrt(ss * (1.0 / HEAD_DIM) + EPS)
            qn = qt[rows] * r * gq_ref[rows, :] * (HEAD_DIM ** -0.5 * LOG2E)
            parts = [zeros_half] * HEADS_PER_STEP
            parts[hd] = qn
            w = jnp.concatenate(parts, axis=0)
            w_ref[i, hd] = w.astype(BF16)
            gate = jnp.where(blk_ids < i, _dot3(kmean, w), -jnp.inf)
            sel = jnp.zeros((n_blocks, blk), jnp.bool_)
            for _ in range(MOBA_TOPK):
                mx = jnp.max(gate, axis=0, keepdims=True)
                cand = jnp.where(gate == mx, blk_ids, n_blocks)
                pick = (blk_ids == jnp.min(cand, axis=0, keepdims=True)) & (mx > -jnp.inf)
                sel = sel | pick
                gate = jnp.where(pick, -jnp.inf, gate)
            pen_ref[hd, i] = jnp.where(sel, 0.0, -MASK_BIG).astype(F32)
        return 0

    lax.fori_loop(0, n_blocks, q_prep, 0, unroll=PREP_UNROLL)

    def score(i, blocks, s_ref, biased):
        keys = [kn_ref[rows_of(j), :] for j, _ in blocks]
        gmax = []
        for hd in heads:
            w = w_ref[i, hd]
            tmax = []
            for u, (_, jpen) in enumerate(blocks):
                s = _dot(keys[u], w)
                if biased:
                    s = s + bias_ref[hd, u]
                s_ref[hd, u] = s
                t = jnp.max(s, axis=0, keepdims=True)
                tmax.append(t if jpen is None else t + pen_ref[hd, i, pl.ds(jpen, 1), :])
            gmax.append(functools.reduce(jnp.maximum, tmax))
        return tuple(gmax)

    def attend(i, blocks, s_ref, gmax, first_visit):
        for hd in heads:
            if first_visit:
                m_new = gmax[hd]
                acc = None
            else:
                m_old = m_ref[i, hd]
                m_new = jnp.maximum(m_old, gmax[hd])
                acc = acc_ref[i, hd] * jnp.exp2(m_old - m_new)
            for u, (j, jpen) in enumerate(blocks):
                shift = m_new if jpen is None else m_new - pen_ref[hd, i, pl.ds(jpen, 1), :]
                p = jnp.exp2(s_ref[hd, u] - shift).astype(BF16)
                pv = _dot(vta_ref[j, hd], p)
                acc = pv if acc is None else acc + pv
            acc_ref[i, hd] = acc
            m_ref[i, hd] = m_new

    def near_blocks(i):
        jp = jnp.maximum(i - 1, 0)
        return ((i, None), (jp, jp))

    slots = (s0_ref, s1_ref)

    def near_body(t, g):
        for u in range(PIPE_UNROLL):
            i_cur = PIPE_UNROLL * t + u
            i_nxt = jnp.minimum(i_cur + 1, n_blocks - 1)
            g_nxt = score(i_nxt, near_blocks(i_nxt), slots[(u + 1) % 2], True)
            attend(i_cur, near_blocks(i_cur), slots[u % 2], g, True)
            g = g_nxt
        return g

    lax.fori_loop(0, n_blocks // PIPE_UNROLL, near_body, score(0, near_blocks(0), s0_ref, True))

    def item(k):
        base = k * ITEM_FIELDS
        i = tbl_ref[base]
        return i, ((tbl_ref[base + 1], tbl_ref[base + 2]), (tbl_ref[base + 3], tbl_ref[base + 4]))

    def far_body(t, g):
        for u in range(PIPE_UNROLL):
            k = PIPE_UNROLL * t + u
            g_nxt = score(*item(k + 1), slots[(u + 1) % 2], False)
            attend(*item(k), slots[u % 2], g, False)
            g = g_nxt
        return g

    lax.fori_loop(0, n_items // PIPE_UNROLL, far_body, score(*item(0), s0_ref, False))

    def finish(i, _):
        outs = []
        for hd in heads:
            acc = acc_ref[i, hd]
            outs.append(acc[0:HEAD_DIM] / acc[HEAD_DIM:HEAD_DIM + 1])
        o_ref[0, rows_of(i), :] = jnp.concatenate(outs, axis=0).T.astype(BF16)
        return 0

    lax.fori_loop(0, n_blocks, finish, 0, unroll=PREP_UNROLL)


def _attention(z, gq_t, gk_row, bias, w_up, w_down):
    b, s, _ = z.shape
    n_blocks = s // MOBA_BLOCK
    assert n_blocks % PIPE_UNROLL == 0 and PIPE_UNROLL % 2 == 0
    blk = MOBA_BLOCK
    items = _far_items(n_blocks)
    n_items = items.shape[0] - 1
    n_steps = b * N_PAIRS
    up_rows, dn_rows = w_up.shape[0] // n_steps, w_down.shape[0] // n_steps
    assert up_rows * n_steps == w_up.shape[0] and dn_rows * n_steps == w_down.shape[0]
    assert up_rows % 16 == 0 and dn_rows % 16 == 0

    def slab(bi, hp, tbl):
        return (bi * N_PAIRS + hp, 0)

    grid_spec = pltpu.PrefetchScalarGridSpec(
        num_scalar_prefetch=1,
        grid=(b, N_PAIRS),
        in_specs=[
            pl.BlockSpec((1, s, PAIR_W), lambda bi, hp, tbl: (bi, 0, hp)),
            pl.BlockSpec((1, s, PAIR_W), lambda bi, hp, tbl: (bi, 0, N_PAIRS + hp)),
            pl.BlockSpec((1, s, PAIR_W), lambda bi, hp, tbl: (bi, 0, 2 * N_PAIRS + hp)),
            pl.BlockSpec((PAIR_W, blk), lambda bi, hp, tbl: (0, 0)),
            pl.BlockSpec((1, PAIR_W), lambda bi, hp, tbl: (0, 0)),
            pl.BlockSpec((HEADS_PER_STEP, 2, blk, blk), lambda bi, hp, tbl: (hp, 0, 0, 0)),
            pl.BlockSpec((up_rows, w_up.shape[1]), slab),
            pl.BlockSpec((dn_rows, w_down.shape[1]), slab),
        ],
        out_specs=[
            pl.BlockSpec((1, s, PAIR_W), lambda bi, hp, tbl: (bi, 0, hp)),
            pl.BlockSpec((up_rows, w_up.shape[1]), slab),
            pl.BlockSpec((dn_rows, w_down.shape[1]), slab),
        ],
        scratch_shapes=[
            pltpu.VMEM((s, PAIR_W), BF16),
            pltpu.VMEM((n_blocks, HEADS_PER_STEP, V_ROWS, blk), BF16),
            pltpu.VMEM((n_blocks, HEADS_PER_STEP, PAIR_W, blk), BF16),
            pltpu.VMEM((HEADS_PER_STEP, n_blocks, n_blocks, blk), F32),
            pltpu.VMEM((n_blocks, HEADS_PER_STEP, V_ROWS, blk), F32),
            pltpu.VMEM((n_blocks, HEADS_PER_STEP, 1, blk), F32),
            pltpu.VMEM((HEADS_PER_STEP, 2, blk, blk), F32),
            pltpu.VMEM((HEADS_PER_STEP, 2, blk, blk), F32),
        ],
    )
    return pl.pallas_call(
        functools.partial(_attn_body, n_blocks=n_blocks, n_items=n_items),
        grid_spec=grid_spec,
        out_shape=[
            jax.ShapeDtypeStruct((b, s, ATTN_W), BF16),
            jax.ShapeDtypeStruct(w_up.shape, BF16),
            jax.ShapeDtypeStruct(w_down.shape, BF16),
        ],
        compiler_params=pltpu.CompilerParams(
            dimension_semantics=("arbitrary", "arbitrary"), vmem_limit_bytes=VMEM_LIMIT),
        name="attn",
    )(jnp.asarray(items.reshape(-1)), z, z, z, gq_t, gk_row, bias, w_up, w_down)


def _mix_body(x_ref, attn_ref, p_ref, ga_ref, gp_ref, mod_ref, pw32_ref, ps_ref, wba32_ref, wbp32_ref,
              wo32_ref, g2_ref, x1_ref, h2_ref, pe_ref, pw_ref, wba_ref, wbp_ref, wo_ref):
    for w32, wbf in ((pw32_ref, pw_ref), (wba32_ref, wba_ref), (wbp32_ref, wbp_ref), (wo32_ref, wo_ref)):
        _cast_weight_once(w32, wbf)
    t = pl.program_id(1)
    tm = MIX_TILE
    sub = MIX_TILE // MIX_SPLIT

    @pl.when(t == 0)
    def _():
        pe_ref[0:POOL_HALO, :] = jnp.zeros((POOL_HALO, POOL_W), F32)

    @pl.when(t > 0)
    def _():
        pe_ref[0:POOL_HALO, :] = pe_ref[tm:tm + POOL_HALO, :]

    pe_ref[POOL_HALO:POOL_HALO + tm, :] = p_ref[0].astype(F32)

    for part in range(MIX_SPLIT):
        r0 = part * sub
        rows = slice(r0, r0 + sub)
        pos = t * tm + r0 + lax.broadcasted_iota(jnp.int32, (sub, POOL_GROUP_W), 0)
        pooled = []
        for g, win in enumerate(POOL_WINDOWS):
            cols = slice(g * POOL_GROUP_W, (g + 1) * POOL_GROUP_W)
            ext = pe_ref[r0:r0 + POOL_HALO + sub, cols]
            tot = ext
            span = 1
            while span < win:
                tot = tot + pltpu.roll(tot, span, 0)
                span *= 2
            count = jnp.minimum(pos + 1, win).astype(F32)
            pooled_g = (tot[POOL_HALO:] / count - ext[POOL_HALO:]).astype(BF16)
            pooled.append(_dot(pooled_g, pw_ref[g]) * ps_ref[:, cols])
        pool = jnp.concatenate(pooled, axis=-1).astype(BF16)

        a_proj = _dot(attn_ref[0, rows, :], wba_ref[...])
        p_proj = _dot(pool, wbp_ref[...])
        merged = (_sigmoid(ga_ref[0, rows, :].astype(F32)) * a_proj
                  + _sigmoid(gp_ref[0, rows, :].astype(F32)) * p_proj).astype(BF16)
        x1 = x_ref[0, rows, :] + mod_ref[0, 2:3, :] * _dot(merged, wo_ref[...])
        x1_ref[0, rows, :] = x1
        h2_ref[0, rows, :] = _norm_mod(x1, g2_ref[...], mod_ref[0, 4:5, :], mod_ref[0, 3:4, :]).astype(BF16)


def _mix(x, attn, z, mod, pool_w, pool_scale, wba, wbp, wo, g2):
    b, s, d = x.shape
    tm = MIX_TILE
    const2 = lambda bi, t: (0, 0)
    return pl.pallas_call(
        _mix_body,
        grid=(b, s // tm),
        in_specs=[
            pl.BlockSpec((1, tm, d), lambda bi, t: (bi, t, 0)),
            pl.BlockSpec((1, tm, ATTN_W), lambda bi, t: (bi, t, 0)),
            pl.BlockSpec((1, tm, POOL_W), lambda bi, t: (bi, t, 3 * ATTN_W // POOL_W)),
            pl.BlockSpec((1, tm, d), lambda bi, t: (bi, t, (3 * ATTN_W + POOL_W) // d)),
            pl.BlockSpec((1, tm, d), lambda bi, t: (bi, t, (3 * ATTN_W + POOL_W) // d + 1)),
            pl.BlockSpec((1, 6, d), lambda bi, t: (bi, 0, 0)),
            pl.BlockSpec((POOL_GROUPS, POOL_GROUP_W, POOL_GROUP_W), lambda bi, t: (0, 0, 0)),
            pl.BlockSpec((1, POOL_W), const2),
            pl.BlockSpec((ATTN_W, d), const2),
            pl.BlockSpec((POOL_W, d), const2),
            pl.BlockSpec((d, d), const2),
            pl.BlockSpec((1, d), const2),
        ],
        out_specs=[
            pl.BlockSpec((1, tm, d), lambda bi, t: (bi, t, 0)),
            pl.BlockSpec((1, tm, d), lambda bi, t: (bi, t, 0)),
        ],
        out_shape=[
            jax.ShapeDtypeStruct((b, s, d), F32),
            jax.ShapeDtypeStruct((b, s, d), BF16),
        ],
        scratch_shapes=[
            pltpu.VMEM((tm + POOL_HALO, POOL_W), F32),
            pltpu.VMEM(pool_w.shape, BF16),
            pltpu.VMEM(wba.shape, BF16),
            pltpu.VMEM(wbp.shape, BF16),
            pltpu.VMEM(wo.shape, BF16),
        ],
        compiler_params=pltpu.CompilerParams(
            dimension_semantics=("arbitrary", "arbitrary"), vmem_limit_bytes=VMEM_LIMIT),
        name="mix",
    )(x, attn, z, z, z, mod, pool_w, pool_scale, wba, wbp, wo, g2)


def _ffn_body(x1_ref, h2_ref, mod_ref, wup_ref, cw_ref, cb_ref, wdn_ref, o_ref,
              hist0_ref, hist1_ref, carry_ref):
    t = pl.program_id(1)
    sub = FFN_TILE // FFN_SPLIT
    fc = FFN_CHUNK
    n_chunks = D_FF // fc
    hist = (hist0_ref, hist1_ref)

    @pl.when(t == 0)
    def _():
        carry_ref[...] = jnp.zeros_like(carry_ref)

    units = [(part, c) for part in range(FFN_SPLIT) for c in range(n_chunks)]
    h2 = [h2_ref[0, part * sub:(part + 1) * sub, :] for part in range(FFN_SPLIT)]

    def up(k):
        part, c = units[k]
        out = []
        for half in range(2):
            cols = slice(half * D_FF + c * fc, half * D_FF + (c + 1) * fc)
            hcols = slice(half * fc, (half + 1) * fc)
            r = _dot(h2[part], wup_ref[:, cols])
            hist[k % 2][0:CONV_HALO, hcols] = carry_ref[:, cols]
            hist[k % 2][CONV_HALO:CONV_HALO + sub, hcols] = r
            carry_ref[:, cols] = r[sub - CONV_HALO:sub]
            out.append(r)
        return out

    def down(acc, k, act):
        part, c = units[k]
        acc[part] = acc[part] + _dot(act, wdn_ref[c * fc:(c + 1) * fc, :])

    acc = [jnp.zeros((sub, D_MODEL), F32) for _ in range(FFN_SPLIT)]
    r_next = up(0)
    act_prev = None
    for k, (part, c) in enumerate(units):
        r_cur = r_next
        if k + 1 < len(units):
            r_next = up(k + 1)
        if act_prev is not None:
            down(acc, k - 1, act_prev)
        halves = []
        for half in range(2):
            cols = slice(half * D_FF + c * fc, half * D_FF + (c + 1) * fc)
            hcols = slice(half * fc, (half + 1) * fc)
            u = cb_ref[:, cols] + cw_ref[CONV_W - 1:CONV_W, cols] * r_cur[half]
            for tap in range(CONV_W - 1):
                back = CONV_W - 1 - tap
                u = u + cw_ref[tap:tap + 1, cols] * hist[k % 2][CONV_HALO - back:CONV_HALO - back + sub, hcols]
            halves.append(u)
        u_g, u_v = halves
        act_prev = (u_g * _sigmoid(u_g) * u_v).astype(BF16)
    down(acc, len(units) - 1, act_prev)
    for part in range(FFN_SPLIT):
        rows = slice(part * sub, (part + 1) * sub)
        o_ref[0, rows, :] = x1_ref[0, rows, :] + mod_ref[0, 5:6, :] * acc[part]


def _ffn(x1, h2, mod, wup_bf, conv_w, conv_b, wdn_bf):
    b, s, d = x1.shape
    tm = FFN_TILE
    const2 = lambda bi, t: (0, 0)
    return pl.pallas_call(
        _ffn_body,
        grid=(b, s // tm),
        in_specs=[
            pl.BlockSpec((1, tm, d), lambda bi, t: (bi, t, 0)),
            pl.BlockSpec((1, tm, d), lambda bi, t: (bi, t, 0)),
            pl.BlockSpec((1, 6, d), lambda bi, t: (bi, 0, 0)),
            pl.BlockSpec((d, 2 * D_FF), const2),
            pl.BlockSpec((CONV_W, 2 * D_FF), const2),
            pl.BlockSpec((1, 2 * D_FF), const2),
            pl.BlockSpec((D_FF, d), const2),
        ],
        out_specs=pl.BlockSpec((1, tm, d), lambda bi, t: (bi, t, 0)),
        out_shape=jax.ShapeDtypeStruct((b, s, d), F32),
        scratch_shapes=[
            pltpu.VMEM((tm // FFN_SPLIT + CONV_HALO, 2 * FFN_CHUNK), F32),
            pltpu.VMEM((tm // FFN_SPLIT + CONV_HALO, 2 * FFN_CHUNK), F32),
            pltpu.VMEM((CONV_HALO, 2 * D_FF), F32),
        ],
        compiler_params=pltpu.CompilerParams(
            dimension_semantics=("arbitrary", "arbitrary"), vmem_limit_bytes=VMEM_LIMIT),
        name="ffn",
    )(x1, h2, mod, wup_bf, conv_w, conv_b, wdn_bf)


def _layer(x, c_lanes, rel_bias, ada_w, ada_b, norm1_g, w_in, q_norm_g, k_norm_g, pool_w, pool_scale,
           w_branch_attn, w_branch_pool, w_out, norm2_g, w_up, conv_w, conv_b, w_down):
    b, s, d = x.shape
    mod = _ada(c_lanes, ada_w, ada_b[None, :])
    z = _inproj(x, mod, norm1_g[None, :], w_in)
    gq_t = jnp.broadcast_to(jnp.tile(q_norm_g, HEADS_PER_STEP)[:, None], (PAIR_W, MOBA_BLOCK))
    gk_row = jnp.tile(k_norm_g, HEADS_PER_STEP)[None, :]
    attn, wup_bf, wdn_bf = _attention(z, gq_t, gk_row, _biasprep(rel_bias), w_up, w_down)
    x1, h2 = _mix(x, attn, z, mod, pool_w, pool_scale[None, :], w_branch_attn, w_branch_pool, w_out,
                  norm2_g[None, :])
    return _ffn(x1, h2, mod, wup_bf, conv_w, conv_b[None, :], wdn_bf)


def kernel(x, c, ada_w, ada_b, norm1_g, w_in, q_norm_g, k_norm_g, rel_bias, pool_w, pool_scale,
           w_branch_attn, w_branch_pool, w_out, norm2_g, w_up, conv_w, conv_b, w_down):
    b, s, d = x.shape
    assert d == D_MODEL and s % MOBA_BLOCK == 0 and w_in.shape[-1] == IN_W
    c_lanes = jnp.broadcast_to(c[:, :, None], (b, d, 128))
    for l in range(ada_w.shape[0]):
        x = _layer(x, c_lanes, rel_bias, ada_w[l], ada_b[l], norm1_g[l], w_in[l], q_norm_g[l],
                   k_norm_g[l], pool_w[l], pool_scale[l], w_branch_attn[l], w_branch_pool[l],
                   w_out[l], norm2_g[l], w_up[l], conv_w[l], conv_b[l], w_down[l])
    return x
```

```python
import functools
import math

import numpy as np
import jax
import jax.numpy as jnp
from jax import lax
from jax.experimental import pallas as pl
from jax.experimental.pallas import tpu as pltpu

F32 = jnp.float32
BF16 = jnp.bfloat16

D_MODEL = 1024
ATTN_HEADS = 8
HEAD_DIM = 64
ATTN_W = ATTN_HEADS * HEAD_DIM
MOBA_BLOCK = 256
MOBA_TOPK = 3
POOL_GROUPS = 4
POOL_GROUP_W = 128
POOL_W = POOL_GROUPS * POOL_GROUP_W
POOL_WINDOWS = (2, 4, 8, 16)
NUM_BUCKETS = 32
MAX_DISTANCE = 128
D_FF = 2816
CONV_W = 3
EPS = 1e-6
IN_W = 3 * ATTN_W + POOL_W + 2 * D_MODEL

LANES = 128
BF16_SUBLANES = 16
HEADS_PER_STEP = LANES // HEAD_DIM
PAIR_W = HEADS_PER_STEP * HEAD_DIM
N_PAIRS = ATTN_HEADS // HEADS_PER_STEP
V_ROWS = HEAD_DIM + BF16_SUBLANES
LOG2E = math.log2(math.e)
MASK_BIG = 1e30
ADA_ROWS = 256
INPROJ_TILE = 1024
INPROJ_COLS = 1024
INPROJ_SPLIT = 4
POOL_HALO = 16
MIX_TILE = 1024
MIX_SPLIT = 4
FFN_CHUNK = 256
FFN_TILE = 512
FFN_SPLIT = 2
CONV_HALO = 8
VMEM_LIMIT = 56 * 1024 * 1024


def _split_bf16(a):
    hi = a.astype(BF16)
    lo = (a - hi.astype(F32)).astype(BF16)
    return hi, lo


def _dot(a, b):
    return jnp.dot(a, b, preferred_element_type=F32)


def _dot3(a, b):
    ah, al = _split_bf16(a)
    bh, bl = _split_bf16(b)
    return _dot(ah, bh) + _dot(al, bh) + _dot(ah, bl)


def _sigmoid(v):
    return 1.0 / (1.0 + jnp.exp2(v * (-LOG2E)))


def _cast_weight_once(src_ref, dst_ref, col_chunk=4 * LANES):
    first = functools.reduce(jnp.logical_and, [pl.program_id(a) == 0 for a in range(2)])

    @pl.when(first)
    def _():
        n = src_ref.shape[-1]
        for c0 in range(0, n, col_chunk):
            dst_ref[..., c0:c0 + col_chunk] = src_ref[..., c0:c0 + col_chunk].astype(BF16)


def _ada_body(c_ref, w_ref, b_ref, o_ref):
    n_batch, _, lanes = c_ref.shape
    _, n_mod, dm = o_ref.shape

    @pl.when(pl.program_id(0) == 0)
    def _():
        for j in range(n_mod):
            o_ref[:, j, :] = jnp.broadcast_to(b_ref[:, j * dm:(j + 1) * dm], (n_batch, dm))

    for bi in range(n_batch):
        c = c_ref[bi]
        act = c * _sigmoid(c)
        for j in range(n_mod):
            cols = [jnp.sum(act * w_ref[:, j * dm + g * lanes:j * dm + (g + 1) * lanes], axis=0, keepdims=True)
                    for g in range(dm // lanes)]
            o_ref[bi, j:j + 1, :] += jnp.concatenate(cols, axis=-1)


def _ada(c_lanes, ada_w, ada_b):
    n_batch, d, lanes = c_lanes.shape
    n = ada_w.shape[1]
    n_mod = n // d
    tk = ADA_ROWS
    return pl.pallas_call(
        _ada_body,
        grid=(d // tk,),
        in_specs=[
            pl.BlockSpec((n_batch, tk, lanes), lambda k: (0, k, 0)),
            pl.BlockSpec((tk, n), lambda k: (k, 0)),
            pl.BlockSpec((1, n), lambda k: (0, 0)),
        ],
        out_specs=pl.BlockSpec((n_batch, n_mod, d), lambda k: (0, 0, 0)),
        out_shape=jax.ShapeDtypeStruct((n_batch, n_mod, d), F32),
        compiler_params=pltpu.CompilerParams(
            dimension_semantics=("arbitrary",), vmem_limit_bytes=VMEM_LIMIT),
        name="ada",
    )(c_lanes, ada_w, ada_b)


def _norm_mod(x, gain, scale, shift):
    ms = jnp.mean(x * x, axis=-1, keepdims=True)
    return x * lax.rsqrt(ms + EPS) * (gain * (1.0 + scale)) + shift


def _inproj_body(x_ref, mod_ref, g_ref, w32_ref, z_ref, w_ref):
    _cast_weight_once(w32_ref, w_ref)
    sub = INPROJ_TILE // INPROJ_SPLIT
    n_chunk = INPROJ_COLS
    for part in range(INPROJ_SPLIT):
        rows = slice(part * sub, (part + 1) * sub)
        h = _norm_mod(x_ref[0, rows, :], g_ref[...], mod_ref[0, 1:2, :], mod_ref[0, 0:1, :]).astype(BF16)
        for n in range(IN_W // n_chunk):
            cols = slice(n * n_chunk, (n + 1) * n_chunk)
            z_ref[0, rows, cols] = _dot(h, w_ref[:, cols]).astype(BF16)


def _inproj(x, mod, g1, w_in):
    b, s, d = x.shape
    return pl.pallas_call(
        _inproj_body,
        grid=(b, s // INPROJ_TILE),
        in_specs=[
            pl.BlockSpec((1, INPROJ_TILE, d), lambda bi, t: (bi, t, 0)),
            pl.BlockSpec((1, 6, d), lambda bi, t: (bi, 0, 0)),
            pl.BlockSpec((1, d), lambda bi, t: (0, 0)),
            pl.BlockSpec((d, IN_W), lambda bi, t: (0, 0)),
        ],
        out_specs=pl.BlockSpec((1, INPROJ_TILE, IN_W), lambda bi, t: (bi, t, 0)),
        out_shape=jax.ShapeDtypeStruct((b, s, IN_W), BF16),
        scratch_shapes=[pltpu.VMEM((d, IN_W), BF16)],
        compiler_params=pltpu.CompilerParams(
            dimension_semantics=("arbitrary", "arbitrary"), vmem_limit_bytes=VMEM_LIMIT),
        name="inproj",
    )(x, mod, g1, w_in)


def _t5_bucket_np(dist):
    max_exact = NUM_BUCKETS // 2
    n = np.maximum(dist, 0)
    nf = np.maximum(n, 1).astype(np.float64)
    large = max_exact + (np.log(nf / max_exact) / math.log(MAX_DISTANCE / max_exact)
                         * (NUM_BUCKETS - max_exact)).astype(np.int32)
    large = np.minimum(large, NUM_BUCKETS - 1)
    return np.where(n < max_exact, n, large).astype(np.int32)


BIAS_ROW_W = 4 * MOBA_BLOCK


def _bucket_row():
    d = np.arange(BIAS_ROW_W) - MOBA_BLOCK
    return np.where(d >= 0, _t5_bucket_np(d), -1).astype(np.int32)[None, :]


def _biasprep_body(rb_ref, bkt_ref, o_ref):
    h = pl.program_id(0)
    far = rb_ref[NUM_BUCKETS - 1, h]
    bkt = bkt_ref[...]
    row = jnp.where(bkt < 0, -MASK_BIG, 0.0).astype(F32)
    for bk in range(NUM_BUCKETS):
        row = jnp.where(bkt == bk, (rb_ref[bk, h] - far) * LOG2E, row)
    rows = jnp.broadcast_to(row, (MOBA_BLOCK, BIAS_ROW_W))
    rolled = pltpu.roll(rows, 0, 1, stride=1, stride_axis=0)
    o_ref[0, 0] = rolled[:, MOBA_BLOCK:2 * MOBA_BLOCK]
    o_ref[0, 1] = rolled[:, 2 * MOBA_BLOCK:3 * MOBA_BLOCK]


def _biasprep(rel_bias):
    bkt = jnp.asarray(_bucket_row())
    return pl.pallas_call(
        _biasprep_body,
        grid=(ATTN_HEADS,),
        in_specs=[
            pl.BlockSpec(memory_space=pltpu.SMEM),
            pl.BlockSpec((1, BIAS_ROW_W), lambda h: (0, 0)),
        ],
        out_specs=pl.BlockSpec((1, 2, MOBA_BLOCK, MOBA_BLOCK), lambda h: (h, 0, 0, 0)),
        out_shape=jax.ShapeDtypeStruct((ATTN_HEADS, 2, MOBA_BLOCK, MOBA_BLOCK), F32),
        compiler_params=pltpu.CompilerParams(dimension_semantics=("arbitrary",)),
        name="biasprep",
    )(rel_bias, bkt)


PREP_UNROLL = 8
ITEM_FIELDS = 5
PIPE_UNROLL = 16


def _far_items(n_blocks):
    never = n_blocks - 1
    items = []
    for i in range(n_blocks):
        n_far = max(i - 1, 0)
        for j0 in range(0, n_far, 2):
            items.append((i, j0, j0, j0 + 1, j0 + 1) if j0 + 1 < n_far else (i, j0, j0, j0, never))
    dummy = (n_blocks - 1, 0, never, 0, never)
    items.extend([dummy] * (-len(items) % PIPE_UNROLL))
    items.append(dummy)
    return np.asarray(items, np.int32)


def _attn_body(tbl_ref, zq_ref, zk_ref, zv_ref, gq_ref, gk_ref, bias_ref, wup32_ref, wdn32_ref,
               o_ref, wup_ref, wdn_ref,
               kn_ref, vta_ref, w_ref, pen_ref, acc_ref, m_ref, s0_ref, s1_ref, *, n_blocks, n_items):
    blk = MOBA_BLOCK
    heads = range(HEADS_PER_STEP)

    wup_ref[...] = wup32_ref[...].astype(BF16)
    wdn_ref[...] = wdn32_ref[...].astype(BF16)

    def rows_of(j):
        return pl.ds(pl.multiple_of(j * blk, blk), blk)

    blk_rows = lax.broadcasted_iota(jnp.int32, (n_blocks, PAIR_W), 0)
    ones = jnp.ones((V_ROWS - HEAD_DIM, blk), BF16)
    same_head = (lax.broadcasted_iota(jnp.int32, (PAIR_W, PAIR_W), 0) // HEAD_DIM
                 == lax.broadcasted_iota(jnp.int32, (PAIR_W, PAIR_W), 1) // HEAD_DIM)
    head_sum = jnp.where(same_head, 1.0, 0.0).astype(BF16)

    def kv_prep(i, kmean):
        kraw = zk_ref[0, rows_of(i), :].astype(F32)
        sq_hi, sq_lo = _split_bf16(kraw * kraw)
        ss = _dot(sq_hi, head_sum) + _dot(sq_lo, head_sum)
        rinv = lax.rsqrt(ss * (1.0 / HEAD_DIM) + EPS)
        kn = kraw * rinv * gk_ref[...]
        kn_ref[rows_of(i), :] = kn.astype(BF16)
        vt = zv_ref[0, rows_of(i), :].astype(F32).T
        for hd in heads:
            vta_ref[i, hd, 0:HEAD_DIM, :] = vt[hd * HEAD_DIM:(hd + 1) * HEAD_DIM].astype(BF16)
            vta_ref[i, hd, HEAD_DIM:V_ROWS, :] = ones
        return jnp.where(blk_rows == i, jnp.mean(kn, axis=0, keepdims=True), kmean)

    kmean = lax.fori_loop(0, n_blocks, kv_prep, jnp.zeros((n_blocks, PAIR_W), F32), unroll=PREP_UNROLL)

    blk_ids = lax.broadcasted_iota(jnp.int32, (n_blocks, blk), 0)
    zeros_half = jnp.zeros((HEAD_DIM, blk), F32)

    def q_prep(i, _):
        qt = zq_ref[0, rows_of(i), :].astype(F32).T
        qsq = qt * qt
        for hd in heads:
            rows = slice(hd * HEAD_DIM, (hd + 1) * HEAD_DIM)
            ss = jnp.sum(qsq[rows], axis=0, keepdims=True)
            r = lax.rsqrt(ss * (1.0 / HEAD_DIM) + EPS)
            qn = qt[rows] * r * gq_ref[rows, :] * (HEAD_DIM ** -0.5 * LOG2E)
            parts = [zeros_half] * HEADS_PER_STEP
            parts[hd] = qn
            w = jnp.concatenate(parts, axis=0)
            w_ref[i, hd] = w.astype(BF16)
            gate = jnp.where(blk_ids < i, _dot3(kmean, w), -jnp.inf)
            sel = jnp.zeros((n_blocks, blk), jnp.bool_)
            for _ in range(MOBA_TOPK):
                mx = jnp.max(gate, axis=0, keepdims=True)
                cand = jnp.where(gate == mx, blk_ids, n_blocks)
                pick = (blk_ids == jnp.min(cand, axis=0, keepdims=True)) & (mx > -jnp.inf)
                sel = sel | pick
                gate = jnp.where(pick, -jnp.inf, gate)
            pen_ref[hd, i] = jnp.where(sel, 0.0, -MASK_BIG).astype(F32)
        return 0

    lax.fori_loop(0, n_blocks, q_prep, 0, unroll=PREP_UNROLL)

    def score(i, blocks, s_ref, biased):
        keys = [kn_ref[rows_of(j), :] for j, _ in blocks]
        gmax = []
        for hd in heads:
            w = w_ref[i, hd]
            tmax = []
            for u, (_, jpen) in enumerate(blocks):
                s = _dot(keys[u], w)
                if biased:
                    s = s + bias_ref[hd, u]
                s_ref[hd, u] = s
                t = jnp.max(s, axis=0, keepdims=True)
                tmax.append(t if jpen is None else t + pen_ref[hd, i, pl.ds(jpen, 1), :])
            gmax.append(functools.reduce(jnp.maximum, tmax))
        return tuple(gmax)

    def attend(i, blocks, s_ref, gmax, first_visit):
        for hd in heads:
            if first_visit:
                m_new = gmax[hd]
                acc = None
            else:
                m_old = m_ref[i, hd]
                m_new = jnp.maximum(m_old, gmax[hd])
                acc = acc_ref[i, hd] * jnp.exp2(m_old - m_new)
            for u, (j, jpen) in enumerate(blocks):
                shift = m_new if jpen is None else m_new - pen_ref[hd, i, pl.ds(jpen, 1), :]
                p = jnp.exp2(s_ref[hd, u] - shift).astype(BF16)
                pv = _dot(vta_ref[j, hd], p)
                acc = pv if acc is None else acc + pv
            acc_ref[i, hd] = acc
            m_ref[i, hd] = m_new

    def near_blocks(i):
        jp = jnp.maximum(i - 1, 0)
        return ((i, None), (jp, jp))

    slots = (s0_ref, s1_ref)

    def near_body(t, g):
        for u in range(PIPE_UNROLL):
            i_cur = PIPE_UNROLL * t + u
            i_nxt = jnp.minimum(i_cur + 1, n_blocks - 1)
            g_nxt = score(i_nxt, near_blocks(i_nxt), slots[(u + 1) % 2], True)
            attend(i_cur, near_blocks(i_cur), slots[u % 2], g, True)
            g = g_nxt
        return g

    lax.fori_loop(0, n_blocks // PIPE_UNROLL, near_body, score(0, near_blocks(0), s0_ref, True))

    def item(k):
        base = k * ITEM_FIELDS
        i = tbl_ref[base]
        return i, ((tbl_ref[base + 1], tbl_ref[base + 2]), (tbl_ref[base + 3], tbl_ref[base + 4]))

    def far_body(t, g):
        for u in range(PIPE_UNROLL):
            k = PIPE_UNROLL * t + u
            g_nxt = score(*item(k + 1), slots[(u + 1) % 2], False)
            attend(*item(k), slots[u % 2], g, False)
            g = g_nxt
        return g

    lax.fori_loop(0, n_items // PIPE_UNROLL, far_body, score(*item(0), s0_ref, False))

    def finish(i, _):
        outs = []
        for hd in heads:
            acc = acc_ref[i, hd]
            outs.append(acc[0:HEAD_DIM] / acc[HEAD_DIM:HEAD_DIM + 1])
        o_ref[0, rows_of(i), :] = jnp.concatenate(outs, axis=0).T.astype(BF16)
        return 0

    lax.fori_loop(0, n_blocks, finish, 0, unroll=PREP_UNROLL)


def _attention(z, gq_t, gk_row, bias, w_up, w_down):
    b, s, _ = z.shape
    n_blocks = s // MOBA_BLOCK
    assert n_blocks % PIPE_UNROLL == 0 and PIPE_UNROLL % 2 == 0
    blk = MOBA_BLOCK
    items = _far_items(n_blocks)
    n_items = items.shape[0] - 1
    n_steps = b * N_PAIRS
    up_rows, dn_rows = w_up.shape[0] // n_steps, w_down.shape[0] // n_steps
    assert up_rows * n_steps == w_up.shape[0] and dn_rows * n_steps == w_down.shape[0]
    assert up_rows % BF16_SUBLANES == 0 and dn_rows % BF16_SUBLANES == 0

    def slab(bi, hp, tbl):
        return (bi * N_PAIRS + hp, 0)

    grid_spec = pltpu.PrefetchScalarGridSpec(
        num_scalar_prefetch=1,
        grid=(b, N_PAIRS),
        in_specs=[
            pl.BlockSpec((1, s, PAIR_W), lambda bi, hp, tbl: (bi, 0, hp)),
            pl.BlockSpec((1, s, PAIR_W), lambda bi, hp, tbl: (bi, 0, N_PAIRS + hp)),
            pl.BlockSpec((1, s, PAIR_W), lambda bi, hp, tbl: (bi, 0, 2 * N_PAIRS + hp)),
            pl.BlockSpec((PAIR_W, blk), lambda bi, hp, tbl: (0, 0)),
            pl.BlockSpec((1, PAIR_W), lambda bi, hp, tbl: (0, 0)),
            pl.BlockSpec((HEADS_PER_STEP, 2, blk, blk), lambda bi, hp, tbl: (hp, 0, 0, 0)),
            pl.BlockSpec((up_rows, w_up.shape[1]), slab),
            pl.BlockSpec((dn_rows, w_down.shape[1]), slab),
        ],
        out_specs=[
            pl.BlockSpec((1, s, PAIR_W), lambda bi, hp, tbl: (bi, 0, hp)),
            pl.BlockSpec((up_rows, w_up.shape[1]), slab),
            pl.BlockSpec((dn_rows, w_down.shape[1]), slab),
        ],
        scratch_shapes=[
            pltpu.VMEM((s, PAIR_W), BF16),
            pltpu.VMEM((n_blocks, HEADS_PER_STEP, V_ROWS, blk), BF16),
            pltpu.VMEM((n_blocks, HEADS_PER_STEP, PAIR_W, blk), BF16),
            pltpu.VMEM((HEADS_PER_STEP, n_blocks, n_blocks, blk), F32),
            pltpu.VMEM((n_blocks, HEADS_PER_STEP, V_ROWS, blk), F32),
            pltpu.VMEM((n_blocks, HEADS_PER_STEP, 1, blk), F32),
            pltpu.VMEM((HEADS_PER_STEP, 2, blk, blk), F32),
            pltpu.VMEM((HEADS_PER_STEP, 2, blk, blk), F32),
        ],
    )
    return pl.pallas_call(
        functools.partial(_attn_body, n_blocks=n_blocks, n_items=n_items),
        grid_spec=grid_spec,
        out_shape=[
            jax.ShapeDtypeStruct((b, s, ATTN_W), BF16),
            jax.ShapeDtypeStruct(w_up.shape, BF16),
            jax.ShapeDtypeStruct(w_down.shape, BF16),
        ],
        compiler_params=pltpu.CompilerParams(
            dimension_semantics=("arbitrary", "arbitrary"), vmem_limit_bytes=VMEM_LIMIT),
        name="attn",
    )(jnp.asarray(items.reshape(-1)), z, z, z, gq_t, gk_row, bias, w_up, w_down)


def _mix_body(x_ref, attn_ref, p_ref, ga_ref, gp_ref, mod_ref, pw32_ref, ps_ref, wba32_ref, wbp32_ref,
              wo32_ref, g2_ref, x1_ref, h2_ref, pe_ref, pw_ref, wba_ref, wbp_ref, wo_ref):
    for w32, wbf in ((pw32_ref, pw_ref), (wba32_ref, wba_ref), (wbp32_ref, wbp_ref), (wo32_ref, wo_ref)):
        _cast_weight_once(w32, wbf)
    t = pl.program_id(1)
    tm = MIX_TILE
    sub = MIX_TILE // MIX_SPLIT

    @pl.when(t == 0)
    def _():
        pe_ref[0:POOL_HALO, :] = jnp.zeros((POOL_HALO, POOL_W), F32)

    @pl.when(t > 0)
    def _():
        pe_ref[0:POOL_HALO, :] = pe_ref[tm:tm + POOL_HALO, :]

    pe_ref[POOL_HALO:POOL_HALO + tm, :] = p_ref[0].astype(F32)

    for part in range(MIX_SPLIT):
        r0 = part * sub
        rows = slice(r0, r0 + sub)
        pos = t * tm + r0 + lax.broadcasted_iota(jnp.int32, (sub, POOL_GROUP_W), 0)
        pooled = []
        for g, win in enumerate(POOL_WINDOWS):
            cols = slice(g * POOL_GROUP_W, (g + 1) * POOL_GROUP_W)
            ext = pe_ref[r0:r0 + POOL_HALO + sub, cols]
            tot = ext
            span = 1
            while span < win:
                tot = tot + pltpu.roll(tot, span, 0)
                span *= 2
            count = jnp.minimum(pos + 1, win).astype(F32)
            pooled_g = (tot[POOL_HALO:] / count - ext[POOL_HALO:]).astype(BF16)
            pooled.append(_dot(pooled_g, pw_ref[g]) * ps_ref[:, cols])
        pool = jnp.concatenate(pooled, axis=-1).astype(BF16)

        a_proj = _dot(attn_ref[0, rows, :], wba_ref[...])
        p_proj = _dot(pool, wbp_ref[...])
        merged = (_sigmoid(ga_ref[0, rows, :].astype(F32)) * a_proj
                  + _sigmoid(gp_ref[0, rows, :].astype(F32)) * p_proj).astype(BF16)
        x1 = x_ref[0, rows, :] + mod_ref[0, 2:3, :] * _dot(merged, wo_ref[...])
        x1_ref[0, rows, :] = x1
        h2_ref[0, rows, :] = _norm_mod(x1, g2_ref[...], mod_ref[0, 4:5, :], mod_ref[0, 3:4, :]).astype(BF16)


def _mix(x, attn, z, mod, pool_w, pool_scale, wba, wbp, wo, g2):
    b, s, d = x.shape
    tm = MIX_TILE
    const2 = lambda bi, t: (0, 0)
    return pl.pallas_call(
        _mix_body,
        grid=(b, s // tm),
        in_specs=[
            pl.BlockSpec((1, tm, d), lambda bi, t: (bi, t, 0)),
            pl.BlockSpec((1, tm, ATTN_W), lambda bi, t: (bi, t, 0)),
            pl.BlockSpec((1, tm, POOL_W), lambda bi, t: (bi, t, 3 * ATTN_W // POOL_W)),
            pl.BlockSpec((1, tm, d), lambda bi, t: (bi, t, (3 * ATTN_W + POOL_W) // d)),
            pl.BlockSpec((1, tm, d), lambda bi, t: (bi, t, (3 * ATTN_W + POOL_W) // d + 1)),
            pl.BlockSpec((1, 6, d), lambda bi, t: (bi, 0, 0)),
            pl.BlockSpec((POOL_GROUPS, POOL_GROUP_W, POOL_GROUP_W), lambda bi, t: (0, 0, 0)),
            pl.BlockSpec((1, POOL_W), const2),
            pl.BlockSpec((ATTN_W, d), const2),
            pl.BlockSpec((POOL_W, d), const2),
            pl.BlockSpec((d, d), const2),
            pl.BlockSpec((1, d), const2),
        ],
        out_specs=[
            pl.BlockSpec((1, tm, d), lambda bi, t: (bi, t, 0)),
            pl.BlockSpec((1, tm, d), lambda bi, t: (bi, t, 0)),
        ],
        out_shape=[
            jax.ShapeDtypeStruct((b, s, d), F32),
            jax.ShapeDtypeStruct((b, s, d), BF16),
        ],
        scratch_shapes=[
            pltpu.VMEM((tm + POOL_HALO, POOL_W), F32),
            pltpu.VMEM(pool_w.shape, BF16),
            pltpu.VMEM(wba.shape, BF16),
            pltpu.VMEM(wbp.shape, BF16),
            pltpu.VMEM(wo.shape, BF16),
        ],
        compiler_params=pltpu.CompilerParams(
            dimension_semantics=("arbitrary", "arbitrary"), vmem_limit_bytes=VMEM_LIMIT),
        name="mix",
    )(x, attn, z, z, z, mod, pool_w, pool_scale, wba, wbp, wo, g2)


def _ffn_body(x1_ref, h2_ref, mod_ref, wup_ref, cw_ref, cb_ref, wdn_ref, o_ref,
              hist0_ref, hist1_ref, carry_ref):
    t = pl.program_id(1)
    sub = FFN_TILE // FFN_SPLIT
    fc = FFN_CHUNK
    n_chunks = D_FF // fc
    hist = (hist0_ref, hist1_ref)

    @pl.when(t == 0)
    def _():
        carry_ref[...] = jnp.zeros_like(carry_ref)

    units = [(part, c) for part in range(FFN_SPLIT) for c in range(n_chunks)]
    h2 = [h2_ref[0, part * sub:(part + 1) * sub, :] for part in range(FFN_SPLIT)]

    def up(k):
        part, c = units[k]
        out = []
        for half in range(2):
            cols = slice(half * D_FF + c * fc, half * D_FF + (c + 1) * fc)
            hcols = slice(half * fc, (half + 1) * fc)
            r = _dot(h2[part], wup_ref[:, cols])
            hist[k % 2][0:CONV_HALO, hcols] = carry_ref[:, cols]
            hist[k % 2][CONV_HALO:CONV_HALO + sub, hcols] = r
            carry_ref[:, cols] = r[sub - CONV_HALO:sub]
            out.append(r)
        return out

    def down(acc, k, act):
        part, c = units[k]
        acc[part] = acc[part] + _dot(act, wdn_ref[c * fc:(c + 1) * fc, :])

    acc = [jnp.zeros((sub, D_MODEL), F32) for _ in range(FFN_SPLIT)]
    r_next = up(0)
    act_prev = None
    for k, (part, c) in enumerate(units):
        r_cur = r_next
        if k + 1 < len(units):
            r_next = up(k + 1)
        if act_prev is not None:
            down(acc, k - 1, act_prev)
        halves = []
        for half in range(2):
            cols = slice(half * D_FF + c * fc, half * D_FF + (c + 1) * fc)
            hcols = slice(half * fc, (half + 1) * fc)
            u = cb_ref[:, cols] + cw_ref[CONV_W - 1:CONV_W, cols] * r_cur[half]
            for tap in range(CONV_W - 1):
                back = CONV_W - 1 - tap
                u = u + cw_ref[tap:tap + 1, cols] * hist[k % 2][CONV_HALO - back:CONV_HALO - back + sub, hcols]
            halves.append(u)
        u_g, u_v = halves
        act_prev = (u_g * _sigmoid(u_g) * u_v).astype(BF16)
    down(acc, len(units) - 1, act_prev)
    for part in range(FFN_SPLIT):
        rows = slice(part * sub, (part + 1) * sub)
        o_ref[0, rows, :] = x1_ref[0, rows, :] + mod_ref[0, 5:6, :] * acc[part]


def _ffn(x1, h2, mod, wup_bf, conv_w, conv_b, wdn_bf):
    b, s, d = x1.shape
    tm = FFN_TILE
    const2 = lambda bi, t: (0, 0)
    return pl.pallas_call(
        _ffn_body,
        grid=(b, s // tm),
        in_specs=[
            pl.BlockSpec((1, tm, d), lambda bi, t: (bi, t, 0)),
            pl.BlockSpec((1, tm, d), lambda bi, t: (bi, t, 0)),
            pl.BlockSpec((1, 6, d), lambda bi, t: (bi, 0, 0)),
            pl.BlockSpec((d, 2 * D_FF), const2),
            pl.BlockSpec((CONV_W, 2 * D_FF), const2),
            pl.BlockSpec((1, 2 * D_FF), const2),
            pl.BlockSpec((D_FF, d), const2),
        ],
        out_specs=pl.BlockSpec((1, tm, d), lambda bi, t: (bi, t, 0)),
        out_shape=jax.ShapeDtypeStruct((b, s, d), F32),
        scratch_shapes=[
            pltpu.VMEM((tm // FFN_SPLIT + CONV_HALO, 2 * FFN_CHUNK), F32),
            pltpu.VMEM((tm // FFN_SPLIT + CONV_HALO, 2 * FFN_CHUNK), F32),
            pltpu.VMEM((CONV_HALO, 2 * D_FF), F32),
        ],
        compiler_params=pltpu.CompilerParams(
            dimension_semantics=("arbitrary", "arbitrary"), vmem_limit_bytes=VMEM_LIMIT),
        name="ffn",
    )(x1, h2, mod, wup_bf, conv_w, conv_b, wdn_bf)


def _layer(x, c_lanes, rel_bias, ada_w, ada_b, norm1_g, w_in, q_norm_g, k_norm_g, pool_w, pool_scale,
           w_branch_attn, w_branch_pool, w_out, norm2_g, w_up, conv_w, conv_b, w_down):
    b, s, d = x.shape
    mod = _ada(c_lanes, ada_w, ada_b[None, :])
    z = _inproj(x, mod, norm1_g[None, :], w_in)
    gq_t = jnp.broadcast_to(jnp.tile(q_norm_g, HEADS_PER_STEP)[:, None], (PAIR_W, MOBA_BLOCK))
    gk_row = jnp.tile(k_norm_g, HEADS_PER_STEP)[None, :]
    attn, wup_bf, wdn_bf = _attention(z, gq_t, gk_row, _biasprep(rel_bias), w_up, w_down)
    x1, h2 = _mix(x, attn, z, mod, pool_w, pool_scale[None, :], w_branch_attn, w_branch_pool, w_out,
                  norm2_g[None, :])
    return _ffn(x1, h2, mod, wup_bf, conv_w, conv_b[None, :], wdn_bf)


def kernel(x, c, ada_w, ada_b, norm1_g, w_in, q_norm_g, k_norm_g, rel_bias, pool_w, pool_scale,
           w_branch_attn, w_branch_pool, w_out, norm2_g, w_up, conv_w, conv_b, w_down):
    b, s, d = x.shape
    assert d == D_MODEL and w_in.shape[-1] == IN_W
    assert all(s % tile == 0 for tile in (MOBA_BLOCK * PIPE_UNROLL, INPROJ_TILE, MIX_TILE, FFN_TILE))
    c_lanes = jnp.broadcast_to(c[:, :, None], (b, d, LANES))
    for l in range(ada_w.shape[0]):
        x = _layer(x, c_lanes, rel_bias, ada_w[l], ada_b[l], norm1_g[l], w_in[l], q_norm_g[l],
                   k_norm_g[l], pool_w[l], pool_scale[l], w_branch_attn[l], w_branch_pool[l],
                   w_out[l], norm2_g[l], w_up[l], conv_w[l], conv_b[l], w_down[l])
    return x
```

```python
import functools
import math

import numpy as np
import jax
import jax.numpy as jnp
from jax import lax
from jax.experimental import pallas as pl
from jax.experimental.pallas import tpu as pltpu

F32 = jnp.float32
BF16 = jnp.bfloat16

D_MODEL = 1024
ATTN_HEADS = 8
HEAD_DIM = 64
ATTN_W = ATTN_HEADS * HEAD_DIM
MOBA_BLOCK = 256
MOBA_TOPK = 3
POOL_GROUPS = 4
POOL_GROUP_W = 128
POOL_W = POOL_GROUPS * POOL_GROUP_W
POOL_WINDOWS = (2, 4, 8, 16)
NUM_BUCKETS = 32
MAX_DISTANCE = 128
D_FF = 2816
CONV_W = 3
EPS = 1e-6
IN_W = 3 * ATTN_W + POOL_W + 2 * D_MODEL

LANES = 128
BF16_SUBLANES = 16
HEADS_PER_STEP = LANES // HEAD_DIM
PAIR_W = HEADS_PER_STEP * HEAD_DIM
N_PAIRS = ATTN_HEADS // HEADS_PER_STEP
V_ROWS = HEAD_DIM + BF16_SUBLANES
LOG2E = math.log2(math.e)
MASK_BIG = 1e30
ADA_ROWS = 256
INPROJ_TILE = 1024
INPROJ_COLS = 1024
INPROJ_SPLIT = 4
POOL_HALO = 16
MIX_TILE = 1024
MIX_SPLIT = 4
FFN_CHUNK = 256
FFN_TILE = 512
FFN_SPLIT = 2
CONV_HALO = 8
VMEM_LIMIT = 56 * 1024 * 1024


def _split_bf16(a):
    hi = a.astype(BF16)
    lo = (a - hi.astype(F32)).astype(BF16)
    return hi, lo


def _dot(a, b):
    return jnp.dot(a, b, preferred_element_type=F32)


def _dot3(a, b):
    ah, al = _split_bf16(a)
    bh, bl = _split_bf16(b)
    return _dot(ah, bh) + _dot(al, bh) + _dot(ah, bl)


def _sigmoid(v):
    return 1.0 / (1.0 + jnp.exp2(v * (-LOG2E)))


def _cast_weight_once(src_ref, dst_ref, col_chunk=4 * LANES):
    first = functools.reduce(jnp.logical_and, [pl.program_id(a) == 0 for a in range(2)])

    @pl.when(first)
    def _():
        n = src_ref.shape[-1]
        for c0 in range(0, n, col_chunk):
            dst_ref[..., c0:c0 + col_chunk] = src_ref[..., c0:c0 + col_chunk].astype(BF16)


def _ada_body(c_ref, w_ref, b_ref, o_ref):
    n_batch, _, lanes = c_ref.shape
    _, n_mod, dm = o_ref.shape

    @pl.when(pl.program_id(0) == 0)
    def _():
        for j in range(n_mod):
            o_ref[:, j, :] = jnp.broadcast_to(b_ref[:, j * dm:(j + 1) * dm], (n_batch, dm))

    for bi in range(n_batch):
        c = c_ref[bi]
        act = c * _sigmoid(c)
        for j in range(n_mod):
            cols = [jnp.sum(act * w_ref[:, j * dm + g * lanes:j * dm + (g + 1) * lanes], axis=0, keepdims=True)
                    for g in range(dm // lanes)]
            o_ref[bi, j:j + 1, :] += jnp.concatenate(cols, axis=-1)


def _ada(c_lanes, ada_w, ada_b):
    n_batch, d, lanes = c_lanes.shape
    n = ada_w.shape[1]
    n_mod = n // d
    tk = ADA_ROWS
    return pl.pallas_call(
        _ada_body,
        grid=(d // tk,),
        in_specs=[
            pl.BlockSpec((n_batch, tk, lanes), lambda k: (0, k, 0)),
            pl.BlockSpec((tk, n), lambda k: (k, 0)),
            pl.BlockSpec((1, n), lambda k: (0, 0)),
        ],
        out_specs=pl.BlockSpec((n_batch, n_mod, d), lambda k: (0, 0, 0)),
        out_shape=jax.ShapeDtypeStruct((n_batch, n_mod, d), F32),
        compiler_params=pltpu.CompilerParams(
            dimension_semantics=("arbitrary",), vmem_limit_bytes=VMEM_LIMIT),
        name="ada",
    )(c_lanes, ada_w, ada_b)


def _norm_mod(x, gain, scale, shift):
    ms = jnp.mean(x * x, axis=-1, keepdims=True)
    return x * lax.rsqrt(ms + EPS) * (gain * (1.0 + scale)) + shift


def _inproj_body(x_ref, mod_ref, g_ref, w32_ref, z_ref, w_ref):
    _cast_weight_once(w32_ref, w_ref)
    sub = INPROJ_TILE // INPROJ_SPLIT
    n_chunk = INPROJ_COLS
    for part in range(INPROJ_SPLIT):
        rows = slice(part * sub, (part + 1) * sub)
        h = _norm_mod(x_ref[0, rows, :], g_ref[...], mod_ref[0, 1:2, :], mod_ref[0, 0:1, :]).astype(BF16)
        for n in range(IN_W // n_chunk):
            cols = slice(n * n_chunk, (n + 1) * n_chunk)
            z_ref[0, rows, cols] = _dot(h, w_ref[:, cols]).astype(BF16)


def _inproj(x, mod, g1, w_in):
    b, s, d = x.shape
    return pl.pallas_call(
        _inproj_body,
        grid=(b, s // INPROJ_TILE),
        in_specs=[
            pl.BlockSpec((1, INPROJ_TILE, d), lambda bi, t: (bi, t, 0)),
            pl.BlockSpec((1, 6, d), lambda bi, t: (bi, 0, 0)),
            pl.BlockSpec((1, d), lambda bi, t: (0, 0)),
            pl.BlockSpec((d, IN_W), lambda bi, t: (0, 0)),
        ],
        out_specs=pl.BlockSpec((1, INPROJ_TILE, IN_W), lambda bi, t: (bi, t, 0)),
        out_shape=jax.ShapeDtypeStruct((b, s, IN_W), BF16),
        scratch_shapes=[pltpu.VMEM((d, IN_W), BF16)],
        compiler_params=pltpu.CompilerParams(
            dimension_semantics=("arbitrary", "arbitrary"), vmem_limit_bytes=VMEM_LIMIT),
        name="inproj",
    )(x, mod, g1, w_in)


def _t5_bucket_np(dist):
    max_exact = NUM_BUCKETS // 2
    n = np.maximum(dist, 0)
    nf = np.maximum(n, 1).astype(np.float64)
    large = max_exact + (np.log(nf / max_exact) / math.log(MAX_DISTANCE / max_exact)
                         * (NUM_BUCKETS - max_exact)).astype(np.int32)
    large = np.minimum(large, NUM_BUCKETS - 1)
    return np.where(n < max_exact, n, large).astype(np.int32)


BIAS_ROW_W = 4 * MOBA_BLOCK


def _bucket_row():
    d = np.arange(BIAS_ROW_W) - MOBA_BLOCK
    return np.where(d >= 0, _t5_bucket_np(d), -1).astype(np.int32)[None, :]


def _biasprep_body(rb_ref, bkt_ref, o_ref):
    h = pl.program_id(0)
    far = rb_ref[NUM_BUCKETS - 1, h]
    bkt = bkt_ref[...]
    row = jnp.where(bkt < 0, -MASK_BIG, 0.0).astype(F32)
    for bk in range(NUM_BUCKETS):
        row = jnp.where(bkt == bk, (rb_ref[bk, h] - far) * LOG2E, row)
    rows = jnp.broadcast_to(row, (MOBA_BLOCK, BIAS_ROW_W))
    rolled = pltpu.roll(rows, 0, 1, stride=1, stride_axis=0)
    o_ref[0, 0] = rolled[:, MOBA_BLOCK:2 * MOBA_BLOCK]
    o_ref[0, 1] = rolled[:, 2 * MOBA_BLOCK:3 * MOBA_BLOCK]


def _biasprep(rel_bias):
    bkt = jnp.asarray(_bucket_row())
    return pl.pallas_call(
        _biasprep_body,
        grid=(ATTN_HEADS,),
        in_specs=[
            pl.BlockSpec(memory_space=pltpu.SMEM),
            pl.BlockSpec((1, BIAS_ROW_W), lambda h: (0, 0)),
        ],
        out_specs=pl.BlockSpec((1, 2, MOBA_BLOCK, MOBA_BLOCK), lambda h: (h, 0, 0, 0)),
        out_shape=jax.ShapeDtypeStruct((ATTN_HEADS, 2, MOBA_BLOCK, MOBA_BLOCK), F32),
        compiler_params=pltpu.CompilerParams(dimension_semantics=("arbitrary",)),
        name="biasprep",
    )(rel_bias, bkt)


PREP_UNROLL = 8
ITEM_FIELDS = 5
PIPE_UNROLL = 16


def _far_items(n_blocks):
    never = n_blocks - 1
    items = []
    for i in range(n_blocks):
        n_far = max(i - 1, 0)
        for j0 in range(0, n_far, 2):
            items.append((i, j0, j0, j0 + 1, j0 + 1) if j0 + 1 < n_far else (i, j0, j0, j0, never))
    dummy = (n_blocks - 1, 0, never, 0, never)
    items.extend([dummy] * (-len(items) % PIPE_UNROLL))
    items.append(dummy)
    return np.asarray(items, np.int32)


def _attn_body(tbl_ref, zq_ref, zk_ref, zv_ref, gq_ref, gk_ref, bias_ref, wup32_ref, wdn32_ref,
               o_ref, wup_ref, wdn_ref,
               kn_ref, vta_ref, w_ref, pen_ref, acc_ref, m_ref, s0_ref, s1_ref, *, n_blocks, n_items):
    blk = MOBA_BLOCK
    heads = range(HEADS_PER_STEP)

    wup_ref[...] = wup32_ref[...].astype(BF16)
    wdn_ref[...] = wdn32_ref[...].astype(BF16)

    def rows_of(j):
        return pl.ds(pl.multiple_of(j * blk, blk), blk)

    blk_rows = lax.broadcasted_iota(jnp.int32, (n_blocks, PAIR_W), 0)
    ones = jnp.ones((V_ROWS - HEAD_DIM, blk), BF16)
    same_head = (lax.broadcasted_iota(jnp.int32, (PAIR_W, PAIR_W), 0) // HEAD_DIM
                 == lax.broadcasted_iota(jnp.int32, (PAIR_W, PAIR_W), 1) // HEAD_DIM)
    head_sum = jnp.where(same_head, 1.0, 0.0).astype(BF16)

    def kv_prep(i, kmean):
        kraw = zk_ref[0, rows_of(i), :].astype(F32)
        sq_hi, sq_lo = _split_bf16(kraw * kraw)
        ss = _dot(sq_hi, head_sum) + _dot(sq_lo, head_sum)
        rinv = lax.rsqrt(ss * (1.0 / HEAD_DIM) + EPS)
        kn = kraw * rinv * gk_ref[...]
        kn_ref[rows_of(i), :] = kn.astype(BF16)
        vt = zv_ref[0, rows_of(i), :].astype(F32).T
        for hd in heads:
            vta_ref[i, hd, 0:HEAD_DIM, :] = vt[hd * HEAD_DIM:(hd + 1) * HEAD_DIM].astype(BF16)
            vta_ref[i, hd, HEAD_DIM:V_ROWS, :] = ones
        return jnp.where(blk_rows == i, jnp.mean(kn, axis=0, keepdims=True), kmean)

    kmean = lax.fori_loop(0, n_blocks, kv_prep, jnp.zeros((n_blocks, PAIR_W), F32), unroll=PREP_UNROLL)

    blk_ids = lax.broadcasted_iota(jnp.int32, (n_blocks, blk), 0)
    zeros_half = jnp.zeros((HEAD_DIM, blk), F32)

    def q_prep(i, _):
        qt = zq_ref[0, rows_of(i), :].astype(F32).T
        qsq = qt * qt
        for hd in heads:
            rows = slice(hd * HEAD_DIM, (hd + 1) * HEAD_DIM)
            ss = jnp.sum(qsq[rows], axis=0, keepdims=True)
            r = lax.rsqrt(ss * (1.0 / HEAD_DIM) + EPS)
            qn = qt[rows] * r * gq_ref[rows, :] * (HEAD_DIM ** -0.5 * LOG2E)
            parts = [zeros_half] * HEADS_PER_STEP
            parts[hd] = qn
            w = jnp.concatenate(parts, axis=0)
            w_ref[i, hd] = w.astype(BF16)
            gate = jnp.where(blk_ids < i, _dot3(kmean, w), -jnp.inf)
            sel = jnp.zeros((n_blocks, blk), jnp.bool_)
            for _ in range(MOBA_TOPK):
                mx = jnp.max(gate, axis=0, keepdims=True)
                cand = jnp.where(gate == mx, blk_ids, n_blocks)
                pick = (blk_ids == jnp.min(cand, axis=0, keepdims=True)) & (mx > -jnp.inf)
                sel = sel | pick
                gate = jnp.where(pick, -jnp.inf, gate)
            pen_ref[hd, i] = jnp.where(sel, 0.0, -MASK_BIG).astype(F32)
        return 0

    lax.fori_loop(0, n_blocks, q_prep, 0, unroll=PREP_UNROLL)

    def score(i, blocks, s_ref, biased):
        keys = [kn_ref[rows_of(j), :] for j, _ in blocks]
        gmax = []
        for hd in heads:
            w = w_ref[i, hd]
            tmax = []
            for u, (_, jpen) in enumerate(blocks):
                s = _dot(keys[u], w)
                if biased:
                    s = s + bias_ref[hd, u]
                s_ref[hd, u] = s
                t = jnp.max(s, axis=0, keepdims=True)
                tmax.append(t if jpen is None else t + pen_ref[hd, i, pl.ds(jpen, 1), :])
            gmax.append(functools.reduce(jnp.maximum, tmax))
        return tuple(gmax)

    def attend(i, blocks, s_ref, gmax, first_visit):
        for hd in heads:
            if first_visit:
                m_new = gmax[hd]
                acc = None
            else:
                m_old = m_ref[i, hd]
                m_new = jnp.maximum(m_old, gmax[hd])
                acc = acc_ref[i, hd] * jnp.exp2(m_old - m_new)
            for u, (j, jpen) in enumerate(blocks):
                shift = m_new if jpen is None else m_new - pen_ref[hd, i, pl.ds(jpen, 1), :]
                p = jnp.exp2(s_ref[hd, u] - shift).astype(BF16)
                pv = _dot(vta_ref[j, hd], p)
                acc = pv if acc is None else acc + pv
            acc_ref[i, hd] = acc
            m_ref[i, hd] = m_new

    def near_blocks(i):
        jp = jnp.maximum(i - 1, 0)
        return ((i, None), (jp, jp))

    slots = (s0_ref, s1_ref)

    def near_body(t, g):
        for u in range(PIPE_UNROLL):
            i_cur = PIPE_UNROLL * t + u
            i_nxt = jnp.minimum(i_cur + 1, n_blocks - 1)
            g_nxt = score(i_nxt, near_blocks(i_nxt), slots[(u + 1) % 2], True)
            attend(i_cur, near_blocks(i_cur), slots[u % 2], g, True)
            g = g_nxt
        return g

    lax.fori_loop(0, n_blocks // PIPE_UNROLL, near_body, score(0, near_blocks(0), s0_ref, True))

    def item(k):
        base = k * ITEM_FIELDS
        i = tbl_ref[base]
        return i, ((tbl_ref[base + 1], tbl_ref[base + 2]), (tbl_ref[base + 3], tbl_ref[base + 4]))

    def far_body(t, g):
        for u in range(PIPE_UNROLL):
            k = PIPE_UNROLL * t + u
            g_nxt = score(*item(k + 1), slots[(u + 1) % 2], False)
            attend(*item(k), slots[u % 2], g, False)
            g = g_nxt
        return g

    lax.fori_loop(0, n_items // PIPE_UNROLL, far_body, score(*item(0), s0_ref, False))

    def finish(i, _):
        outs = []
        for hd in heads:
            acc = acc_ref[i, hd]
            outs.append(acc[0:HEAD_DIM] / acc[HEAD_DIM:HEAD_DIM + 1])
        o_ref[0, rows_of(i), :] = jnp.concatenate(outs, axis=0).T.astype(BF16)
        return 0

    lax.fori_loop(0, n_blocks, finish, 0, unroll=PREP_UNROLL)


def _attention(z, gq_t, gk_row, bias, w_up, w_down):
    b, s, _ = z.shape
    n_blocks = s // MOBA_BLOCK
    assert n_blocks % PIPE_UNROLL == 0 and PIPE_UNROLL % 2 == 0
    blk = MOBA_BLOCK
    items = _far_items(n_blocks)
    n_items = items.shape[0] - 1
    n_steps = b * N_PAIRS
    up_rows, dn_rows = w_up.shape[0] // n_steps, w_down.shape[0] // n_steps
    assert up_rows * n_steps == w_up.shape[0] and dn_rows * n_steps == w_down.shape[0]
    assert up_rows % BF16_SUBLANES == 0 and dn_rows % BF16_SUBLANES == 0

    def slab(bi, hp, tbl):
        return (bi * N_PAIRS + hp, 0)

    grid_spec = pltpu.PrefetchScalarGridSpec(
        num_scalar_prefetch=1,
        grid=(b, N_PAIRS),
        in_specs=[
            pl.BlockSpec((1, s, PAIR_W), lambda bi, hp, tbl: (bi, 0, hp)),
            pl.BlockSpec((1, s, PAIR_W), lambda bi, hp, tbl: (bi, 0, N_PAIRS + hp)),
            pl.BlockSpec((1, s, PAIR_W), lambda bi, hp, tbl: (bi, 0, 2 * N_PAIRS + hp)),
            pl.BlockSpec((PAIR_W, blk), lambda bi, hp, tbl: (0, 0)),
            pl.BlockSpec((1, PAIR_W), lambda bi, hp, tbl: (0, 0)),
            pl.BlockSpec((HEADS_PER_STEP, 2, blk, blk), lambda bi, hp, tbl: (hp, 0, 0, 0)),
            pl.BlockSpec((up_rows, w_up.shape[1]), slab),
            pl.BlockSpec((dn_rows, w_down.shape[1]), slab),
        ],
        out_specs=[
            pl.BlockSpec((1, s, PAIR_W), lambda bi, hp, tbl: (bi, 0, hp)),
            pl.BlockSpec((up_rows, w_up.shape[1]), slab),
            pl.BlockSpec((dn_rows, w_down.shape[1]), slab),
        ],
        scratch_shapes=[
            pltpu.VMEM((s, PAIR_W), BF16),
            pltpu.VMEM((n_blocks, HEADS_PER_STEP, V_ROWS, blk), BF16),
            pltpu.VMEM((n_blocks, HEADS_PER_STEP, PAIR_W, blk), BF16),
            pltpu.VMEM((HEADS_PER_STEP, n_blocks, n_blocks, blk), F32),
            pltpu.VMEM((n_blocks, HEADS_PER_STEP, V_ROWS, blk), F32),
            pltpu.VMEM((n_blocks, HEADS_PER_STEP, 1, blk), F32),
            pltpu.VMEM((HEADS_PER_STEP, 2, blk, blk), F32),
            pltpu.VMEM((HEADS_PER_STEP, 2, blk, blk), F32),
        ],
    )
    return pl.pallas_call(
        functools.partial(_attn_body, n_blocks=n_blocks, n_items=n_items),
        grid_spec=grid_spec,
        out_shape=[
            jax.ShapeDtypeStruct((b, s, ATTN_W), BF16),
            jax.ShapeDtypeStruct(w_up.shape, BF16),
            jax.ShapeDtypeStruct(w_down.shape, BF16),
        ],
        compiler_params=pltpu.CompilerParams(
            dimension_semantics=("arbitrary", "arbitrary"), vmem_limit_bytes=VMEM_LIMIT),
        name="attn",
    )(jnp.asarray(items.reshape(-1)), z, z, z, gq_t, gk_row, bias, w_up, w_down)


def _mix_body(x_ref, attn_ref, p_ref, ga_ref, gp_ref, mod_ref, pw32_ref, ps_ref, wba32_ref, wbp32_ref,
              wo32_ref, g2_ref, x1_ref, h2_ref, pe_ref, pw_ref, wba_ref, wbp_ref, wo_ref):
    for w32, wbf in ((pw32_ref, pw_ref), (wba32_ref, wba_ref), (wbp32_ref, wbp_ref), (wo32_ref, wo_ref)):
        _cast_weight_once(w32, wbf)
    t = pl.program_id(1)
    tm = MIX_TILE
    sub = MIX_TILE // MIX_SPLIT

    @pl.when(t == 0)
    def _():
        pe_ref[0:POOL_HALO, :] = jnp.zeros((POOL_HALO, POOL_W), F32)

    @pl.when(t > 0)
    def _():
        pe_ref[0:POOL_HALO, :] = pe_ref[tm:tm + POOL_HALO, :]

    pe_ref[POOL_HALO:POOL_HALO + tm, :] = p_ref[0].astype(F32)

    for part in range(MIX_SPLIT):
        r0 = part * sub
        rows = slice(r0, r0 + sub)
        pos = t * tm + r0 + lax.broadcasted_iota(jnp.int32, (sub, POOL_GROUP_W), 0)
        pooled = []
        for g, win in enumerate(POOL_WINDOWS):
            cols = slice(g * POOL_GROUP_W, (g + 1) * POOL_GROUP_W)
            ext = pe_ref[r0:r0 + POOL_HALO + sub, cols]
            tot = ext
            span = 1
            while span < win:
                tot = tot + pltpu.roll(tot, span, 0)
                span *= 2
            count = jnp.minimum(pos + 1, win).astype(F32)
            pooled_g = (tot[POOL_HALO:] / count - ext[POOL_HALO:]).astype(BF16)
            pooled.append(_dot(pooled_g, pw_ref[g]) * ps_ref[:, cols])
        pool = jnp.concatenate(pooled, axis=-1).astype(BF16)

        a_proj = _dot(attn_ref[0, rows, :], wba_ref[...])
        p_proj = _dot(pool, wbp_ref[...])
        merged = (_sigmoid(ga_ref[0, rows, :].astype(F32)) * a_proj
                  + _sigmoid(gp_ref[0, rows, :].astype(F32)) * p_proj).astype(BF16)
        x1 = x_ref[0, rows, :] + mod_ref[0, 2:3, :] * _dot(merged, wo_ref[...])
        x1_ref[0, rows, :] = x1
        h2_ref[0, rows, :] = _norm_mod(x1, g2_ref[...], mod_ref[0, 4:5, :], mod_ref[0, 3:4, :]).astype(BF16)


def _mix(x, attn, z, mod, pool_w, pool_scale, wba, wbp, wo, g2):
    b, s, d = x.shape
    tm = MIX_TILE
    const2 = lambda bi, t: (0, 0)
    return pl.pallas_call(
        _mix_body,
        grid=(b, s // tm),
        in_specs=[
            pl.BlockSpec((1, tm, d), lambda bi, t: (bi, t, 0)),
            pl.BlockSpec((1, tm, ATTN_W), lambda bi, t: (bi, t, 0)),
            pl.BlockSpec((1, tm, POOL_W), lambda bi, t: (bi, t, 3 * ATTN_W // POOL_W)),
            pl.BlockSpec((1, tm, d), lambda bi, t: (bi, t, (3 * ATTN_W + POOL_W) // d)),
            pl.BlockSpec((1, tm, d), lambda bi, t: (bi, t, (3 * ATTN_W + POOL_W) // d + 1)),
            pl.BlockSpec((1, 6, d), lambda bi, t: (bi, 0, 0)),
            pl.BlockSpec((POOL_GROUPS, POOL_GROUP_W, POOL_GROUP_W), lambda bi, t: (0, 0, 0)),
            pl.BlockSpec((1, POOL_W), const2),
            pl.BlockSpec((ATTN_W, d), const2),
            pl.BlockSpec((POOL_W, d), const2),
            pl.BlockSpec((d, d), const2),
            pl.BlockSpec((1, d), const2),
        ],
        out_specs=[
            pl.BlockSpec((1, tm, d), lambda bi, t: (bi, t, 0)),
            pl.BlockSpec((1, tm, d), lambda bi, t: (bi, t, 0)),
        ],
        out_shape=[
            jax.ShapeDtypeStruct((b, s, d), F32),
            jax.ShapeDtypeStruct((b, s, d), BF16),
        ],
        scratch_shapes=[
            pltpu.VMEM((tm + POOL_HALO, POOL_W), F32),
            pltpu.VMEM(pool_w.shape, BF16),
            pltpu.VMEM(wba.shape, BF16),
            pltpu.VMEM(wbp.shape, BF16),
            pltpu.VMEM(wo.shape, BF16),
        ],
        compiler_params=pltpu.CompilerParams(
            dimension_semantics=("arbitrary", "arbitrary"), vmem_limit_bytes=VMEM_LIMIT),
        name="mix",
    )(x, attn, z, z, z, mod, pool_w, pool_scale, wba, wbp, wo, g2)


def _ffn_body(x1_ref, h2_ref, mod_ref, wup_ref, cw_ref, cb_ref, wdn_ref, o_ref,
              hist0_ref, hist1_ref, carry_ref):
    t = pl.program_id(1)
    sub = FFN_TILE // FFN_SPLIT
    fc = FFN_CHUNK
    n_chunks = D_FF // fc
    hist = (hist0_ref, hist1_ref)

    @pl.when(t == 0)
    def _():
        carry_ref[...] = jnp.zeros_like(carry_ref)

    units = [(part, c) for part in range(FFN_SPLIT) for c in range(n_chunks)]
    h2 = [h2_ref[0, part * sub:(part + 1) * sub, :] for part in range(FFN_SPLIT)]

    def up(k):
        part, c = units[k]
        out = []
        for half in range(2):
            cols = slice(half * D_FF + c * fc, half * D_FF + (c + 1) * fc)
            hcols = slice(half * fc, (half + 1) * fc)
            r = _dot(h2[part], wup_ref[:, cols])
            hist[k % 2][0:CONV_HALO, hcols] = carry_ref[:, cols]
            hist[k % 2][CONV_HALO:CONV_HALO + sub, hcols] = r
            carry_ref[:, cols] = r[sub - CONV_HALO:sub]
            out.append(r)
        return out

    def down(acc, k, act):
        part, c = units[k]
        acc[part] = acc[part] + _dot(act, wdn_ref[c * fc:(c + 1) * fc, :])

    acc = [jnp.zeros((sub, D_MODEL), F32) for _ in range(FFN_SPLIT)]
    r_next = up(0)
    act_prev = None
    for k, (part, c) in enumerate(units):
        r_cur = r_next
        if k + 1 < len(units):
            r_next = up(k + 1)
        if act_prev is not None:
            down(acc, k - 1, act_prev)
        halves = []
        for half in range(2):
            cols = slice(half * D_FF + c * fc, half * D_FF + (c + 1) * fc)
            hcols = slice(half * fc, (half + 1) * fc)
            u = cb_ref[:, cols] + cw_ref[CONV_W - 1:CONV_W, cols] * r_cur[half]
            for tap in range(CONV_W - 1):
                back = CONV_W - 1 - tap
                u = u + cw_ref[tap:tap + 1, cols] * hist[k % 2][CONV_HALO - back:CONV_HALO - back + sub, hcols]
            halves.append(u)
        u_g, u_v = halves
        act_prev = (u_g * _sigmoid(u_g) * u_v).astype(BF16)
    down(acc, len(units) - 1, act_prev)
    for part in range(FFN_SPLIT):
        rows = slice(part * sub, (part + 1) * sub)
        o_ref[0, rows, :] = x1_ref[0, rows, :] + mod_ref[0, 5:6, :] * acc[part]


def _ffn(x1, h2, mod, wup_bf, conv_w, conv_b, wdn_bf):
    b, s, d = x1.shape
    tm = FFN_TILE
    const2 = lambda bi, t: (0, 0)
    return pl.pallas_call(
        _ffn_body,
        grid=(b, s // tm),
        in_specs=[
            pl.BlockSpec((1, tm, d), lambda bi, t: (bi, t, 0)),
            pl.BlockSpec((1, tm, d), lambda bi, t: (bi, t, 0)),
            pl.BlockSpec((1, 6, d), lambda bi, t: (bi, 0, 0)),
            pl.BlockSpec((d, 2 * D_FF), const2),
            pl.BlockSpec((CONV_W, 2 * D_FF), const2),
            pl.BlockSpec((1, 2 * D_FF), const2),
            pl.BlockSpec((D_FF, d), const2),
        ],
        out_specs=pl.BlockSpec((1, tm, d), lambda bi, t: (bi, t, 0)),
        out_shape=jax.ShapeDtypeStruct((b, s, d), F32),
        scratch_shapes=[
            pltpu.VMEM((tm // FFN_SPLIT + CONV_HALO, 2 * FFN_CHUNK), F32),
            pltpu.VMEM((tm // FFN_SPLIT + CONV_HALO, 2 * FFN_CHUNK), F32),
            pltpu.VMEM((CONV_HALO, 2 * D_FF), F32),
        ],
        compiler_params=pltpu.CompilerParams(
            dimension_semantics=("arbitrary", "arbitrary"), vmem_limit_bytes=VMEM_LIMIT),
        name="ffn",
    )(x1, h2, mod, wup_bf, conv_w, conv_b, wdn_bf)


TAIL_TILE = 512
TAIL_SPLIT = 2
MIX_STAGE_AFTER = ((1, 2, 3, 5), (11, 12, 13, 15))


def _tail_body(x_ref, attn_ref, p_ref, ga_ref, gp_ref, modm_ref, modf_ref, pw_ref, ps_ref, wba_ref, wbp_ref,
               wo_ref, g2_ref, wup_ref, cw_ref, cb_ref, wdn_ref, o_ref,
               pe_ref, x1s_ref, h2s_ref, hist0_ref, hist1_ref, carry_ref, *, tiles_per_batch):
    n = pl.program_id(0)
    tm = TAIL_TILE
    sub = tm // TAIL_SPLIT
    fc = FFN_CHUNK
    n_chunks = D_FF // fc
    hist = (hist0_ref, hist1_ref)
    n_tiles = pl.num_programs(0) - 1
    t_mix = jnp.minimum(n, n_tiles - 1) % tiles_per_batch
    t_ffn = jnp.maximum(n - 1, 0) % tiles_per_batch
    slot_mix = n % 2
    slot_ffn = 1 - slot_mix

    @pl.when(n == 0)
    def _():
        x1s_ref[1] = jnp.zeros((tm, D_MODEL), F32)
        h2s_ref[1] = jnp.zeros((tm, D_MODEL), BF16)

    @pl.when(t_ffn == 0)
    def _():
        carry_ref[...] = jnp.zeros_like(carry_ref)

    @pl.when(t_mix == 0)
    def _():
        pe_ref[0:POOL_HALO, :] = jnp.zeros((POOL_HALO, POOL_W), F32)

    @pl.when(t_mix > 0)
    def _():
        pe_ref[0:POOL_HALO, :] = pe_ref[tm:tm + POOL_HALO, :]

    pe_ref[POOL_HALO:POOL_HALO + tm, :] = p_ref[0].astype(F32)

    def mix_part(part):
        r0 = part * sub
        rows = slice(r0, r0 + sub)
        yield
        pos = t_mix * tm + r0 + lax.broadcasted_iota(jnp.int32, (sub, POOL_GROUP_W), 0)
        pooled_in = []
        for g, win in enumerate(POOL_WINDOWS):
            cols = slice(g * POOL_GROUP_W, (g + 1) * POOL_GROUP_W)
            ext = pe_ref[r0:r0 + POOL_HALO + sub, cols]
            tot = ext
            span = 1
            while span < win:
                tot = tot + pltpu.roll(tot, span, 0)
                span *= 2
            count = jnp.minimum(pos + 1, win).astype(F32)
            pooled_in.append((tot[POOL_HALO:] / count - ext[POOL_HALO:]).astype(BF16))
        yield
        pooled = [_dot(pooled_in[g], pw_ref[g]) * ps_ref[:, g * POOL_GROUP_W:(g + 1) * POOL_GROUP_W]
                  for g in range(POOL_GROUPS)]
        pool = jnp.concatenate(pooled, axis=-1).astype(BF16)
        yield
        a_proj = _dot(attn_ref[0, rows, :], wba_ref[...])
        p_proj = _dot(pool, wbp_ref[...])
        merged = (_sigmoid(ga_ref[0, rows, :].astype(F32)) * a_proj
                  + _sigmoid(gp_ref[0, rows, :].astype(F32)) * p_proj).astype(BF16)
        yield
        x1 = x_ref[0, rows, :] + modm_ref[0, 2:3, :] * _dot(merged, wo_ref[...])
        x1s_ref[slot_mix, rows, :] = x1
        h2s_ref[slot_mix, rows, :] = _norm_mod(x1, g2_ref[...], modm_ref[0, 4:5, :],
                                               modm_ref[0, 3:4, :]).astype(BF16)
        yield

    mix_gens = [mix_part(part) for part in range(TAIL_SPLIT)]
    for gen in mix_gens:
        next(gen)

    def mix_advance(k):
        for part, gen in enumerate(mix_gens):
            if k in MIX_STAGE_AFTER[part]:
                next(gen)

    units = [(part, c) for part in range(TAIL_SPLIT) for c in range(n_chunks)]

    def up(k):
        part, c = units[k]
        h2 = h2s_ref[slot_ffn, part * sub:(part + 1) * sub, :]
        out = []
        for half in range(2):
            cols = slice(half * D_FF + c * fc, half * D_FF + (c + 1) * fc)
            hcols = slice(half * fc, (half + 1) * fc)
            r = _dot(h2, wup_ref[:, cols])
            hist[k % 2][0:CONV_HALO, hcols] = carry_ref[:, cols]
            hist[k % 2][CONV_HALO:CONV_HALO + sub, hcols] = r
            carry_ref[:, cols] = r[sub - CONV_HALO:sub]
            out.append(r)
        return out

    acc = [jnp.zeros((sub, D_MODEL), F32) for _ in range(TAIL_SPLIT)]
    r_next = up(0)
    act_prev = None
    for k, (part, c) in enumerate(units):
        r_cur = r_next
        if k + 1 < len(units):
            r_next = up(k + 1)
        if act_prev is not None:
            pp, pc = units[k - 1]
            acc[pp] = acc[pp] + _dot(act_prev, wdn_ref[pc * fc:(pc + 1) * fc, :])
        mix_advance(k)
        halves = []
        for half in range(2):
            cols = slice(half * D_FF + c * fc, half * D_FF + (c + 1) * fc)
            hcols = slice(half * fc, (half + 1) * fc)
            u = cb_ref[:, cols] + cw_ref[CONV_W - 1:CONV_W, cols] * r_cur[half]
            for tap in range(CONV_W - 1):
                back = CONV_W - 1 - tap
                u = u + cw_ref[tap:tap + 1, cols] * hist[k % 2][CONV_HALO - back:CONV_HALO - back + sub, hcols]
            halves.append(u)
        u_g, u_v = halves
        act_prev = (u_g * _sigmoid(u_g) * u_v).astype(BF16)
    pp, pc = units[-1]
    acc[pp] = acc[pp] + _dot(act_prev, wdn_ref[pc * fc:(pc + 1) * fc, :])
    for part in range(TAIL_SPLIT):
        rows = slice(part * sub, (part + 1) * sub)
        o_ref[0, rows, :] = x1s_ref[slot_ffn, rows, :] + modf_ref[0, 5:6, :] * acc[part]


def _tail(x, attn, z, mod, pool_w_bf, pool_scale, wba_bf, wbp_bf, wo_bf, g2, wup_bf, conv_w, conv_b, wdn_bf):
    b, s, d = x.shape
    tm = TAIL_TILE
    tpb = s // tm
    n_tiles = b * tpb

    def mix_tile(n, col=0):
        m = jnp.minimum(n, n_tiles - 1)
        return (m // tpb, m % tpb, col)

    def ffn_tile(n):
        f = jnp.maximum(n - 1, 0)
        return (f // tpb, f % tpb, 0)

    const2 = lambda n: (0, 0)
    sub = tm // TAIL_SPLIT
    return pl.pallas_call(
        functools.partial(_tail_body, tiles_per_batch=tpb),
        grid=(n_tiles + 1,),
        in_specs=[
            pl.BlockSpec((1, tm, d), lambda n: mix_tile(n)),
            pl.BlockSpec((1, tm, ATTN_W), lambda n: mix_tile(n)),
            pl.BlockSpec((1, tm, POOL_W), lambda n: mix_tile(n, 3 * ATTN_W // POOL_W)),
            pl.BlockSpec((1, tm, d), lambda n: mix_tile(n, (3 * ATTN_W + POOL_W) // d)),
            pl.BlockSpec((1, tm, d), lambda n: mix_tile(n, (3 * ATTN_W + POOL_W) // d + 1)),
            pl.BlockSpec((1, 6, d), lambda n: (jnp.minimum(n, n_tiles - 1) // tpb, 0, 0)),
            pl.BlockSpec((1, 6, d), lambda n: (jnp.maximum(n - 1, 0) // tpb, 0, 0)),
            pl.BlockSpec((POOL_GROUPS, POOL_GROUP_W, POOL_GROUP_W), lambda n: (0, 0, 0)),
            pl.BlockSpec((1, POOL_W), const2),
            pl.BlockSpec((ATTN_W, d), const2),
            pl.BlockSpec((POOL_W, d), const2),
            pl.BlockSpec((d, d), const2),
            pl.BlockSpec((1, d), const2),
            pl.BlockSpec((d, 2 * D_FF), const2),
            pl.BlockSpec((CONV_W, 2 * D_FF), const2),
            pl.BlockSpec((1, 2 * D_FF), const2),
            pl.BlockSpec((D_FF, d), const2),
        ],
        out_specs=pl.BlockSpec((1, tm, d), lambda n: ffn_tile(n)),
        out_shape=jax.ShapeDtypeStruct((b, s, d), F32),
        scratch_shapes=[
            pltpu.VMEM((tm + POOL_HALO, POOL_W), F32),
            pltpu.VMEM((2, tm, d), F32),
            pltpu.VMEM((2, tm, d), BF16),
            pltpu.VMEM((sub + CONV_HALO, 2 * FFN_CHUNK), F32),
            pltpu.VMEM((sub + CONV_HALO, 2 * FFN_CHUNK), F32),
            pltpu.VMEM((CONV_HALO, 2 * D_FF), F32),
        ],
        compiler_params=pltpu.CompilerParams(
            dimension_semantics=("arbitrary",), vmem_limit_bytes=VMEM_LIMIT),
        name="tail",
    )(x, attn, z, z, z, mod, mod, pool_w_bf, pool_scale, wba_bf, wbp_bf, wo_bf, g2, wup_bf, conv_w, conv_b, wdn_bf)


def _layer(x, c_lanes, rel_bias, ada_w, ada_b, norm1_g, w_in, q_norm_g, k_norm_g, pool_w, pool_scale,
           w_branch_attn, w_branch_pool, w_out, norm2_g, w_up, conv_w, conv_b, w_down):
    b, s, d = x.shape
    mod = _ada(c_lanes, ada_w, ada_b[None, :])
    z = _inproj(x, mod, norm1_g[None, :], w_in)
    gq_t = jnp.broadcast_to(jnp.tile(q_norm_g, HEADS_PER_STEP)[:, None], (PAIR_W, MOBA_BLOCK))
    gk_row = jnp.tile(k_norm_g, HEADS_PER_STEP)[None, :]
    attn, wup_bf, wdn_bf = _attention(z, gq_t, gk_row, _biasprep(rel_bias), w_up, w_down)
    return _tail(x, attn, z, mod, pool_w.astype(BF16), pool_scale[None, :], w_branch_attn.astype(BF16),
                 w_branch_pool.astype(BF16), w_out.astype(BF16), norm2_g[None, :], wup_bf, conv_w,
                 conv_b[None, :], wdn_bf)


def kernel(x, c, ada_w, ada_b, norm1_g, w_in, q_norm_g, k_norm_g, rel_bias, pool_w, pool_scale,
           w_branch_attn, w_branch_pool, w_out, norm2_g, w_up, conv_w, conv_b, w_down):
    b, s, d = x.shape
    assert d == D_MODEL and w_in.shape[-1] == IN_W
    assert all(s % tile == 0 for tile in (MOBA_BLOCK * PIPE_UNROLL, INPROJ_TILE, MIX_TILE, FFN_TILE))
    c_lanes = jnp.broadcast_to(c[:, :, None], (b, d, LANES))
    for l in range(ada_w.shape[0]):
        x = _layer(x, c_lanes, rel_bias, ada_w[l], ada_b[l], norm1_g[l], w_in[l], q_norm_g[l],
                   k_norm_g[l], pool_w[l], pool_scale[l], w_branch_attn[l], w_branch_pool[l],
                   w_out[l], norm2_g[l], w_up[l], conv_w[l], conv_b[l], w_down[l])
    return x
```

```python
import functools
import math

import numpy as np
import jax
import jax.numpy as jnp
from jax import lax
from jax.experimental import pallas as pl
from jax.experimental.pallas import tpu as pltpu

F32 = jnp.float32
BF16 = jnp.bfloat16

D_MODEL = 1024
ATTN_HEADS = 8
HEAD_DIM = 64
ATTN_W = ATTN_HEADS * HEAD_DIM
MOBA_BLOCK = 256
MOBA_TOPK = 3
POOL_GROUPS = 4
POOL_GROUP_W = 128
POOL_W = POOL_GROUPS * POOL_GROUP_W
POOL_WINDOWS = (2, 4, 8, 16)
NUM_BUCKETS = 32
MAX_DISTANCE = 128
D_FF = 2816
CONV_W = 3
EPS = 1e-6
IN_W = 3 * ATTN_W + POOL_W + 2 * D_MODEL

LANES = 128
BF16_SUBLANES = 16
HEADS_PER_STEP = LANES // HEAD_DIM
PAIR_W = HEADS_PER_STEP * HEAD_DIM
N_PAIRS = ATTN_HEADS // HEADS_PER_STEP
V_ROWS = HEAD_DIM + BF16_SUBLANES
LOG2E = math.log2(math.e)
MASK_BIG = 1e30
ADA_ROWS = 256
INPROJ_TILE = 1024
INPROJ_COLS = 1024
INPROJ_SPLIT = 4
POOL_HALO = 16
MIX_TILE = 512
MIX_SPLIT = 2
FFN_CHUNK = 256
FFN_TILE = 512
FFN_SPLIT = 2
CONV_HALO = 8
VMEM_LIMIT = 56 * 1024 * 1024


def _split_bf16(a):
    hi = a.astype(BF16)
    lo = (a - hi.astype(F32)).astype(BF16)
    return hi, lo


def _dot(a, b):
    return jnp.dot(a, b, preferred_element_type=F32)


def _dot3(a, b):
    ah, al = _split_bf16(a)
    bh, bl = _split_bf16(b)
    return _dot(ah, bh) + _dot(al, bh) + _dot(ah, bl)


def _sigmoid(v):
    return 1.0 / (1.0 + jnp.exp2(v * (-LOG2E)))


def _cast_weight_once(src_ref, dst_ref, col_chunk=4 * LANES):
    first = functools.reduce(jnp.logical_and, [pl.program_id(a) == 0 for a in range(2)])

    @pl.when(first)
    def _():
        n = src_ref.shape[-1]
        for c0 in range(0, n, col_chunk):
            dst_ref[..., c0:c0 + col_chunk] = src_ref[..., c0:c0 + col_chunk].astype(BF16)


def _ada_body(c_ref, w_ref, b_ref, o_ref):
    n_batch, _, lanes = c_ref.shape
    _, n_mod, dm = o_ref.shape

    @pl.when(pl.program_id(0) == 0)
    def _():
        for j in range(n_mod):
            o_ref[:, j, :] = jnp.broadcast_to(b_ref[:, j * dm:(j + 1) * dm], (n_batch, dm))

    for bi in range(n_batch):
        c = c_ref[bi]
        act = c * _sigmoid(c)
        for j in range(n_mod):
            cols = [jnp.sum(act * w_ref[:, j * dm + g * lanes:j * dm + (g + 1) * lanes], axis=0, keepdims=True)
                    for g in range(dm // lanes)]
            o_ref[bi, j:j + 1, :] += jnp.concatenate(cols, axis=-1)


def _ada(c_lanes, ada_w, ada_b):
    n_batch, d, lanes = c_lanes.shape
    n = ada_w.shape[1]
    n_mod = n // d
    tk = ADA_ROWS
    return pl.pallas_call(
        _ada_body,
        grid=(d // tk,),
        in_specs=[
            pl.BlockSpec((n_batch, tk, lanes), lambda k: (0, k, 0)),
            pl.BlockSpec((tk, n), lambda k: (k, 0)),
            pl.BlockSpec((1, n), lambda k: (0, 0)),
        ],
        out_specs=pl.BlockSpec((n_batch, n_mod, d), lambda k: (0, 0, 0)),
        out_shape=jax.ShapeDtypeStruct((n_batch, n_mod, d), F32),
        compiler_params=pltpu.CompilerParams(
            dimension_semantics=("arbitrary",), vmem_limit_bytes=VMEM_LIMIT),
        name="ada",
    )(c_lanes, ada_w, ada_b)


def _norm_mod(x, gain, scale, shift):
    ms = jnp.mean(x * x, axis=-1, keepdims=True)
    return x * lax.rsqrt(ms + EPS) * (gain * (1.0 + scale)) + shift


def _inproj_body(x_ref, mod_ref, g_ref, w32_ref, z_ref, w_ref):
    _cast_weight_once(w32_ref, w_ref)
    sub = INPROJ_TILE // INPROJ_SPLIT
    n_chunk = INPROJ_COLS
    for part in range(INPROJ_SPLIT):
        rows = slice(part * sub, (part + 1) * sub)
        h = _norm_mod(x_ref[0, rows, :], g_ref[...], mod_ref[0, 1:2, :], mod_ref[0, 0:1, :]).astype(BF16)
        for n in range(IN_W // n_chunk):
            cols = slice(n * n_chunk, (n + 1) * n_chunk)
            z_ref[0, rows, cols] = _dot(h, w_ref[:, cols]).astype(BF16)


def _inproj(x, mod, g1, w_in):
    b, s, d = x.shape
    return pl.pallas_call(
        _inproj_body,
        grid=(b, s // INPROJ_TILE),
        in_specs=[
            pl.BlockSpec((1, INPROJ_TILE, d), lambda bi, t: (bi, t, 0)),
            pl.BlockSpec((1, 6, d), lambda bi, t: (bi, 0, 0)),
            pl.BlockSpec((1, d), lambda bi, t: (0, 0)),
            pl.BlockSpec((d, IN_W), lambda bi, t: (0, 0)),
        ],
        out_specs=pl.BlockSpec((1, INPROJ_TILE, IN_W), lambda bi, t: (bi, t, 0)),
        out_shape=jax.ShapeDtypeStruct((b, s, IN_W), BF16),
        scratch_shapes=[pltpu.VMEM((d, IN_W), BF16)],
        compiler_params=pltpu.CompilerParams(
            dimension_semantics=("arbitrary", "arbitrary"), vmem_limit_bytes=VMEM_LIMIT),
        name="inproj",
    )(x, mod, g1, w_in)


def _t5_bucket_np(dist):
    max_exact = NUM_BUCKETS // 2
    n = np.maximum(dist, 0)
    nf = np.maximum(n, 1).astype(np.float64)
    large = max_exact + (np.log(nf / max_exact) / math.log(MAX_DISTANCE / max_exact)
                         * (NUM_BUCKETS - max_exact)).astype(np.int32)
    large = np.minimum(large, NUM_BUCKETS - 1)
    return np.where(n < max_exact, n, large).astype(np.int32)


BIAS_ROW_W = 4 * MOBA_BLOCK


def _bucket_row():
    d = np.arange(BIAS_ROW_W) - MOBA_BLOCK
    return np.where(d >= 0, _t5_bucket_np(d), -1).astype(np.int32)[None, :]


def _biasprep_body(rb_ref, bkt_ref, o_ref):
    bkt = bkt_ref[...]
    for h in range(ATTN_HEADS):
        far = rb_ref[NUM_BUCKETS - 1, h]
        row = jnp.where(bkt < 0, -MASK_BIG, 0.0).astype(F32)
        for bk in range(NUM_BUCKETS):
            row = jnp.where(bkt == bk, (rb_ref[bk, h] - far) * LOG2E, row)
        rows = jnp.broadcast_to(row, (MOBA_BLOCK, BIAS_ROW_W))
        rolled = pltpu.roll(rows, 0, 1, stride=1, stride_axis=0)
        o_ref[h, 0] = rolled[:, MOBA_BLOCK:2 * MOBA_BLOCK]
        o_ref[h, 1] = rolled[:, 2 * MOBA_BLOCK:3 * MOBA_BLOCK]


def _biasprep(rel_bias):
    bkt = jnp.asarray(_bucket_row())
    return pl.pallas_call(
        _biasprep_body,
        grid=(1,),
        in_specs=[
            pl.BlockSpec(memory_space=pltpu.SMEM),
            pl.BlockSpec((1, BIAS_ROW_W), lambda n: (0, 0)),
        ],
        out_specs=pl.BlockSpec((ATTN_HEADS, 2, MOBA_BLOCK, MOBA_BLOCK), lambda n: (0, 0, 0, 0)),
        out_shape=jax.ShapeDtypeStruct((ATTN_HEADS, 2, MOBA_BLOCK, MOBA_BLOCK), F32),
        compiler_params=pltpu.CompilerParams(dimension_semantics=("arbitrary",)),
        name="biasprep",
    )(rel_bias, bkt)


PREP_UNROLL = 8
ITEM_FIELDS = 5
PIPE_UNROLL = 16


def _far_items(n_blocks):
    never = n_blocks - 1
    items = []
    for i in range(n_blocks):
        n_far = max(i - 1, 0)
        for j0 in range(0, n_far, 2):
            items.append((i, j0, j0, j0 + 1, j0 + 1) if j0 + 1 < n_far else (i, j0, j0, j0, never))
    dummy = (n_blocks - 1, 0, never, 0, never)
    items.extend([dummy] * (-len(items) % PIPE_UNROLL))
    items.append(dummy)
    return np.asarray(items, np.int32)


def _attn_body(tbl_ref, zq_ref, zk_ref, zv_ref, gq_ref, gk_ref, bias_ref, wup32_ref, wdn32_ref,
               o_ref, wup_ref, wdn_ref,
               kn_ref, vta_ref, w_ref, pen_ref, acc_ref, m_ref, s0_ref, s1_ref, *, n_blocks, n_items):
    blk = MOBA_BLOCK
    heads = range(HEADS_PER_STEP)

    wup_ref[...] = wup32_ref[...].astype(BF16)
    wdn_ref[...] = wdn32_ref[...].astype(BF16)

    def rows_of(j):
        return pl.ds(pl.multiple_of(j * blk, blk), blk)

    blk_rows = lax.broadcasted_iota(jnp.int32, (n_blocks, PAIR_W), 0)
    ones = jnp.ones((V_ROWS - HEAD_DIM, blk), BF16)
    same_head = (lax.broadcasted_iota(jnp.int32, (PAIR_W, PAIR_W), 0) // HEAD_DIM
                 == lax.broadcasted_iota(jnp.int32, (PAIR_W, PAIR_W), 1) // HEAD_DIM)
    head_sum = jnp.where(same_head, 1.0, 0.0).astype(BF16)

    def kv_prep(i, kmean):
        kraw = zk_ref[0, rows_of(i), :].astype(F32)
        sq_hi, sq_lo = _split_bf16(kraw * kraw)
        ss = _dot(sq_hi, head_sum) + _dot(sq_lo, head_sum)
        rinv = lax.rsqrt(ss * (1.0 / HEAD_DIM) + EPS)
        kn = kraw * rinv * gk_ref[...]
        kn_ref[rows_of(i), :] = kn.astype(BF16)
        vt = zv_ref[0, rows_of(i), :].astype(F32).T
        for hd in heads:
            vta_ref[i, hd, 0:HEAD_DIM, :] = vt[hd * HEAD_DIM:(hd + 1) * HEAD_DIM].astype(BF16)
            vta_ref[i, hd, HEAD_DIM:V_ROWS, :] = ones
        return jnp.where(blk_rows == i, jnp.mean(kn, axis=0, keepdims=True), kmean)

    kmean = lax.fori_loop(0, n_blocks, kv_prep, jnp.zeros((n_blocks, PAIR_W), F32), unroll=PREP_UNROLL)

    blk_ids = lax.broadcasted_iota(jnp.int32, (n_blocks, blk), 0)
    zeros_half = jnp.zeros((HEAD_DIM, blk), F32)

    def q_prep(i, _):
        qt = zq_ref[0, rows_of(i), :].astype(F32).T
        qsq = qt * qt
        for hd in heads:
            rows = slice(hd * HEAD_DIM, (hd + 1) * HEAD_DIM)
            ss = jnp.sum(qsq[rows], axis=0, keepdims=True)
            r = lax.rsqrt(ss * (1.0 / HEAD_DIM) + EPS)
            qn = qt[rows] * r * gq_ref[rows, :] * (HEAD_DIM ** -0.5 * LOG2E)
            parts = [zeros_half] * HEADS_PER_STEP
            parts[hd] = qn
            w = jnp.concatenate(parts, axis=0)
            w_ref[i, hd] = w.astype(BF16)
            gate = jnp.where(blk_ids < i, _dot3(kmean, w), -jnp.inf)
            sel = jnp.zeros((n_blocks, blk), jnp.bool_)
            for _ in range(MOBA_TOPK):
                mx = jnp.max(gate, axis=0, keepdims=True)
                cand = jnp.where(gate == mx, blk_ids, n_blocks)
                pick = (blk_ids == jnp.min(cand, axis=0, keepdims=True)) & (mx > -jnp.inf)
                sel = sel | pick
                gate = jnp.where(pick, -jnp.inf, gate)
            pen_ref[hd, i] = jnp.where(sel, 0.0, -MASK_BIG).astype(F32)
        return 0

    lax.fori_loop(0, n_blocks, q_prep, 0, unroll=PREP_UNROLL)

    def score(i, blocks, s_ref, biased):
        keys = [kn_ref[rows_of(j), :] for j, _ in blocks]
        gmax = []
        for hd in heads:
            w = w_ref[i, hd]
            tmax = []
            for u, (_, jpen) in enumerate(blocks):
                s = _dot(keys[u], w)
                if biased:
                    s = s + bias_ref[hd, u]
                s_ref[hd, u] = s
                t = jnp.max(s, axis=0, keepdims=True)
                tmax.append(t if jpen is None else t + pen_ref[hd, i, pl.ds(jpen, 1), :])
            gmax.append(functools.reduce(jnp.maximum, tmax))
        return tuple(gmax)

    def attend(i, blocks, s_ref, gmax, first_visit):
        for hd in heads:
            if first_visit:
                m_new = gmax[hd]
                acc = None
            else:
                m_old = m_ref[i, hd]
                m_new = jnp.maximum(m_old, gmax[hd])
                acc = acc_ref[i, hd] * jnp.exp2(m_old - m_new)
            for u, (j, jpen) in enumerate(blocks):
                shift = m_new if jpen is None else m_new - pen_ref[hd, i, pl.ds(jpen, 1), :]
                p = jnp.exp2(s_ref[hd, u] - shift).astype(BF16)
                pv = _dot(vta_ref[j, hd], p)
                acc = pv if acc is None else acc + pv
            acc_ref[i, hd] = acc
            m_ref[i, hd] = m_new

    def near_blocks(i):
        jp = jnp.maximum(i - 1, 0)
        return ((i, None), (jp, jp))

    slots = (s0_ref, s1_ref)

    def near_body(t, g):
        for u in range(PIPE_UNROLL):
            i_cur = PIPE_UNROLL * t + u
            i_nxt = jnp.minimum(i_cur + 1, n_blocks - 1)
            g_nxt = score(i_nxt, near_blocks(i_nxt), slots[(u + 1) % 2], True)
            attend(i_cur, near_blocks(i_cur), slots[u % 2], g, True)
            g = g_nxt
        return g

    lax.fori_loop(0, n_blocks // PIPE_UNROLL, near_body, score(0, near_blocks(0), s0_ref, True))

    def item(k):
        base = k * ITEM_FIELDS
        i = tbl_ref[base]
        return i, ((tbl_ref[base + 1], tbl_ref[base + 2]), (tbl_ref[base + 3], tbl_ref[base + 4]))

    def far_body(t, g):
        for u in range(PIPE_UNROLL):
            k = PIPE_UNROLL * t + u
            g_nxt = score(*item(k + 1), slots[(u + 1) % 2], False)
            attend(*item(k), slots[u % 2], g, False)
            g = g_nxt
        return g

    lax.fori_loop(0, n_items // PIPE_UNROLL, far_body, score(*item(0), s0_ref, False))

    def finish(i, _):
        outs = []
        for hd in heads:
            acc = acc_ref[i, hd]
            outs.append(acc[0:HEAD_DIM] / acc[HEAD_DIM:HEAD_DIM + 1])
        o_ref[0, rows_of(i), :] = jnp.concatenate(outs, axis=0).T.astype(BF16)
        return 0

    lax.fori_loop(0, n_blocks, finish, 0, unroll=PREP_UNROLL)


def _attention(z, gq_t, gk_row, bias, w_up, w_down):
    b, s, _ = z.shape
    n_blocks = s // MOBA_BLOCK
    assert n_blocks % PIPE_UNROLL == 0 and PIPE_UNROLL % 2 == 0
    blk = MOBA_BLOCK
    items = _far_items(n_blocks)
    n_items = items.shape[0] - 1
    n_steps = b * N_PAIRS
    up_rows, dn_rows = w_up.shape[0] // n_steps, w_down.shape[0] // n_steps
    assert up_rows * n_steps == w_up.shape[0] and dn_rows * n_steps == w_down.shape[0]
    assert up_rows % BF16_SUBLANES == 0 and dn_rows % BF16_SUBLANES == 0

    def slab(bi, hp, tbl):
        return (bi * N_PAIRS + hp, 0)

    grid_spec = pltpu.PrefetchScalarGridSpec(
        num_scalar_prefetch=1,
        grid=(b, N_PAIRS),
        in_specs=[
            pl.BlockSpec((1, s, PAIR_W), lambda bi, hp, tbl: (bi, 0, hp)),
            pl.BlockSpec((1, s, PAIR_W), lambda bi, hp, tbl: (bi, 0, N_PAIRS + hp)),
            pl.BlockSpec((1, s, PAIR_W), lambda bi, hp, tbl: (bi, 0, 2 * N_PAIRS + hp)),
            pl.BlockSpec((PAIR_W, blk), lambda bi, hp, tbl: (0, 0)),
            pl.BlockSpec((1, PAIR_W), lambda bi, hp, tbl: (0, 0)),
            pl.BlockSpec((HEADS_PER_STEP, 2, blk, blk), lambda bi, hp, tbl: (hp, 0, 0, 0)),
            pl.BlockSpec((up_rows, w_up.shape[1]), slab),
            pl.BlockSpec((dn_rows, w_down.shape[1]), slab),
        ],
        out_specs=[
            pl.BlockSpec((1, s, PAIR_W), lambda bi, hp, tbl: (bi, 0, hp)),
            pl.BlockSpec((up_rows, w_up.shape[1]), slab),
            pl.BlockSpec((dn_rows, w_down.shape[1]), slab),
        ],
        scratch_shapes=[
            pltpu.VMEM((s, PAIR_W), BF16),
            pltpu.VMEM((n_blocks, HEADS_PER_STEP, V_ROWS, blk), BF16),
            pltpu.VMEM((n_blocks, HEADS_PER_STEP, PAIR_W, blk), BF16),
            pltpu.VMEM((HEADS_PER_STEP, n_blocks, n_blocks, blk), F32),
            pltpu.VMEM((n_blocks, HEADS_PER_STEP, V_ROWS, blk), F32),
            pltpu.VMEM((n_blocks, HEADS_PER_STEP, 1, blk), F32),
            pltpu.VMEM((HEADS_PER_STEP, 2, blk, blk), F32),
            pltpu.VMEM((HEADS_PER_STEP, 2, blk, blk), F32),
        ],
    )
    return pl.pallas_call(
        functools.partial(_attn_body, n_blocks=n_blocks, n_items=n_items),
        grid_spec=grid_spec,
        out_shape=[
            jax.ShapeDtypeStruct((b, s, ATTN_W), BF16),
            jax.ShapeDtypeStruct(w_up.shape, BF16),
            jax.ShapeDtypeStruct(w_down.shape, BF16),
        ],
        compiler_params=pltpu.CompilerParams(
            dimension_semantics=("arbitrary", "arbitrary"), vmem_limit_bytes=VMEM_LIMIT),
        name="attn",
    )(jnp.asarray(items.reshape(-1)), z, z, z, gq_t, gk_row, bias, w_up, w_down)


def _mix_body(x_ref, attn_ref, p_ref, ga_ref, gp_ref, mod_ref, pw32_ref, ps_ref, wba32_ref, wbp32_ref,
              wo32_ref, g2_ref, x1_ref, h2_ref, pe_ref, pw_ref, wba_ref, wbp_ref, wo_ref):
    for w32, wbf in ((pw32_ref, pw_ref), (wba32_ref, wba_ref), (wbp32_ref, wbp_ref), (wo32_ref, wo_ref)):
        _cast_weight_once(w32, wbf)
    t = pl.program_id(1)
    tm = MIX_TILE
    sub = MIX_TILE // MIX_SPLIT

    @pl.when(t == 0)
    def _():
        pe_ref[0:POOL_HALO, :] = jnp.zeros((POOL_HALO, POOL_W), F32)

    @pl.when(t > 0)
    def _():
        pe_ref[0:POOL_HALO, :] = pe_ref[tm:tm + POOL_HALO, :]

    pe_ref[POOL_HALO:POOL_HALO + tm, :] = p_ref[0].astype(F32)

    for part in range(MIX_SPLIT):
        r0 = part * sub
        rows = slice(r0, r0 + sub)
        pos = t * tm + r0 + lax.broadcasted_iota(jnp.int32, (sub, POOL_GROUP_W), 0)
        pooled = []
        for g, win in enumerate(POOL_WINDOWS):
            cols = slice(g * POOL_GROUP_W, (g + 1) * POOL_GROUP_W)
            ext = pe_ref[r0:r0 + POOL_HALO + sub, cols]
            tot = ext
            span = 1
            while span < win:
                tot = tot + pltpu.roll(tot, span, 0)
                span *= 2
            count = jnp.minimum(pos + 1, win).astype(F32)
            pooled_g = (tot[POOL_HALO:] / count - ext[POOL_HALO:]).astype(BF16)
            pooled.append(_dot(pooled_g, pw_ref[g]) * ps_ref[:, cols])
        pool = jnp.concatenate(pooled, axis=-1).astype(BF16)

        a_proj = _dot(attn_ref[0, rows, :], wba_ref[...])
        p_proj = _dot(pool, wbp_ref[...])
        merged = (_sigmoid(ga_ref[0, rows, :].astype(F32)) * a_proj
                  + _sigmoid(gp_ref[0, rows, :].astype(F32)) * p_proj).astype(BF16)
        x1 = x_ref[0, rows, :] + mod_ref[0, 2:3, :] * _dot(merged, wo_ref[...])
        x1_ref[0, rows, :] = x1
        h2_ref[0, rows, :] = _norm_mod(x1, g2_ref[...], mod_ref[0, 4:5, :], mod_ref[0, 3:4, :]).astype(BF16)


def _mix(x, attn, z, mod, pool_w, pool_scale, wba, wbp, wo, g2):
    b, s, d = x.shape
    tm = MIX_TILE
    const2 = lambda bi, t: (0, 0)
    return pl.pallas_call(
        _mix_body,
        grid=(b, s // tm),
        in_specs=[
            pl.BlockSpec((1, tm, d), lambda bi, t: (bi, t, 0)),
            pl.BlockSpec((1, tm, ATTN_W), lambda bi, t: (bi, t, 0)),
            pl.BlockSpec((1, tm, POOL_W), lambda bi, t: (bi, t, 3 * ATTN_W // POOL_W)),
            pl.BlockSpec((1, tm, d), lambda bi, t: (bi, t, (3 * ATTN_W + POOL_W) // d)),
            pl.BlockSpec((1, tm, d), lambda bi, t: (bi, t, (3 * ATTN_W + POOL_W) // d + 1)),
            pl.BlockSpec((1, 6, d), lambda bi, t: (bi, 0, 0)),
            pl.BlockSpec((POOL_GROUPS, POOL_GROUP_W, POOL_GROUP_W), lambda bi, t: (0, 0, 0)),
            pl.BlockSpec((1, POOL_W), const2),
            pl.BlockSpec((ATTN_W, d), const2),
            pl.BlockSpec((POOL_W, d), const2),
            pl.BlockSpec((d, d), const2),
            pl.BlockSpec((1, d), const2),
        ],
        out_specs=[
            pl.BlockSpec((1, tm, d), lambda bi, t: (bi, t, 0)),
            pl.BlockSpec((1, tm, d), lambda bi, t: (bi, t, 0)),
        ],
        out_shape=[
            jax.ShapeDtypeStruct((b, s, d), F32),
            jax.ShapeDtypeStruct((b, s, d), BF16),
        ],
        scratch_shapes=[
            pltpu.VMEM((tm + POOL_HALO, POOL_W), F32),
            pltpu.VMEM(pool_w.shape, BF16),
            pltpu.VMEM(wba.shape, BF16),
            pltpu.VMEM(wbp.shape, BF16),
            pltpu.VMEM(wo.shape, BF16),
        ],
        compiler_params=pltpu.CompilerParams(
            dimension_semantics=("arbitrary", "arbitrary"), vmem_limit_bytes=VMEM_LIMIT),
        name="mix",
    )(x, attn, z, z, z, mod, pool_w, pool_scale, wba, wbp, wo, g2)


def _ffn_body(x1_ref, h2_ref, mod_ref, wup_ref, cw_ref, cb_ref, wdn_ref, o_ref,
              hist0_ref, hist1_ref, carry_ref):
    t = pl.program_id(1)
    sub = FFN_TILE // FFN_SPLIT
    fc = FFN_CHUNK
    n_chunks = D_FF // fc
    hist = (hist0_ref, hist1_ref)

    @pl.when(t == 0)
    def _():
        carry_ref[...] = jnp.zeros_like(carry_ref)

    units = [(part, c) for part in range(FFN_SPLIT) for c in range(n_chunks)]
    h2 = [h2_ref[0, part * sub:(part + 1) * sub, :] for part in range(FFN_SPLIT)]

    def up(k):
        part, c = units[k]
        out = []
        for half in range(2):
            cols = slice(half * D_FF + c * fc, half * D_FF + (c + 1) * fc)
            hcols = slice(half * fc, (half + 1) * fc)
            r = _dot(h2[part], wup_ref[:, cols])
            hist[k % 2][0:CONV_HALO, hcols] = carry_ref[:, cols]
            hist[k % 2][CONV_HALO:CONV_HALO + sub, hcols] = r
            carry_ref[:, cols] = r[sub - CONV_HALO:sub]
            out.append(r)
        return out

    def down(acc, k, act):
        part, c = units[k]
        acc[part] = acc[part] + _dot(act, wdn_ref[c * fc:(c + 1) * fc, :])

    acc = [jnp.zeros((sub, D_MODEL), F32) for _ in range(FFN_SPLIT)]
    r_next = up(0)
    act_prev = None
    for k, (part, c) in enumerate(units):
        r_cur = r_next
        if k + 1 < len(units):
            r_next = up(k + 1)
        if act_prev is not None:
            down(acc, k - 1, act_prev)
        halves = []
        for half in range(2):
            cols = slice(half * D_FF + c * fc, half * D_FF + (c + 1) * fc)
            hcols = slice(half * fc, (half + 1) * fc)
            u = cb_ref[:, cols] + cw_ref[CONV_W - 1:CONV_W, cols] * r_cur[half]
            for tap in range(CONV_W - 1):
                back = CONV_W - 1 - tap
                u = u + cw_ref[tap:tap + 1, cols] * hist[k % 2][CONV_HALO - back:CONV_HALO - back + sub, hcols]
            halves.append(u)
        u_g, u_v = halves
        act_prev = (u_g * _sigmoid(u_g) * u_v).astype(BF16)
    down(acc, len(units) - 1, act_prev)
    for part in range(FFN_SPLIT):
        rows = slice(part * sub, (part + 1) * sub)
        o_ref[0, rows, :] = x1_ref[0, rows, :] + mod_ref[0, 5:6, :] * acc[part]


def _ffn(x1, h2, mod, wup_bf, conv_w, conv_b, wdn_bf):
    b, s, d = x1.shape
    tm = FFN_TILE
    const2 = lambda bi, t: (0, 0)
    return pl.pallas_call(
        _ffn_body,
        grid=(b, s // tm),
        in_specs=[
            pl.BlockSpec((1, tm, d), lambda bi, t: (bi, t, 0)),
            pl.BlockSpec((1, tm, d), lambda bi, t: (bi, t, 0)),
            pl.BlockSpec((1, 6, d), lambda bi, t: (bi, 0, 0)),
            pl.BlockSpec((d, 2 * D_FF), const2),
            pl.BlockSpec((CONV_W, 2 * D_FF), const2),
            pl.BlockSpec((1, 2 * D_FF), const2),
            pl.BlockSpec((D_FF, d), const2),
        ],
        out_specs=pl.BlockSpec((1, tm, d), lambda bi, t: (bi, t, 0)),
        out_shape=jax.ShapeDtypeStruct((b, s, d), F32),
        scratch_shapes=[
            pltpu.VMEM((tm // FFN_SPLIT + CONV_HALO, 2 * FFN_CHUNK), F32),
            pltpu.VMEM((tm // FFN_SPLIT + CONV_HALO, 2 * FFN_CHUNK), F32),
            pltpu.VMEM((CONV_HALO, 2 * D_FF), F32),
        ],
        compiler_params=pltpu.CompilerParams(
            dimension_semantics=("arbitrary", "arbitrary"), vmem_limit_bytes=VMEM_LIMIT),
        name="ffn",
    )(x1, h2, mod, wup_bf, conv_w, conv_b, wdn_bf)


def _layer(x, c_lanes, rel_bias, ada_w, ada_b, norm1_g, w_in, q_norm_g, k_norm_g, pool_w, pool_scale,
           w_branch_attn, w_branch_pool, w_out, norm2_g, w_up, conv_w, conv_b, w_down):
    b, s, d = x.shape
    mod = _ada(c_lanes, ada_w, ada_b[None, :])
    z = _inproj(x, mod, norm1_g[None, :], w_in)
    gq_t = jnp.broadcast_to(jnp.tile(q_norm_g, HEADS_PER_STEP)[:, None], (PAIR_W, MOBA_BLOCK))
    gk_row = jnp.tile(k_norm_g, HEADS_PER_STEP)[None, :]
    attn, wup_bf, wdn_bf = _attention(z, gq_t, gk_row, _biasprep(rel_bias), w_up, w_down)
    x1, h2 = _mix(x, attn, z, mod, pool_w, pool_scale[None, :], w_branch_attn, w_branch_pool, w_out,
                  norm2_g[None, :])
    return _ffn(x1, h2, mod, wup_bf, conv_w, conv_b[None, :], wdn_bf)


def kernel(x, c, ada_w, ada_b, norm1_g, w_in, q_norm_g, k_norm_g, rel_bias, pool_w, pool_scale,
           w_branch_attn, w_branch_pool, w_out, norm2_g, w_up, conv_w, conv_b, w_down):
    b, s, d = x.shape
    assert d == D_MODEL and w_in.shape[-1] == IN_W
    assert all(s % tile == 0 for tile in (MOBA_BLOCK * PIPE_UNROLL, INPROJ_TILE, MIX_TILE, FFN_TILE))
    c_lanes = jnp.broadcast_to(c[:, :, None], (b, d, LANES))
    for l in range(ada_w.shape[0]):
        x = _layer(x, c_lanes, rel_bias, ada_w[l], ada_b[l], norm1_g[l], w_in[l], q_norm_g[l],
                   k_norm_g[l], pool_w[l], pool_scale[l], w_branch_attn[l], w_branch_pool[l],
                   w_out[l], norm2_g[l], w_up[l], conv_w[l], conv_b[l], w_down[l])
    return x
```

```python
import functools
import math

import numpy as np
import jax
import jax.numpy as jnp
from jax import lax
from jax.experimental import pallas as pl
from jax.experimental.pallas import tpu as pltpu

F32 = jnp.float32
BF16 = jnp.bfloat16

D_MODEL = 1024
ATTN_HEADS = 8
HEAD_DIM = 64
ATTN_W = ATTN_HEADS * HEAD_DIM
MOBA_BLOCK = 256
MOBA_TOPK = 3
POOL_GROUPS = 4
POOL_GROUP_W = 128
POOL_W = POOL_GROUPS * POOL_GROUP_W
POOL_WINDOWS = (2, 4, 8, 16)
NUM_BUCKETS = 32
MAX_DISTANCE = 128
D_FF = 2816
CONV_W = 3
EPS = 1e-6
IN_W = 3 * ATTN_W + POOL_W + 2 * D_MODEL

LANES = 128
BF16_SUBLANES = 16
HEADS_PER_STEP = LANES // HEAD_DIM
PAIR_W = HEADS_PER_STEP * HEAD_DIM
N_PAIRS = ATTN_HEADS // HEADS_PER_STEP
V_ROWS = HEAD_DIM + BF16_SUBLANES
LOG2E = math.log2(math.e)
MASK_BIG = 1e30
ADA_ROWS = 256
INPROJ_TILE = 1024
INPROJ_COLS = 1024
INPROJ_SPLIT = 4
POOL_HALO = 16
MIX_TILE = 512
MIX_SPLIT = 2
FFN_CHUNK = 256
FFN_TILE = 512
FFN_SPLIT = 2
CONV_HALO = 8
VMEM_LIMIT = 56 * 1024 * 1024


def _split_bf16(a):
    hi = a.astype(BF16)
    lo = (a - hi.astype(F32)).astype(BF16)
    return hi, lo


def _dot(a, b):
    return jnp.dot(a, b, preferred_element_type=F32)


def _dot3(a, b):
    ah, al = _split_bf16(a)
    bh, bl = _split_bf16(b)
    return _dot(ah, bh) + _dot(al, bh) + _dot(ah, bl)


def _sigmoid(v):
    return 1.0 / (1.0 + jnp.exp2(v * (-LOG2E)))


def _cast_weight_once(src_ref, dst_ref, col_chunk=4 * LANES):
    first = functools.reduce(jnp.logical_and, [pl.program_id(a) == 0 for a in range(2)])

    @pl.when(first)
    def _():
        n = src_ref.shape[-1]
        for c0 in range(0, n, col_chunk):
            dst_ref[..., c0:c0 + col_chunk] = src_ref[..., c0:c0 + col_chunk].astype(BF16)


N_MOD = 6
N_MOD_EARLY = 2


def _ada_accumulate(c_ref, w_refs, b_refs, o_ref, first):
    n_batch, _, lanes = c_ref.shape
    dm = o_ref.shape[2]
    vectors = [(w_ref, b_ref, j) for w_ref, b_ref in zip(w_refs, b_refs) for j in range(w_ref.shape[1] // dm)]

    @pl.when(first)
    def _():
        for v, (_, b_ref, j) in enumerate(vectors):
            o_ref[:, v, :] = jnp.broadcast_to(b_ref[:, j * dm:(j + 1) * dm], (n_batch, dm))

    for bi in range(n_batch):
        c = c_ref[bi]
        act = c * _sigmoid(c)
        for v, (w_ref, _, j) in enumerate(vectors):
            cols = [jnp.sum(act * w_ref[:, j * dm + g * lanes:j * dm + (g + 1) * lanes], axis=0, keepdims=True)
                    for g in range(dm // lanes)]
            o_ref[bi, v:v + 1, :] += jnp.concatenate(cols, axis=-1)


def _ada_body(c_ref, w_ref, b_ref, o_ref):
    _ada_accumulate(c_ref, [w_ref], [b_ref], o_ref, pl.program_id(0) == 0)


def _ada_early(c_lanes, ada_w, ada_b):
    n_batch, d, lanes = c_lanes.shape
    n = N_MOD_EARLY * d
    tk = ADA_ROWS
    return pl.pallas_call(
        _ada_body,
        grid=(d // tk,),
        in_specs=[
            pl.BlockSpec((n_batch, tk, lanes), lambda k: (0, k, 0)),
            pl.BlockSpec((tk, n), lambda k: (k, 0)),
            pl.BlockSpec((1, n), lambda k: (0, 0)),
        ],
        out_specs=pl.BlockSpec((n_batch, N_MOD_EARLY, d), lambda k: (0, 0, 0)),
        out_shape=jax.ShapeDtypeStruct((n_batch, N_MOD_EARLY, d), F32),
        compiler_params=pltpu.CompilerParams(
            dimension_semantics=("arbitrary",), vmem_limit_bytes=VMEM_LIMIT),
        name="ada",
    )(c_lanes, ada_w, ada_b)


def _norm_mod(x, gain, scale, shift):
    ms = jnp.mean(x * x, axis=-1, keepdims=True)
    return x * lax.rsqrt(ms + EPS) * (gain * (1.0 + scale)) + shift


def _inproj_body(x_ref, mod_ref, g_ref, w32_ref, c_ref, aw0_ref, aw1_ref, ab0_ref, ab1_ref, z_ref, mod_late_ref,
                 w_ref):
    _cast_weight_once(w32_ref, w_ref)
    first = jnp.logical_and(pl.program_id(0) == 0, pl.program_id(1) == 0)
    _ada_accumulate(c_ref, [aw0_ref, aw1_ref], [ab0_ref, ab1_ref], mod_late_ref, first)
    sub = INPROJ_TILE // INPROJ_SPLIT
    n_chunk = INPROJ_COLS
    for part in range(INPROJ_SPLIT):
        rows = slice(part * sub, (part + 1) * sub)
        h = _norm_mod(x_ref[0, rows, :], g_ref[...], mod_ref[0, 1:2, :], mod_ref[0, 0:1, :]).astype(BF16)
        for n in range(IN_W // n_chunk):
            cols = slice(n * n_chunk, (n + 1) * n_chunk)
            z_ref[0, rows, cols] = _dot(h, w_ref[:, cols]).astype(BF16)


def _inproj(x, mod_early, g1, w_in, c_lanes, ada_w, ada_b):
    b, s, d = x.shape
    tiles = s // INPROJ_TILE
    n_late = N_MOD - N_MOD_EARLY
    pair = 2 * d
    assert N_MOD_EARLY * d == pair and n_late * d == 2 * pair
    slab_rows = d // (b * tiles)
    assert slab_rows * b * tiles == d and slab_rows % 8 == 0
    slab = lambda bi, t: bi * tiles + t
    return pl.pallas_call(
        _inproj_body,
        grid=(b, tiles),
        in_specs=[
            pl.BlockSpec((1, INPROJ_TILE, d), lambda bi, t: (bi, t, 0)),
            pl.BlockSpec((1, N_MOD_EARLY, d), lambda bi, t: (bi, 0, 0)),
            pl.BlockSpec((1, d), lambda bi, t: (0, 0)),
            pl.BlockSpec((d, IN_W), lambda bi, t: (0, 0)),
            pl.BlockSpec((b, slab_rows, c_lanes.shape[2]), lambda bi, t: (0, slab(bi, t), 0)),
            pl.BlockSpec((slab_rows, pair), lambda bi, t: (slab(bi, t), 1)),
            pl.BlockSpec((slab_rows, pair), lambda bi, t: (slab(bi, t), 2)),
            pl.BlockSpec((1, pair), lambda bi, t: (0, 1)),
            pl.BlockSpec((1, pair), lambda bi, t: (0, 2)),
        ],
        out_specs=[
            pl.BlockSpec((1, INPROJ_TILE, IN_W), lambda bi, t: (bi, t, 0)),
            pl.BlockSpec((b, n_late, d), lambda bi, t: (0, 0, 0)),
        ],
        out_shape=[
            jax.ShapeDtypeStruct((b, s, IN_W), BF16),
            jax.ShapeDtypeStruct((b, n_late, d), F32),
        ],
        scratch_shapes=[pltpu.VMEM((d, IN_W), BF16)],
        compiler_params=pltpu.CompilerParams(
            dimension_semantics=("arbitrary", "arbitrary"), vmem_limit_bytes=VMEM_LIMIT),
        name="inproj",
    )(x, mod_early, g1, w_in, c_lanes, ada_w, ada_w, ada_b, ada_b)


def _t5_bucket_np(dist):
    max_exact = NUM_BUCKETS // 2
    n = np.maximum(dist, 0)
    nf = np.maximum(n, 1).astype(np.float64)
    large = max_exact + (np.log(nf / max_exact) / math.log(MAX_DISTANCE / max_exact)
                         * (NUM_BUCKETS - max_exact)).astype(np.int32)
    large = np.minimum(large, NUM_BUCKETS - 1)
    return np.where(n < max_exact, n, large).astype(np.int32)


BIAS_ROW_W = 4 * MOBA_BLOCK


def _bucket_row():
    d = np.arange(BIAS_ROW_W) - MOBA_BLOCK
    return np.where(d >= 0, _t5_bucket_np(d), -1).astype(np.int32)[None, :]


def _biasprep_body(rb_ref, bkt_ref, o_ref):
    bkt = bkt_ref[...]
    for h in range(ATTN_HEADS):
        far = rb_ref[NUM_BUCKETS - 1, h]
        row = jnp.where(bkt < 0, -MASK_BIG, 0.0).astype(F32)
        for bk in range(NUM_BUCKETS):
            row = jnp.where(bkt == bk, (rb_ref[bk, h] - far) * LOG2E, row)
        rows = jnp.broadcast_to(row, (MOBA_BLOCK, BIAS_ROW_W))
        rolled = pltpu.roll(rows, 0, 1, stride=1, stride_axis=0)
        o_ref[h, 0] = rolled[:, MOBA_BLOCK:2 * MOBA_BLOCK]
        o_ref[h, 1] = rolled[:, 2 * MOBA_BLOCK:3 * MOBA_BLOCK]


def _biasprep(rel_bias):
    bkt = jnp.asarray(_bucket_row())
    return pl.pallas_call(
        _biasprep_body,
        grid=(1,),
        in_specs=[
            pl.BlockSpec(memory_space=pltpu.SMEM),
            pl.BlockSpec((1, BIAS_ROW_W), lambda n: (0, 0)),
        ],
        out_specs=pl.BlockSpec((ATTN_HEADS, 2, MOBA_BLOCK, MOBA_BLOCK), lambda n: (0, 0, 0, 0)),
        out_shape=jax.ShapeDtypeStruct((ATTN_HEADS, 2, MOBA_BLOCK, MOBA_BLOCK), F32),
        compiler_params=pltpu.CompilerParams(dimension_semantics=("arbitrary",)),
        name="biasprep",
    )(rel_bias, bkt)


PREP_UNROLL = 8
ITEM_FIELDS = 5
PIPE_UNROLL = 16


def _far_items(n_blocks):
    never = n_blocks - 1
    items = []
    for i in range(n_blocks):
        n_far = max(i - 1, 0)
        for j0 in range(0, n_far, 2):
            items.append((i, j0, j0, j0 + 1, j0 + 1) if j0 + 1 < n_far else (i, j0, j0, j0, never))
    dummy = (n_blocks - 1, 0, never, 0, never)
    items.extend([dummy] * (-len(items) % PIPE_UNROLL))
    items.append(dummy)
    return np.asarray(items, np.int32)


def _attn_body(tbl_ref, zq_ref, zk_ref, zv_ref, gq_ref, gk_ref, bias_ref, wup32_ref, wdn32_ref,
               o_ref, wup_ref, wdn_ref,
               kn_ref, vta_ref, w_ref, pen_ref, acc_ref, m_ref, s0_ref, s1_ref, *, n_blocks, n_items):
    blk = MOBA_BLOCK
    heads = range(HEADS_PER_STEP)

    wup_ref[...] = wup32_ref[...].astype(BF16)
    wdn_ref[...] = wdn32_ref[...].astype(BF16)

    def rows_of(j):
        return pl.ds(pl.multiple_of(j * blk, blk), blk)

    blk_rows = lax.broadcasted_iota(jnp.int32, (n_blocks, PAIR_W), 0)
    ones = jnp.ones((V_ROWS - HEAD_DIM, blk), BF16)
    same_head = (lax.broadcasted_iota(jnp.int32, (PAIR_W, PAIR_W), 0) // HEAD_DIM
                 == lax.broadcasted_iota(jnp.int32, (PAIR_W, PAIR_W), 1) // HEAD_DIM)
    head_sum = jnp.where(same_head, 1.0, 0.0).astype(BF16)

    def kv_prep(i, kmean):
        kraw = zk_ref[0, rows_of(i), :].astype(F32)
        sq_hi, sq_lo = _split_bf16(kraw * kraw)
        ss = _dot(sq_hi, head_sum) + _dot(sq_lo, head_sum)
        rinv = lax.rsqrt(ss * (1.0 / HEAD_DIM) + EPS)
        kn = kraw * rinv * gk_ref[...]
        kn_ref[rows_of(i), :] = kn.astype(BF16)
        vt = zv_ref[0, rows_of(i), :].astype(F32).T
        for hd in heads:
            vta_ref[i, hd, 0:HEAD_DIM, :] = vt[hd * HEAD_DIM:(hd + 1) * HEAD_DIM].astype(BF16)
            vta_ref[i, hd, HEAD_DIM:V_ROWS, :] = ones
        return jnp.where(blk_rows == i, jnp.mean(kn, axis=0, keepdims=True), kmean)

    kmean = lax.fori_loop(0, n_blocks, kv_prep, jnp.zeros((n_blocks, PAIR_W), F32), unroll=PREP_UNROLL)

    blk_ids = lax.broadcasted_iota(jnp.int32, (n_blocks, blk), 0)
    zeros_half = jnp.zeros((HEAD_DIM, blk), F32)

    def q_prep(i, _):
        qt = zq_ref[0, rows_of(i), :].astype(F32).T
        qsq = qt * qt
        for hd in heads:
            rows = slice(hd * HEAD_DIM, (hd + 1) * HEAD_DIM)
            ss = jnp.sum(qsq[rows], axis=0, keepdims=True)
            r = lax.rsqrt(ss * (1.0 / HEAD_DIM) + EPS)
            qn = qt[rows] * r * gq_ref[rows, :] * (HEAD_DIM ** -0.5 * LOG2E)
            parts = [zeros_half] * HEADS_PER_STEP
            parts[hd] = qn
            w = jnp.concatenate(parts, axis=0)
            w_ref[i, hd] = w.astype(BF16)
            gate = jnp.where(blk_ids < i, _dot3(kmean, w), -jnp.inf)
            sel = jnp.zeros((n_blocks, blk), jnp.bool_)
            for _ in range(MOBA_TOPK):
                mx = jnp.max(gate, axis=0, keepdims=True)
                cand = jnp.where(gate == mx, blk_ids, n_blocks)
                pick = (blk_ids == jnp.min(cand, axis=0, keepdims=True)) & (mx > -jnp.inf)
                sel = sel | pick
                gate = jnp.where(pick, -jnp.inf, gate)
            pen_ref[hd, i] = jnp.where(sel, 0.0, -MASK_BIG).astype(F32)
        return 0

    lax.fori_loop(0, n_blocks, q_prep, 0, unroll=PREP_UNROLL)

    def score(i, blocks, s_ref, biased):
        keys = [kn_ref[rows_of(j), :] for j, _ in blocks]
        gmax = []
        for hd in heads:
            w = w_ref[i, hd]
            tmax = []
            for u, (_, jpen) in enumerate(blocks):
                s = _dot(keys[u], w)
                if biased:
                    s = s + bias_ref[hd, u]
                s_ref[hd, u] = s
                t = jnp.max(s, axis=0, keepdims=True)
                tmax.append(t if jpen is None else t + pen_ref[hd, i, pl.ds(jpen, 1), :])
            gmax.append(functools.reduce(jnp.maximum, tmax))
        return tuple(gmax)

    def attend(i, blocks, s_ref, gmax, first_visit):
        for hd in heads:
            if first_visit:
                m_new = gmax[hd]
                acc = None
            else:
                m_old = m_ref[i, hd]
                m_new = jnp.maximum(m_old, gmax[hd])
                acc = acc_ref[i, hd] * jnp.exp2(m_old - m_new)
            for u, (j, jpen) in enumerate(blocks):
                shift = m_new if jpen is None else m_new - pen_ref[hd, i, pl.ds(jpen, 1), :]
                p = jnp.exp2(s_ref[hd, u] - shift).astype(BF16)
                pv = _dot(vta_ref[j, hd], p)
                acc = pv if acc is None else acc + pv
            acc_ref[i, hd] = acc
            m_ref[i, hd] = m_new

    def near_blocks(i):
        jp = jnp.maximum(i - 1, 0)
        return ((i, None), (jp, jp))

    slots = (s0_ref, s1_ref)

    def near_body(t, g):
        for u in range(PIPE_UNROLL):
            i_cur = PIPE_UNROLL * t + u
            i_nxt = jnp.minimum(i_cur + 1, n_blocks - 1)
            g_nxt = score(i_nxt, near_blocks(i_nxt), slots[(u + 1) % 2], True)
            attend(i_cur, near_blocks(i_cur), slots[u % 2], g, True)
            g = g_nxt
        return g

    lax.fori_loop(0, n_blocks // PIPE_UNROLL, near_body, score(0, near_blocks(0), s0_ref, True))

    def item(k):
        base = k * ITEM_FIELDS
        i = tbl_ref[base]
        return i, ((tbl_ref[base + 1], tbl_ref[base + 2]), (tbl_ref[base + 3], tbl_ref[base + 4]))

    def far_body(t, g):
        for u in range(PIPE_UNROLL):
            k = PIPE_UNROLL * t + u
            g_nxt = score(*item(k + 1), slots[(u + 1) % 2], False)
            attend(*item(k), slots[u % 2], g, False)
            g = g_nxt
        return g

    lax.fori_loop(0, n_items // PIPE_UNROLL, far_body, score(*item(0), s0_ref, False))

    def finish(i, _):
        outs = []
        for hd in heads:
            acc = acc_ref[i, hd]
            outs.append(acc[0:HEAD_DIM] / acc[HEAD_DIM:HEAD_DIM + 1])
        o_ref[0, rows_of(i), :] = jnp.concatenate(outs, axis=0).T.astype(BF16)
        return 0

    lax.fori_loop(0, n_blocks, finish, 0, unroll=PREP_UNROLL)


def _attention(z, gq_t, gk_row, bias, w_up, w_down):
    b, s, _ = z.shape
    n_blocks = s // MOBA_BLOCK
    assert n_blocks % PIPE_UNROLL == 0 and PIPE_UNROLL % 2 == 0
    blk = MOBA_BLOCK
    items = _far_items(n_blocks)
    n_items = items.shape[0] - 1
    n_steps = b * N_PAIRS
    up_rows, dn_rows = w_up.shape[0] // n_steps, w_down.shape[0] // n_steps
    assert up_rows * n_steps == w_up.shape[0] and dn_rows * n_steps == w_down.shape[0]
    assert up_rows % BF16_SUBLANES == 0 and dn_rows % BF16_SUBLANES == 0

    def slab(bi, hp, tbl):
        return (bi * N_PAIRS + hp, 0)

    grid_spec = pltpu.PrefetchScalarGridSpec(
        num_scalar_prefetch=1,
        grid=(b, N_PAIRS),
        in_specs=[
            pl.BlockSpec((1, s, PAIR_W), lambda bi, hp, tbl: (bi, 0, hp)),
            pl.BlockSpec((1, s, PAIR_W), lambda bi, hp, tbl: (bi, 0, N_PAIRS + hp)),
            pl.BlockSpec((1, s, PAIR_W), lambda bi, hp, tbl: (bi, 0, 2 * N_PAIRS + hp)),
            pl.BlockSpec((PAIR_W, blk), lambda bi, hp, tbl: (0, 0)),
            pl.BlockSpec((1, PAIR_W), lambda bi, hp, tbl: (0, 0)),
            pl.BlockSpec((HEADS_PER_STEP, 2, blk, blk), lambda bi, hp, tbl: (hp, 0, 0, 0)),
            pl.BlockSpec((up_rows, w_up.shape[1]), slab),
            pl.BlockSpec((dn_rows, w_down.shape[1]), slab),
        ],
        out_specs=[
            pl.BlockSpec((1, s, PAIR_W), lambda bi, hp, tbl: (bi, 0, hp)),
            pl.BlockSpec((up_rows, w_up.shape[1]), slab),
            pl.BlockSpec((dn_rows, w_down.shape[1]), slab),
        ],
        scratch_shapes=[
            pltpu.VMEM((s, PAIR_W), BF16),
            pltpu.VMEM((n_blocks, HEADS_PER_STEP, V_ROWS, blk), BF16),
            pltpu.VMEM((n_blocks, HEADS_PER_STEP, PAIR_W, blk), BF16),
            pltpu.VMEM((HEADS_PER_STEP, n_blocks, n_blocks, blk), F32),
            pltpu.VMEM((n_blocks, HEADS_PER_STEP, V_ROWS, blk), F32),
            pltpu.VMEM((n_blocks, HEADS_PER_STEP, 1, blk), F32),
            pltpu.VMEM((HEADS_PER_STEP, 2, blk, blk), F32),
            pltpu.VMEM((HEADS_PER_STEP, 2, blk, blk), F32),
        ],
    )
    return pl.pallas_call(
        functools.partial(_attn_body, n_blocks=n_blocks, n_items=n_items),
        grid_spec=grid_spec,
        out_shape=[
            jax.ShapeDtypeStruct((b, s, ATTN_W), BF16),
            jax.ShapeDtypeStruct(w_up.shape, BF16),
            jax.ShapeDtypeStruct(w_down.shape, BF16),
        ],
        compiler_params=pltpu.CompilerParams(
            dimension_semantics=("arbitrary", "arbitrary"), vmem_limit_bytes=VMEM_LIMIT),
        name="attn",
    )(jnp.asarray(items.reshape(-1)), z, z, z, gq_t, gk_row, bias, w_up, w_down)


def _mix_body(x_ref, attn_ref, p_ref, ga_ref, gp_ref, mod_ref, pw32_ref, ps_ref, wba32_ref, wbp32_ref,
              wo32_ref, g2_ref, x1_ref, h2_ref, pe_ref, pw_ref, wba_ref, wbp_ref, wo_ref):
    for w32, wbf in ((pw32_ref, pw_ref), (wba32_ref, wba_ref), (wbp32_ref, wbp_ref), (wo32_ref, wo_ref)):
        _cast_weight_once(w32, wbf)
    t = pl.program_id(1)
    tm = MIX_TILE
    sub = MIX_TILE // MIX_SPLIT

    @pl.when(t == 0)
    def _():
        pe_ref[0:POOL_HALO, :] = jnp.zeros((POOL_HALO, POOL_W), F32)

    @pl.when(t > 0)
    def _():
        pe_ref[0:POOL_HALO, :] = pe_ref[tm:tm + POOL_HALO, :]

    pe_ref[POOL_HALO:POOL_HALO + tm, :] = p_ref[0].astype(F32)

    for part in range(MIX_SPLIT):
        r0 = part * sub
        rows = slice(r0, r0 + sub)
        pos = t * tm + r0 + lax.broadcasted_iota(jnp.int32, (sub, POOL_GROUP_W), 0)
        pooled = []
        for g, win in enumerate(POOL_WINDOWS):
            cols = slice(g * POOL_GROUP_W, (g + 1) * POOL_GROUP_W)
            ext = pe_ref[r0:r0 + POOL_HALO + sub, cols]
            tot = ext
            span = 1
            while span < win:
                tot = tot + pltpu.roll(tot, span, 0)
                span *= 2
            count = jnp.minimum(pos + 1, win).astype(F32)
            pooled_g = (tot[POOL_HALO:] / count - ext[POOL_HALO:]).astype(BF16)
            pooled.append(_dot(pooled_g, pw_ref[g]) * ps_ref[:, cols])
        pool = jnp.concatenate(pooled, axis=-1).astype(BF16)

        a_proj = _dot(attn_ref[0, rows, :], wba_ref[...])
        p_proj = _dot(pool, wbp_ref[...])
        merged = (_sigmoid(ga_ref[0, rows, :].astype(F32)) * a_proj
                  + _sigmoid(gp_ref[0, rows, :].astype(F32)) * p_proj).astype(BF16)
        x1 = x_ref[0, rows, :] + mod_ref[0, 0:1, :] * _dot(merged, wo_ref[...])
        x1_ref[0, rows, :] = x1
        h2_ref[0, rows, :] = _norm_mod(x1, g2_ref[...], mod_ref[0, 2:3, :], mod_ref[0, 1:2, :]).astype(BF16)


def _mix(x, attn, z, mod, pool_w, pool_scale, wba, wbp, wo, g2):
    b, s, d = x.shape
    tm = MIX_TILE
    const2 = lambda bi, t: (0, 0)
    return pl.pallas_call(
        _mix_body,
        grid=(b, s // tm),
        in_specs=[
            pl.BlockSpec((1, tm, d), lambda bi, t: (bi, t, 0)),
            pl.BlockSpec((1, tm, ATTN_W), lambda bi, t: (bi, t, 0)),
            pl.BlockSpec((1, tm, POOL_W), lambda bi, t: (bi, t, 3 * ATTN_W // POOL_W)),
            pl.BlockSpec((1, tm, d), lambda bi, t: (bi, t, (3 * ATTN_W + POOL_W) // d)),
            pl.BlockSpec((1, tm, d), lambda bi, t: (bi, t, (3 * ATTN_W + POOL_W) // d + 1)),
            pl.BlockSpec((1, N_MOD - N_MOD_EARLY, d), lambda bi, t: (bi, 0, 0)),
            pl.BlockSpec((POOL_GROUPS, POOL_GROUP_W, POOL_GROUP_W), lambda bi, t: (0, 0, 0)),
            pl.BlockSpec((1, POOL_W), const2),
            pl.BlockSpec((ATTN_W, d), const2),
            pl.BlockSpec((POOL_W, d), const2),
            pl.BlockSpec((d, d), const2),
            pl.BlockSpec((1, d), const2),
        ],
        out_specs=[
            pl.BlockSpec((1, tm, d), lambda bi, t: (bi, t, 0)),
            pl.BlockSpec((1, tm, d), lambda bi, t: (bi, t, 0)),
        ],
        out_shape=[
            jax.ShapeDtypeStruct((b, s, d), F32),
            jax.ShapeDtypeStruct((b, s, d), BF16),
        ],
        scratch_shapes=[
            pltpu.VMEM((tm + POOL_HALO, POOL_W), F32),
            pltpu.VMEM(pool_w.shape, BF16),
            pltpu.VMEM(wba.shape, BF16),
            pltpu.VMEM(wbp.shape, BF16),
            pltpu.VMEM(wo.shape, BF16),
        ],
        compiler_params=pltpu.CompilerParams(
            dimension_semantics=("arbitrary", "arbitrary"), vmem_limit_bytes=VMEM_LIMIT),
        name="mix",
    )(x, attn, z, z, z, mod, pool_w, pool_scale, wba, wbp, wo, g2)


def _ffn_body(x1_ref, h2_ref, mod_ref, wup_ref, cw_ref, cb_ref, wdn_ref, o_ref,
              hist0_ref, hist1_ref, carry_ref):
    t = pl.program_id(1)
    sub = FFN_TILE // FFN_SPLIT
    fc = FFN_CHUNK
    n_chunks = D_FF // fc
    hist = (hist0_ref, hist1_ref)

    @pl.when(t == 0)
    def _():
        carry_ref[...] = jnp.zeros_like(carry_ref)

    units = [(part, c) for part in range(FFN_SPLIT) for c in range(n_chunks)]
    h2 = [h2_ref[0, part * sub:(part + 1) * sub, :] for part in range(FFN_SPLIT)]

    def up(k):
        part, c = units[k]
        out = []
        for half in range(2):
            cols = slice(half * D_FF + c * fc, half * D_FF + (c + 1) * fc)
            hcols = slice(half * fc, (half + 1) * fc)
            r = _dot(h2[part], wup_ref[:, cols])
            hist[k % 2][0:CONV_HALO, hcols] = carry_ref[:, cols]
            hist[k % 2][CONV_HALO:CONV_HALO + sub, hcols] = r
            carry_ref[:, cols] = r[sub - CONV_HALO:sub]
            out.append(r)
        return out

    def down(acc, k, act):
        part, c = units[k]
        acc[part] = acc[part] + _dot(act, wdn_ref[c * fc:(c + 1) * fc, :])

    acc = [jnp.zeros((sub, D_MODEL), F32) for _ in range(FFN_SPLIT)]
    r_next = up(0)
    act_prev = None
    for k, (part, c) in enumerate(units):
        r_cur = r_next
        if k + 1 < len(units):
            r_next = up(k + 1)
        if act_prev is not None:
            down(acc, k - 1, act_prev)
        halves = []
        for half in range(2):
            cols = slice(half * D_FF + c * fc, half * D_FF + (c + 1) * fc)
            hcols = slice(half * fc, (half + 1) * fc)
            u = cb_ref[:, cols] + cw_ref[CONV_W - 1:CONV_W, cols] * r_cur[half]
            for tap in range(CONV_W - 1):
                back = CONV_W - 1 - tap
                u = u + cw_ref[tap:tap + 1, cols] * hist[k % 2][CONV_HALO - back:CONV_HALO - back + sub, hcols]
            halves.append(u)
        u_g, u_v = halves
        act_prev = (u_g * _sigmoid(u_g) * u_v).astype(BF16)
    down(acc, len(units) - 1, act_prev)
    for part in range(FFN_SPLIT):
        rows = slice(part * sub, (part + 1) * sub)
        o_ref[0, rows, :] = x1_ref[0, rows, :] + mod_ref[0, 3:4, :] * acc[part]


def _ffn(x1, h2, mod, wup_bf, conv_w, conv_b, wdn_bf):
    b, s, d = x1.shape
    tm = FFN_TILE
    const2 = lambda bi, t: (0, 0)
    return pl.pallas_call(
        _ffn_body,
        grid=(b, s // tm),
        in_specs=[
            pl.BlockSpec((1, tm, d), lambda bi, t: (bi, t, 0)),
            pl.BlockSpec((1, tm, d), lambda bi, t: (bi, t, 0)),
            pl.BlockSpec((1, N_MOD - N_MOD_EARLY, d), lambda bi, t: (bi, 0, 0)),
            pl.BlockSpec((d, 2 * D_FF), const2),
            pl.BlockSpec((CONV_W, 2 * D_FF), const2),
            pl.BlockSpec((1, 2 * D_FF), const2),
            pl.BlockSpec((D_FF, d), const2),
        ],
        out_specs=pl.BlockSpec((1, tm, d), lambda bi, t: (bi, t, 0)),
        out_shape=jax.ShapeDtypeStruct((b, s, d), F32),
        scratch_shapes=[
            pltpu.VMEM((tm // FFN_SPLIT + CONV_HALO, 2 * FFN_CHUNK), F32),
            pltpu.VMEM((tm // FFN_SPLIT + CONV_HALO, 2 * FFN_CHUNK), F32),
            pltpu.VMEM((CONV_HALO, 2 * D_FF), F32),
        ],
        compiler_params=pltpu.CompilerParams(
            dimension_semantics=("arbitrary", "arbitrary"), vmem_limit_bytes=VMEM_LIMIT),
        name="ffn",
    )(x1, h2, mod, wup_bf, conv_w, conv_b, wdn_bf)


def _layer(x, c_lanes, rel_bias, ada_w, ada_b, norm1_g, w_in, q_norm_g, k_norm_g, pool_w, pool_scale,
           w_branch_attn, w_branch_pool, w_out, norm2_g, w_up, conv_w, conv_b, w_down):
    b, s, d = x.shape
    mod_early = _ada_early(c_lanes, ada_w, ada_b[None, :])
    z, mod = _inproj(x, mod_early, norm1_g[None, :], w_in, c_lanes, ada_w, ada_b[None, :])
    gq_t = jnp.broadcast_to(jnp.tile(q_norm_g, HEADS_PER_STEP)[:, None], (PAIR_W, MOBA_BLOCK))
    gk_row = jnp.tile(k_norm_g, HEADS_PER_STEP)[None, :]
    attn, wup_bf, wdn_bf = _attention(z, gq_t, gk_row, _biasprep(rel_bias), w_up, w_down)
    x1, h2 = _mix(x, attn, z, mod, pool_w, pool_scale[None, :], w_branch_attn, w_branch_pool, w_out,
                  norm2_g[None, :])
    return _ffn(x1, h2, mod, wup_bf, conv_w, conv_b[None, :], wdn_bf)


def kernel(x, c, ada_w, ada_b, norm1_g, w_in, q_norm_g, k_norm_g, rel_bias, pool_w, pool_scale,
           w_branch_attn, w_branch_pool, w_out, norm2_g, w_up, conv_w, conv_b, w_down):
    b, s, d = x.shape
    assert d == D_MODEL and w_in.shape[-1] == IN_W
    assert all(s % tile == 0 for tile in (MOBA_BLOCK * PIPE_UNROLL, INPROJ_TILE, MIX_TILE, FFN_TILE))
    c_lanes = jnp.broadcast_to(c[:, :, None], (b, d, LANES))
    for l in range(ada_w.shape[0]):
        x = _layer(x, c_lanes, rel_bias, ada_w[l], ada_b[l], norm1_g[l], w_in[l], q_norm_g[l],
                   k_norm_g[l], pool_w[l], pool_scale[l], w_branch_attn[l], w_branch_pool[l],
                   w_out[l], norm2_g[l], w_up[l], conv_w[l], conv_b[l], w_down[l])
    return x
```

```python
import functools
import math

import numpy as np
import jax
import jax.numpy as jnp
from jax import lax
from jax.experimental import pallas as pl
from jax.experimental.pallas import tpu as pltpu

F32 = jnp.float32
BF16 = jnp.bfloat16

D_MODEL = 1024
ATTN_HEADS = 8
HEAD_DIM = 64
ATTN_W = ATTN_HEADS * HEAD_DIM
MOBA_BLOCK = 256
MOBA_TOPK = 3
POOL_GROUPS = 4
POOL_GROUP_W = 128
POOL_W = POOL_GROUPS * POOL_GROUP_W
POOL_WINDOWS = (2, 4, 8, 16)
NUM_BUCKETS = 32
MAX_DISTANCE = 128
D_FF = 2816
CONV_W = 3
EPS = 1e-6
IN_W = 3 * ATTN_W + POOL_W + 2 * D_MODEL

LANES = 128
BF16_SUBLANES = 16
HEADS_PER_STEP = LANES // HEAD_DIM
PAIR_W = HEADS_PER_STEP * HEAD_DIM
N_PAIRS = ATTN_HEADS // HEADS_PER_STEP
V_ROWS = HEAD_DIM + BF16_SUBLANES
LOG2E = math.log2(math.e)
MASK_BIG = 1e30
ADA_ROWS = 256
INPROJ_TILE = 1024
INPROJ_COLS = 1024
INPROJ_SPLIT = 4
POOL_HALO = 16
MIX_TILE = 512
MIX_SPLIT = 2
FFN_CHUNK = 256
FFN_TILE = 512
FFN_SPLIT = 2
CONV_HALO = 8
VMEM_LIMIT = 56 * 1024 * 1024


def _split_bf16(a):
    hi = a.astype(BF16)
    lo = (a - hi.astype(F32)).astype(BF16)
    return hi, lo


def _dot(a, b):
    return jnp.dot(a, b, preferred_element_type=F32)


def _dot3(a, b):
    ah, al = _split_bf16(a)
    bh, bl = _split_bf16(b)
    return _dot(ah, bh) + _dot(al, bh) + _dot(ah, bl)


def _sigmoid(v):
    return 1.0 / (1.0 + jnp.exp2(v * (-LOG2E)))


def _cast_weight_once(src_ref, dst_ref, col_chunk=4 * LANES):
    first = functools.reduce(jnp.logical_and, [pl.program_id(a) == 0 for a in range(2)])

    @pl.when(first)
    def _():
        n = src_ref.shape[-1]
        for c0 in range(0, n, col_chunk):
            dst_ref[..., c0:c0 + col_chunk] = src_ref[..., c0:c0 + col_chunk].astype(BF16)


N_MOD = 6
N_MOD_EARLY = 2


def _ada_accumulate(c_ref, w_refs, b_refs, o_ref, first):
    n_batch, _, lanes = c_ref.shape
    dm = o_ref.shape[2]
    vectors = [(w_ref, b_ref, j) for w_ref, b_ref in zip(w_refs, b_refs) for j in range(w_ref.shape[1] // dm)]

    @pl.when(first)
    def _():
        for v, (_, b_ref, j) in enumerate(vectors):
            o_ref[:, v, :] = jnp.broadcast_to(b_ref[:, j * dm:(j + 1) * dm], (n_batch, dm))

    for bi in range(n_batch):
        c = c_ref[bi]
        act = c * _sigmoid(c)
        for v, (w_ref, _, j) in enumerate(vectors):
            cols = [jnp.sum(act * w_ref[:, j * dm + g * lanes:j * dm + (g + 1) * lanes], axis=0, keepdims=True)
                    for g in range(dm // lanes)]
            o_ref[bi, v:v + 1, :] += jnp.concatenate(cols, axis=-1)


def _ada_body(c_ref, w_ref, b_ref, o_ref):
    _ada_accumulate(c_ref, [w_ref], [b_ref], o_ref, pl.program_id(0) == 0)


def _ada_early(c_lanes, ada_w, ada_b):
    n_batch, d, lanes = c_lanes.shape
    n = N_MOD_EARLY * d
    tk = ADA_ROWS
    return pl.pallas_call(
        _ada_body,
        grid=(d // tk,),
        in_specs=[
            pl.BlockSpec((n_batch, tk, lanes), lambda k: (0, k, 0)),
            pl.BlockSpec((tk, n), lambda k: (k, 0)),
            pl.BlockSpec((1, n), lambda k: (0, 0)),
        ],
        out_specs=pl.BlockSpec((n_batch, N_MOD_EARLY, d), lambda k: (0, 0, 0)),
        out_shape=jax.ShapeDtypeStruct((n_batch, N_MOD_EARLY, d), F32),
        compiler_params=pltpu.CompilerParams(
            dimension_semantics=("arbitrary",), vmem_limit_bytes=VMEM_LIMIT),
        name="ada",
    )(c_lanes, ada_w, ada_b)


def _norm_mod(x, gain, scale, shift):
    ms = jnp.mean(x * x, axis=-1, keepdims=True)
    return x * lax.rsqrt(ms + EPS) * (gain * (1.0 + scale)) + shift


def _inproj_body(x_ref, mod_ref, g_ref, w32_ref, c_ref, aw0_ref, aw1_ref, ab0_ref, ab1_ref, z_ref, mod_late_ref,
                 w_ref):
    _cast_weight_once(w32_ref, w_ref)
    first = jnp.logical_and(pl.program_id(0) == 0, pl.program_id(1) == 0)
    _ada_accumulate(c_ref, [aw0_ref, aw1_ref], [ab0_ref, ab1_ref], mod_late_ref, first)
    sub = INPROJ_TILE // INPROJ_SPLIT
    n_chunk = INPROJ_COLS
    for part in range(INPROJ_SPLIT):
        rows = slice(part * sub, (part + 1) * sub)
        h = _norm_mod(x_ref[0, rows, :], g_ref[...], mod_ref[0, 1:2, :], mod_ref[0, 0:1, :]).astype(BF16)
        for n in range(IN_W // n_chunk):
            cols = slice(n * n_chunk, (n + 1) * n_chunk)
            z_ref[0, rows, cols] = _dot(h, w_ref[:, cols]).astype(BF16)


def _inproj(x, mod_early, g1, w_in, c_lanes, ada_w, ada_b):
    b, s, d = x.shape
    tiles = s // INPROJ_TILE
    n_late = N_MOD - N_MOD_EARLY
    pair = 2 * d
    assert N_MOD_EARLY * d == pair and n_late * d == 2 * pair
    slab_rows = d // (b * tiles)
    assert slab_rows * b * tiles == d and slab_rows % 8 == 0
    slab = lambda bi, t: bi * tiles + t
    return pl.pallas_call(
        _inproj_body,
        grid=(b, tiles),
        in_specs=[
            pl.BlockSpec((1, INPROJ_TILE, d), lambda bi, t: (bi, t, 0)),
            pl.BlockSpec((1, N_MOD_EARLY, d), lambda bi, t: (bi, 0, 0)),
            pl.BlockSpec((1, d), lambda bi, t: (0, 0)),
            pl.BlockSpec((d, IN_W), lambda bi, t: (0, 0)),
            pl.BlockSpec((b, slab_rows, c_lanes.shape[2]), lambda bi, t: (0, slab(bi, t), 0)),
            pl.BlockSpec((slab_rows, pair), lambda bi, t: (slab(bi, t), 1)),
            pl.BlockSpec((slab_rows, pair), lambda bi, t: (slab(bi, t), 2)),
            pl.BlockSpec((1, pair), lambda bi, t: (0, 1)),
            pl.BlockSpec((1, pair), lambda bi, t: (0, 2)),
        ],
        out_specs=[
            pl.BlockSpec((1, INPROJ_TILE, IN_W), lambda bi, t: (bi, t, 0)),
            pl.BlockSpec((b, n_late, d), lambda bi, t: (0, 0, 0)),
        ],
        out_shape=[
            jax.ShapeDtypeStruct((b, s, IN_W), BF16),
            jax.ShapeDtypeStruct((b, n_late, d), F32),
        ],
        scratch_shapes=[pltpu.VMEM((d, IN_W), BF16)],
        compiler_params=pltpu.CompilerParams(
            dimension_semantics=("arbitrary", "arbitrary"), vmem_limit_bytes=VMEM_LIMIT),
        name="inproj",
    )(x, mod_early, g1, w_in, c_lanes, ada_w, ada_w, ada_b, ada_b)


def _t5_bucket_np(dist):
    max_exact = NUM_BUCKETS // 2
    n = np.maximum(dist, 0)
    nf = np.maximum(n, 1).astype(np.float64)
    large = max_exact + (np.log(nf / max_exact) / math.log(MAX_DISTANCE / max_exact)
                         * (NUM_BUCKETS - max_exact)).astype(np.int32)
    large = np.minimum(large, NUM_BUCKETS - 1)
    return np.where(n < max_exact, n, large).astype(np.int32)


BIAS_ROW_W = 4 * MOBA_BLOCK


def _bucket_row():
    d = np.arange(BIAS_ROW_W) - MOBA_BLOCK
    return np.where(d >= 0, _t5_bucket_np(d), -1).astype(np.int32)[None, :]


def _biasprep_body(rb_ref, bkt_ref, o_ref):
    bkt = bkt_ref[...]
    for h in range(ATTN_HEADS):
        far = rb_ref[NUM_BUCKETS - 1, h]
        row = jnp.where(bkt < 0, -MASK_BIG, 0.0).astype(F32)
        for bk in range(NUM_BUCKETS):
            row = jnp.where(bkt == bk, (rb_ref[bk, h] - far) * LOG2E, row)
        rows = jnp.broadcast_to(row, (MOBA_BLOCK, BIAS_ROW_W))
        rolled = pltpu.roll(rows, 0, 1, stride=1, stride_axis=0)
        o_ref[h, 0] = rolled[:, MOBA_BLOCK:2 * MOBA_BLOCK]
        o_ref[h, 1] = rolled[:, 2 * MOBA_BLOCK:3 * MOBA_BLOCK]


def _biasprep(rel_bias):
    bkt = jnp.asarray(_bucket_row())
    return pl.pallas_call(
        _biasprep_body,
        grid=(1,),
        in_specs=[
            pl.BlockSpec(memory_space=pltpu.SMEM),
            pl.BlockSpec((1, BIAS_ROW_W), lambda n: (0, 0)),
        ],
        out_specs=pl.BlockSpec((ATTN_HEADS, 2, MOBA_BLOCK, MOBA_BLOCK), lambda n: (0, 0, 0, 0)),
        out_shape=jax.ShapeDtypeStruct((ATTN_HEADS, 2, MOBA_BLOCK, MOBA_BLOCK), F32),
        compiler_params=pltpu.CompilerParams(dimension_semantics=("arbitrary",)),
        name="biasprep",
    )(rel_bias, bkt)


PREP_UNROLL = 8
ITEM_FIELDS = 5
PIPE_UNROLL = 16


def _far_items(n_blocks):
    never = n_blocks - 1
    per_block = []
    for i in range(n_blocks):
        n_far = max(i - 1, 0)
        per_block.append([(i, j0, j0, j0 + 1, j0 + 1) if j0 + 1 < n_far else (i, j0, j0, j0, never)
                          for j0 in range(0, n_far, 2)])
    items, last = [], None
    while any(per_block):
        pick = max((i for i in range(n_blocks) if per_block[i] and i != last),
                   key=lambda i: len(per_block[i]), default=last)
        items.append(per_block[pick].pop(0))
        last = pick
    dummy = (n_blocks - 1, 0, never, 0, never)
    items.extend([dummy] * (-len(items) % PIPE_UNROLL))
    items.append(dummy)
    return np.asarray(items, np.int32)


def _attn_body(tbl_ref, zq_ref, zk_ref, zv_ref, gq_ref, gk_ref, bias_ref, wup32_ref, wdn32_ref,
               o_ref, wup_ref, wdn_ref,
               kn_ref, vta_ref, w_ref, pen_ref, acc_ref, m_ref, s0_ref, s1_ref, *, n_blocks, n_items):
    blk = MOBA_BLOCK
    heads = range(HEADS_PER_STEP)

    wup_ref[...] = wup32_ref[...].astype(BF16)
    wdn_ref[...] = wdn32_ref[...].astype(BF16)

    def rows_of(j):
        return pl.ds(pl.multiple_of(j * blk, blk), blk)

    blk_rows = lax.broadcasted_iota(jnp.int32, (n_blocks, PAIR_W), 0)
    ones = jnp.ones((V_ROWS - HEAD_DIM, blk), BF16)
    same_head = (lax.broadcasted_iota(jnp.int32, (PAIR_W, PAIR_W), 0) // HEAD_DIM
                 == lax.broadcasted_iota(jnp.int32, (PAIR_W, PAIR_W), 1) // HEAD_DIM)
    head_sum = jnp.where(same_head, 1.0, 0.0).astype(BF16)

    def kv_prep(i, kmean):
        kraw = zk_ref[0, rows_of(i), :].astype(F32)
        sq_hi, sq_lo = _split_bf16(kraw * kraw)
        ss = _dot(sq_hi, head_sum) + _dot(sq_lo, head_sum)
        rinv = lax.rsqrt(ss * (1.0 / HEAD_DIM) + EPS)
        kn = kraw * rinv * gk_ref[...]
        kn_ref[rows_of(i), :] = kn.astype(BF16)
        vt = zv_ref[0, rows_of(i), :].astype(F32).T
        for hd in heads:
            vta_ref[i, hd, 0:HEAD_DIM, :] = vt[hd * HEAD_DIM:(hd + 1) * HEAD_DIM].astype(BF16)
            vta_ref[i, hd, HEAD_DIM:V_ROWS, :] = ones
        return jnp.where(blk_rows == i, jnp.mean(kn, axis=0, keepdims=True), kmean)

    kmean = lax.fori_loop(0, n_blocks, kv_prep, jnp.zeros((n_blocks, PAIR_W), F32), unroll=PREP_UNROLL)

    blk_ids = lax.broadcasted_iota(jnp.int32, (n_blocks, blk), 0)
    zeros_half = jnp.zeros((HEAD_DIM, blk), F32)

    def q_prep(i, _):
        qt = zq_ref[0, rows_of(i), :].astype(F32).T
        qsq = qt * qt
        for hd in heads:
            rows = slice(hd * HEAD_DIM, (hd + 1) * HEAD_DIM)
            ss = jnp.sum(qsq[rows], axis=0, keepdims=True)
            r = lax.rsqrt(ss * (1.0 / HEAD_DIM) + EPS)
            qn = qt[rows] * r * gq_ref[rows, :] * (HEAD_DIM ** -0.5 * LOG2E)
            parts = [zeros_half] * HEADS_PER_STEP
            parts[hd] = qn
            w = jnp.concatenate(parts, axis=0)
            w_ref[i, hd] = w.astype(BF16)
            gate = jnp.where(blk_ids < i, _dot3(kmean, w), -jnp.inf)
            sel = jnp.zeros((n_blocks, blk), jnp.bool_)
            for _ in range(MOBA_TOPK):
                mx = jnp.max(gate, axis=0, keepdims=True)
                cand = jnp.where(gate == mx, blk_ids, n_blocks)
                pick = (blk_ids == jnp.min(cand, axis=0, keepdims=True)) & (mx > -jnp.inf)
                sel = sel | pick
                gate = jnp.where(pick, -jnp.inf, gate)
            pen_ref[hd, i] = jnp.where(sel, 0.0, -MASK_BIG).astype(F32)
        return 0

    lax.fori_loop(0, n_blocks, q_prep, 0, unroll=PREP_UNROLL)

    def score(i, blocks, s_ref, biased):
        keys = [kn_ref[rows_of(j), :] for j, _ in blocks]
        gmax = []
        for hd in heads:
            w = w_ref[i, hd]
            tmax = []
            for u, (_, jpen) in enumerate(blocks):
                s = _dot(keys[u], w)
                if biased:
                    s = s + bias_ref[hd, u]
                s_ref[hd, u] = s
                t = jnp.max(s, axis=0, keepdims=True)
                tmax.append(t if jpen is None else t + pen_ref[hd, i, pl.ds(jpen, 1), :])
            gmax.append(functools.reduce(jnp.maximum, tmax))
        return tuple(gmax)

    def attend(i, blocks, s_ref, gmax, first_visit):
        for hd in heads:
            if first_visit:
                m_new = gmax[hd]
                acc = None
            else:
                m_old = m_ref[i, hd]
                m_new = jnp.maximum(m_old, gmax[hd])
                acc = acc_ref[i, hd] * jnp.exp2(m_old - m_new)
            for u, (j, jpen) in enumerate(blocks):
                shift = m_new if jpen is None else m_new - pen_ref[hd, i, pl.ds(jpen, 1), :]
                p = jnp.exp2(s_ref[hd, u] - shift).astype(BF16)
                pv = _dot(vta_ref[j, hd], p)
                acc = pv if acc is None else acc + pv
            acc_ref[i, hd] = acc
            m_ref[i, hd] = m_new

    def near_blocks(i):
        jp = jnp.maximum(i - 1, 0)
        return ((i, None), (jp, jp))

    slots = (s0_ref, s1_ref)

    def near_body(t, g):
        for u in range(PIPE_UNROLL):
            i_cur = PIPE_UNROLL * t + u
            i_nxt = jnp.minimum(i_cur + 1, n_blocks - 1)
            g_nxt = score(i_nxt, near_blocks(i_nxt), slots[(u + 1) % 2], True)
            attend(i_cur, near_blocks(i_cur), slots[u % 2], g, True)
            g = g_nxt
        return g

    lax.fori_loop(0, n_blocks // PIPE_UNROLL, near_body, score(0, near_blocks(0), s0_ref, True))

    def item(k):
        base = k * ITEM_FIELDS
        i = tbl_ref[base]
        return i, ((tbl_ref[base + 1], tbl_ref[base + 2]), (tbl_ref[base + 3], tbl_ref[base + 4]))

    def far_body(t, g):
        for u in range(PIPE_UNROLL):
            k = PIPE_UNROLL * t + u
            g_nxt = score(*item(k + 1), slots[(u + 1) % 2], False)
            attend(*item(k), slots[u % 2], g, False)
            g = g_nxt
        return g

    lax.fori_loop(0, n_items // PIPE_UNROLL, far_body, score(*item(0), s0_ref, False))

    def finish(i, _):
        outs = []
        for hd in heads:
            acc = acc_ref[i, hd]
            outs.append(acc[0:HEAD_DIM] / acc[HEAD_DIM:HEAD_DIM + 1])
        o_ref[0, rows_of(i), :] = jnp.concatenate(outs, axis=0).T.astype(BF16)
        return 0

    lax.fori_loop(0, n_blocks, finish, 0, unroll=PREP_UNROLL)


def _attention(z, gq_t, gk_row, bias, w_up, w_down):
    b, s, _ = z.shape
    n_blocks = s // MOBA_BLOCK
    assert n_blocks % PIPE_UNROLL == 0 and PIPE_UNROLL % 2 == 0
    blk = MOBA_BLOCK
    items = _far_items(n_blocks)
    n_items = items.shape[0] - 1
    n_steps = b * N_PAIRS
    up_rows, dn_rows = w_up.shape[0] // n_steps, w_down.shape[0] // n_steps
    assert up_rows * n_steps == w_up.shape[0] and dn_rows * n_steps == w_down.shape[0]
    assert up_rows % BF16_SUBLANES == 0 and dn_rows % BF16_SUBLANES == 0

    def slab(bi, hp, tbl):
        return (bi * N_PAIRS + hp, 0)

    grid_spec = pltpu.PrefetchScalarGridSpec(
        num_scalar_prefetch=1,
        grid=(b, N_PAIRS),
        in_specs=[
            pl.BlockSpec((1, s, PAIR_W), lambda bi, hp, tbl: (bi, 0, hp)),
            pl.BlockSpec((1, s, PAIR_W), lambda bi, hp, tbl: (bi, 0, N_PAIRS + hp)),
            pl.BlockSpec((1, s, PAIR_W), lambda bi, hp, tbl: (bi, 0, 2 * N_PAIRS + hp)),
            pl.BlockSpec((PAIR_W, blk), lambda bi, hp, tbl: (0, 0)),
            pl.BlockSpec((1, PAIR_W), lambda bi, hp, tbl: (0, 0)),
            pl.BlockSpec((HEADS_PER_STEP, 2, blk, blk), lambda bi, hp, tbl: (hp, 0, 0, 0)),
            pl.BlockSpec((up_rows, w_up.shape[1]), slab),
            pl.BlockSpec((dn_rows, w_down.shape[1]), slab),
        ],
        out_specs=[
            pl.BlockSpec((1, s, PAIR_W), lambda bi, hp, tbl: (bi, 0, hp)),
            pl.BlockSpec((up_rows, w_up.shape[1]), slab),
            pl.BlockSpec((dn_rows, w_down.shape[1]), slab),
        ],
        scratch_shapes=[
            pltpu.VMEM((s, PAIR_W), BF16),
            pltpu.VMEM((n_blocks, HEADS_PER_STEP, V_ROWS, blk), BF16),
            pltpu.VMEM((n_blocks, HEADS_PER_STEP, PAIR_W, blk), BF16),
            pltpu.VMEM((HEADS_PER_STEP, n_blocks, n_blocks, blk), F32),
            pltpu.VMEM((n_blocks, HEADS_PER_STEP, V_ROWS, blk), F32),
            pltpu.VMEM((n_blocks, HEADS_PER_STEP, 1, blk), F32),
            pltpu.VMEM((HEADS_PER_STEP, 2, blk, blk), F32),
            pltpu.VMEM((HEADS_PER_STEP, 2, blk, blk), F32),
        ],
    )
    return pl.pallas_call(
        functools.partial(_attn_body, n_blocks=n_blocks, n_items=n_items),
        grid_spec=grid_spec,
        out_shape=[
            jax.ShapeDtypeStruct((b, s, ATTN_W), BF16),
            jax.ShapeDtypeStruct(w_up.shape, BF16),
            jax.ShapeDtypeStruct(w_down.shape, BF16),
        ],
        compiler_params=pltpu.CompilerParams(
            dimension_semantics=("arbitrary", "arbitrary"), vmem_limit_bytes=VMEM_LIMIT),
        name="attn",
    )(jnp.asarray(items.reshape(-1)), z, z, z, gq_t, gk_row, bias, w_up, w_down)


def _mix_body(x_ref, attn_ref, p_ref, ga_ref, gp_ref, mod_ref, pw32_ref, ps_ref, wba32_ref, wbp32_ref,
              wo32_ref, g2_ref, x1_ref, h2_ref, pe_ref, pw_ref, wba_ref, wbp_ref, wo_ref):
    for w32, wbf in ((pw32_ref, pw_ref), (wba32_ref, wba_ref), (wbp32_ref, wbp_ref), (wo32_ref, wo_ref)):
        _cast_weight_once(w32, wbf)
    t = pl.program_id(1)
    tm = MIX_TILE
    sub = MIX_TILE // MIX_SPLIT

    @pl.when(t == 0)
    def _():
        pe_ref[0:POOL_HALO, :] = jnp.zeros((POOL_HALO, POOL_W), F32)

    @pl.when(t > 0)
    def _():
        pe_ref[0:POOL_HALO, :] = pe_ref[tm:tm + POOL_HALO, :]

    pe_ref[POOL_HALO:POOL_HALO + tm, :] = p_ref[0].astype(F32)

    for part in range(MIX_SPLIT):
        r0 = part * sub
        rows = slice(r0, r0 + sub)
        pos = t * tm + r0 + lax.broadcasted_iota(jnp.int32, (sub, POOL_GROUP_W), 0)
        pooled = []
        for g, win in enumerate(POOL_WINDOWS):
            cols = slice(g * POOL_GROUP_W, (g + 1) * POOL_GROUP_W)
            ext = pe_ref[r0:r0 + POOL_HALO + sub, cols]
            tot = ext
            span = 1
            while span < win:
                tot = tot + pltpu.roll(tot, span, 0)
                span *= 2
            count = jnp.minimum(pos + 1, win).astype(F32)
            pooled_g = (tot[POOL_HALO:] / count - ext[POOL_HALO:]).astype(BF16)
            pooled.append(_dot(pooled_g, pw_ref[g]) * ps_ref[:, cols])
        pool = jnp.concatenate(pooled, axis=-1).astype(BF16)

        a_proj = _dot(attn_ref[0, rows, :], wba_ref[...])
        p_proj = _dot(pool, wbp_ref[...])
        merged = (_sigmoid(ga_ref[0, rows, :].astype(F32)) * a_proj
                  + _sigmoid(gp_ref[0, rows, :].astype(F32)) * p_proj).astype(BF16)
        x1 = x_ref[0, rows, :] + mod_ref[0, 0:1, :] * _dot(merged, wo_ref[...])
        x1_ref[0, rows, :] = x1
        h2_ref[0, rows, :] = _norm_mod(x1, g2_ref[...], mod_ref[0, 2:3, :], mod_ref[0, 1:2, :]).astype(BF16)


def _mix(x, attn, z, mod, pool_w, pool_scale, wba, wbp, wo, g2):
    b, s, d = x.shape
    tm = MIX_TILE
    const2 = lambda bi, t: (0, 0)
    return pl.pallas_call(
        _mix_body,
        grid=(b, s // tm),
        in_specs=[
            pl.BlockSpec((1, tm, d), lambda bi, t: (bi, t, 0)),
            pl.BlockSpec((1, tm, ATTN_W), lambda bi, t: (bi, t, 0)),
            pl.BlockSpec((1, tm, POOL_W), lambda bi, t: (bi, t, 3 * ATTN_W // POOL_W)),
            pl.BlockSpec((1, tm, d), lambda bi, t: (bi, t, (3 * ATTN_W + POOL_W) // d)),
            pl.BlockSpec((1, tm, d), lambda bi, t: (bi, t, (3 * ATTN_W + POOL_W) // d + 1)),
            pl.BlockSpec((1, N_MOD - N_MOD_EARLY, d), lambda bi, t: (bi, 0, 0)),
            pl.BlockSpec((POOL_GROUPS, POOL_GROUP_W, POOL_GROUP_W), lambda bi, t: (0, 0, 0)),
            pl.BlockSpec((1, POOL_W), const2),
            pl.BlockSpec((ATTN_W, d), const2),
            pl.BlockSpec((POOL_W, d), const2),
            pl.BlockSpec((d, d), const2),
            pl.BlockSpec((1, d), const2),
        ],
        out_specs=[
            pl.BlockSpec((1, tm, d), lambda bi, t: (bi, t, 0)),
            pl.BlockSpec((1, tm, d), lambda bi, t: (bi, t, 0)),
        ],
        out_shape=[
            jax.ShapeDtypeStruct((b, s, d), F32),
            jax.ShapeDtypeStruct((b, s, d), BF16),
        ],
        scratch_shapes=[
            pltpu.VMEM((tm + POOL_HALO, POOL_W), F32),
            pltpu.VMEM(pool_w.shape, BF16),
            pltpu.VMEM(wba.shape, BF16),
            pltpu.VMEM(wbp.shape, BF16),
            pltpu.VMEM(wo.shape, BF16),
        ],
        compiler_params=pltpu.CompilerParams(
            dimension_semantics=("arbitrary", "arbitrary"), vmem_limit_bytes=VMEM_LIMIT),
        name="mix",
    )(x, attn, z, z, z, mod, pool_w, pool_scale, wba, wbp, wo, g2)


def _ffn_body(x1_ref, h2_ref, mod_ref, wup_ref, cw_ref, cb_ref, wdn_ref, o_ref,
              hist0_ref, hist1_ref, carry_ref):
    t = pl.program_id(1)
    sub = FFN_TILE // FFN_SPLIT
    fc = FFN_CHUNK
    n_chunks = D_FF // fc
    hist = (hist0_ref, hist1_ref)

    @pl.when(t == 0)
    def _():
        carry_ref[...] = jnp.zeros_like(carry_ref)

    units = [(part, c) for part in range(FFN_SPLIT) for c in range(n_chunks)]
    h2 = [h2_ref[0, part * sub:(part + 1) * sub, :] for part in range(FFN_SPLIT)]

    def up(k):
        part, c = units[k]
        out = []
        for half in range(2):
            cols = slice(half * D_FF + c * fc, half * D_FF + (c + 1) * fc)
            hcols = slice(half * fc, (half + 1) * fc)
            r = _dot(h2[part], wup_ref[:, cols])
            hist[k % 2][0:CONV_HALO, hcols] = carry_ref[:, cols]
            hist[k % 2][CONV_HALO:CONV_HALO + sub, hcols] = r
            carry_ref[:, cols] = r[sub - CONV_HALO:sub]
            out.append(r)
        return out

    def down(acc, k, act):
        part, c = units[k]
        acc[part] = acc[part] + _dot(act, wdn_ref[c * fc:(c + 1) * fc, :])

    acc = [jnp.zeros((sub, D_MODEL), F32) for _ in range(FFN_SPLIT)]
    r_next = up(0)
    act_prev = None
    for k, (part, c) in enumerate(units):
        r_cur = r_next
        if k + 1 < len(units):
            r_next = up(k + 1)
        if act_prev is not None:
            down(acc, k - 1, act_prev)
        halves = []
        for half in range(2):
            cols = slice(half * D_FF + c * fc, half * D_FF + (c + 1) * fc)
            hcols = slice(half * fc, (half + 1) * fc)
            u = cb_ref[:, cols] + cw_ref[CONV_W - 1:CONV_W, cols] * r_cur[half]
            for tap in range(CONV_W - 1):
                back = CONV_W - 1 - tap
                u = u + cw_ref[tap:tap + 1, cols] * hist[k % 2][CONV_HALO - back:CONV_HALO - back + sub, hcols]
            halves.append(u)
        u_g, u_v = halves
        act_prev = (u_g * _sigmoid(u_g) * u_v).astype(BF16)
    down(acc, len(units) - 1, act_prev)
    for part in range(FFN_SPLIT):
        rows = slice(part * sub, (part + 1) * sub)
        o_ref[0, rows, :] = x1_ref[0, rows, :] + mod_ref[0, 3:4, :] * acc[part]


def _ffn(x1, h2, mod, wup_bf, conv_w, conv_b, wdn_bf):
    b, s, d = x1.shape
    tm = FFN_TILE
    const2 = lambda bi, t: (0, 0)
    return pl.pallas_call(
        _ffn_body,
        grid=(b, s // tm),
        in_specs=[
            pl.BlockSpec((1, tm, d), lambda bi, t: (bi, t, 0)),
            pl.BlockSpec((1, tm, d), lambda bi, t: (bi, t, 0)),
            pl.BlockSpec((1, N_MOD - N_MOD_EARLY, d), lambda bi, t: (bi, 0, 0)),
            pl.BlockSpec((d, 2 * D_FF), const2),
            pl.BlockSpec((CONV_W, 2 * D_FF), const2),
            pl.BlockSpec((1, 2 * D_FF), const2),
            pl.BlockSpec((D_FF, d), const2),
        ],
        out_specs=pl.BlockSpec((1, tm, d), lambda bi, t: (bi, t, 0)),
        out_shape=jax.ShapeDtypeStruct((b, s, d), F32),
        scratch_shapes=[
            pltpu.VMEM((tm // FFN_SPLIT + CONV_HALO, 2 * FFN_CHUNK), F32),
            pltpu.VMEM((tm // FFN_SPLIT + CONV_HALO, 2 * FFN_CHUNK), F32),
            pltpu.VMEM((CONV_HALO, 2 * D_FF), F32),
        ],
        compiler_params=pltpu.CompilerParams(
            dimension_semantics=("arbitrary", "arbitrary"), vmem_limit_bytes=VMEM_LIMIT),
        name="ffn",
    )(x1, h2, mod, wup_bf, conv_w, conv_b, wdn_bf)


def _layer(x, c_lanes, rel_bias, ada_w, ada_b, norm1_g, w_in, q_norm_g, k_norm_g, pool_w, pool_scale,
           w_branch_attn, w_branch_pool, w_out, norm2_g, w_up, conv_w, conv_b, w_down):
    b, s, d = x.shape
    mod_early = _ada_early(c_lanes, ada_w, ada_b[None, :])
    z, mod = _inproj(x, mod_early, norm1_g[None, :], w_in, c_lanes, ada_w, ada_b[None, :])
    gq_t = jnp.broadcast_to(jnp.tile(q_norm_g, HEADS_PER_STEP)[:, None], (PAIR_W, MOBA_BLOCK))
    gk_row = jnp.tile(k_norm_g, HEADS_PER_STEP)[None, :]
    attn, wup_bf, wdn_bf = _attention(z, gq_t, gk_row, _biasprep(rel_bias), w_up, w_down)
    x1, h2 = _mix(x, attn, z, mod, pool_w, pool_scale[None, :], w_branch_attn, w_branch_pool, w_out,
                  norm2_g[None, :])
    return _ffn(x1, h2, mod, wup_bf, conv_w, conv_b[None, :], wdn_bf)


def kernel(x, c, ada_w, ada_b, norm1_g, w_in, q_norm_g, k_norm_g, rel_bias, pool_w, pool_scale,
           w_branch_attn, w_branch_pool, w_out, norm2_g, w_up, conv_w, conv_b, w_down):
    b, s, d = x.shape
    assert d == D_MODEL and w_in.shape[-1] == IN_W
    assert all(s % tile == 0 for tile in (MOBA_BLOCK * PIPE_UNROLL, INPROJ_TILE, MIX_TILE, FFN_TILE))
    c_lanes = jnp.broadcast_to(c[:, :, None], (b, d, LANES))
    for l in range(ada_w.shape[0]):
        x = _layer(x, c_lanes, rel_bias, ada_w[l], ada_b[l], norm1_g[l], w_in[l], q_norm_g[l],
                   k_norm_g[l], pool_w[l], pool_scale[l], w_branch_attn[l], w_branch_pool[l],
                   w_out[l], norm2_g[l], w_up[l], conv_w[l], conv_b[l], w_down[l])
    return x
```

```python
import functools
import math

import numpy as np
import jax
import jax.numpy as jnp
from jax import lax
from jax.experimental import pallas as pl
from jax.experimental.pallas import tpu as pltpu

F32 = jnp.float32
BF16 = jnp.bfloat16

D_MODEL = 1024
ATTN_HEADS = 8
HEAD_DIM = 64
ATTN_W = ATTN_HEADS * HEAD_DIM
MOBA_BLOCK = 256
MOBA_TOPK = 3
POOL_GROUPS = 4
POOL_GROUP_W = 128
POOL_W = POOL_GROUPS * POOL_GROUP_W
POOL_WINDOWS = (2, 4, 8, 16)
NUM_BUCKETS = 32
MAX_DISTANCE = 128
D_FF = 2816
CONV_W = 3
EPS = 1e-6
IN_W = 3 * ATTN_W + POOL_W + 2 * D_MODEL

LANES = 128
BF16_SUBLANES = 16
HEADS_PER_STEP = LANES // HEAD_DIM
PAIR_W = HEADS_PER_STEP * HEAD_DIM
N_PAIRS = ATTN_HEADS // HEADS_PER_STEP
V_ROWS = HEAD_DIM + BF16_SUBLANES
LOG2E = math.log2(math.e)
MASK_BIG = 1e30
ADA_ROWS = 256
INPROJ_TILE = 1024
INPROJ_COLS = 1024
INPROJ_SPLIT = 4
POOL_HALO = 16
MIX_TILE = 512
MIX_SPLIT = 2
FFN_CHUNK = 256
FFN_TILE = 512
FFN_SPLIT = 2
FFN_AHEAD = 3
CONV_HALO = 8
VMEM_LIMIT = 56 * 1024 * 1024


def _split_bf16(a):
    hi = a.astype(BF16)
    lo = (a - hi.astype(F32)).astype(BF16)
    return hi, lo


def _dot(a, b):
    return jnp.dot(a, b, preferred_element_type=F32)


def _dot3(a, b):
    ah, al = _split_bf16(a)
    bh, bl = _split_bf16(b)
    return _dot(ah, bh) + _dot(al, bh) + _dot(ah, bl)


def _sigmoid(v):
    return 1.0 / (1.0 + jnp.exp2(v * (-LOG2E)))


def _cast_weight_once(src_ref, dst_ref, col_chunk=4 * LANES):
    first = functools.reduce(jnp.logical_and, [pl.program_id(a) == 0 for a in range(2)])

    @pl.when(first)
    def _():
        n = src_ref.shape[-1]
        for c0 in range(0, n, col_chunk):
            dst_ref[..., c0:c0 + col_chunk] = src_ref[..., c0:c0 + col_chunk].astype(BF16)


N_MOD = 6
N_MOD_EARLY = 2


def _ada_accumulate(c_ref, w_refs, b_refs, o_ref, first):
    n_batch, _, lanes = c_ref.shape
    dm = o_ref.shape[2]
    vectors = [(w_ref, b_ref, j) for w_ref, b_ref in zip(w_refs, b_refs) for j in range(w_ref.shape[1] // dm)]

    @pl.when(first)
    def _():
        for v, (_, b_ref, j) in enumerate(vectors):
            o_ref[:, v, :] = jnp.broadcast_to(b_ref[:, j * dm:(j + 1) * dm], (n_batch, dm))

    for bi in range(n_batch):
        c = c_ref[bi]
        act = c * _sigmoid(c)
        for v, (w_ref, _, j) in enumerate(vectors):
            cols = [jnp.sum(act * w_ref[:, j * dm + g * lanes:j * dm + (g + 1) * lanes], axis=0, keepdims=True)
                    for g in range(dm // lanes)]
            o_ref[bi, v:v + 1, :] += jnp.concatenate(cols, axis=-1)


def _ada_body(c_ref, w_ref, b_ref, o_ref):
    _ada_accumulate(c_ref, [w_ref], [b_ref], o_ref, pl.program_id(0) == 0)


def _ada_early(c_lanes, ada_w, ada_b):
    n_batch, d, lanes = c_lanes.shape
    n = N_MOD_EARLY * d
    tk = ADA_ROWS
    return pl.pallas_call(
        _ada_body,
        grid=(d // tk,),
        in_specs=[
            pl.BlockSpec((n_batch, tk, lanes), lambda k: (0, k, 0)),
            pl.BlockSpec((tk, n), lambda k: (k, 0)),
            pl.BlockSpec((1, n), lambda k: (0, 0)),
        ],
        out_specs=pl.BlockSpec((n_batch, N_MOD_EARLY, d), lambda k: (0, 0, 0)),
        out_shape=jax.ShapeDtypeStruct((n_batch, N_MOD_EARLY, d), F32),
        compiler_params=pltpu.CompilerParams(
            dimension_semantics=("arbitrary",), vmem_limit_bytes=VMEM_LIMIT),
        name="ada",
    )(c_lanes, ada_w, ada_b)


def _norm_mod(x, gain, scale, shift):
    ms = jnp.mean(x * x, axis=-1, keepdims=True)
    return x * lax.rsqrt(ms + EPS) * (gain * (1.0 + scale)) + shift


def _inproj_body(x_ref, mod_ref, g_ref, w32_ref, c_ref, aw0_ref, aw1_ref, ab0_ref, ab1_ref, z_ref, mod_late_ref,
                 w_ref):
    _cast_weight_once(w32_ref, w_ref)
    first = jnp.logical_and(pl.program_id(0) == 0, pl.program_id(1) == 0)
    _ada_accumulate(c_ref, [aw0_ref, aw1_ref], [ab0_ref, ab1_ref], mod_late_ref, first)
    sub = INPROJ_TILE // INPROJ_SPLIT
    n_chunk = INPROJ_COLS
    for part in range(INPROJ_SPLIT):
        rows = slice(part * sub, (part + 1) * sub)
        h = _norm_mod(x_ref[0, rows, :], g_ref[...], mod_ref[0, 1:2, :], mod_ref[0, 0:1, :]).astype(BF16)
        for n in range(IN_W // n_chunk):
            cols = slice(n * n_chunk, (n + 1) * n_chunk)
            z_ref[0, rows, cols] = _dot(h, w_ref[:, cols]).astype(BF16)


def _inproj(x, mod_early, g1, w_in, c_lanes, ada_w, ada_b):
    b, s, d = x.shape
    tiles = s // INPROJ_TILE
    n_late = N_MOD - N_MOD_EARLY
    pair = 2 * d
    assert N_MOD_EARLY * d == pair and n_late * d == 2 * pair
    slab_rows = d // (b * tiles)
    assert slab_rows * b * tiles == d and slab_rows % 8 == 0
    slab = lambda bi, t: bi * tiles + t
    return pl.pallas_call(
        _inproj_body,
        grid=(b, tiles),
        in_specs=[
            pl.BlockSpec((1, INPROJ_TILE, d), lambda bi, t: (bi, t, 0)),
            pl.BlockSpec((1, N_MOD_EARLY, d), lambda bi, t: (bi, 0, 0)),
            pl.BlockSpec((1, d), lambda bi, t: (0, 0)),
            pl.BlockSpec((d, IN_W), lambda bi, t: (0, 0)),
            pl.BlockSpec((b, slab_rows, c_lanes.shape[2]), lambda bi, t: (0, slab(bi, t), 0)),
            pl.BlockSpec((slab_rows, pair), lambda bi, t: (slab(bi, t), 1)),
            pl.BlockSpec((slab_rows, pair), lambda bi, t: (slab(bi, t), 2)),
            pl.BlockSpec((1, pair), lambda bi, t: (0, 1)),
            pl.BlockSpec((1, pair), lambda bi, t: (0, 2)),
        ],
        out_specs=[
            pl.BlockSpec((1, INPROJ_TILE, IN_W), lambda bi, t: (bi, t, 0)),
            pl.BlockSpec((b, n_late, d), lambda bi, t: (0, 0, 0)),
        ],
        out_shape=[
            jax.ShapeDtypeStruct((b, s, IN_W), BF16),
            jax.ShapeDtypeStruct((b, n_late, d), F32),
        ],
        scratch_shapes=[pltpu.VMEM((d, IN_W), BF16)],
        compiler_params=pltpu.CompilerParams(
            dimension_semantics=("arbitrary", "arbitrary"), vmem_limit_bytes=VMEM_LIMIT),
        name="inproj",
    )(x, mod_early, g1, w_in, c_lanes, ada_w, ada_w, ada_b, ada_b)


def _t5_bucket_np(dist):
    max_exact = NUM_BUCKETS // 2
    n = np.maximum(dist, 0)
    nf = np.maximum(n, 1).astype(np.float64)
    large = max_exact + (np.log(nf / max_exact) / math.log(MAX_DISTANCE / max_exact)
                         * (NUM_BUCKETS - max_exact)).astype(np.int32)
    large = np.minimum(large, NUM_BUCKETS - 1)
    return np.where(n < max_exact, n, large).astype(np.int32)


BIAS_ROW_W = 4 * MOBA_BLOCK


def _bucket_row():
    d = np.arange(BIAS_ROW_W) - MOBA_BLOCK
    return np.where(d >= 0, _t5_bucket_np(d), -1).astype(np.int32)[None, :]


def _biasprep_body(rb_ref, bkt_ref, o_ref):
    bkt = bkt_ref[...]
    for h in range(ATTN_HEADS):
        far = rb_ref[NUM_BUCKETS - 1, h]
        row = jnp.where(bkt < 0, -MASK_BIG, 0.0).astype(F32)
        for bk in range(NUM_BUCKETS):
            row = jnp.where(bkt == bk, (rb_ref[bk, h] - far) * LOG2E, row)
        rows = jnp.broadcast_to(row, (MOBA_BLOCK, BIAS_ROW_W))
        rolled = pltpu.roll(rows, 0, 1, stride=1, stride_axis=0)
        o_ref[h, 0] = rolled[:, MOBA_BLOCK:2 * MOBA_BLOCK]
        o_ref[h, 1] = rolled[:, 2 * MOBA_BLOCK:3 * MOBA_BLOCK]


def _biasprep(rel_bias):
    bkt = jnp.asarray(_bucket_row())
    return pl.pallas_call(
        _biasprep_body,
        grid=(1,),
        in_specs=[
            pl.BlockSpec(memory_space=pltpu.SMEM),
            pl.BlockSpec((1, BIAS_ROW_W), lambda n: (0, 0)),
        ],
        out_specs=pl.BlockSpec((ATTN_HEADS, 2, MOBA_BLOCK, MOBA_BLOCK), lambda n: (0, 0, 0, 0)),
        out_shape=jax.ShapeDtypeStruct((ATTN_HEADS, 2, MOBA_BLOCK, MOBA_BLOCK), F32),
        compiler_params=pltpu.CompilerParams(dimension_semantics=("arbitrary",)),
        name="biasprep",
    )(rel_bias, bkt)


PREP_UNROLL = 8
ITEM_FIELDS = 5
PIPE_UNROLL = 16


def _far_items(n_blocks):
    never = n_blocks - 1
    items = []
    for i in range(n_blocks):
        n_far = max(i - 1, 0)
        for j0 in range(0, n_far, 2):
            items.append((i, j0, j0, j0 + 1, j0 + 1) if j0 + 1 < n_far else (i, j0, j0, j0, never))
    dummy = (n_blocks - 1, 0, never, 0, never)
    items.extend([dummy] * (-len(items) % PIPE_UNROLL))
    items.append(dummy)
    return np.asarray(items, np.int32)


def _attn_body(tbl_ref, zq_ref, zk_ref, zv_ref, gq_ref, gk_ref, bias_ref, wup32_ref, wdn32_ref,
               o_ref, wup_ref, wdn_ref,
               kn_ref, vta_ref, w_ref, pen_ref, acc_ref, m_ref, s0_ref, s1_ref, *, n_blocks, n_items):
    blk = MOBA_BLOCK
    heads = range(HEADS_PER_STEP)

    wup_ref[...] = wup32_ref[...].astype(BF16)
    wdn_ref[...] = wdn32_ref[...].astype(BF16)

    def rows_of(j):
        return pl.ds(pl.multiple_of(j * blk, blk), blk)

    blk_rows = lax.broadcasted_iota(jnp.int32, (n_blocks, PAIR_W), 0)
    ones = jnp.ones((V_ROWS - HEAD_DIM, blk), BF16)
    same_head = (lax.broadcasted_iota(jnp.int32, (PAIR_W, PAIR_W), 0) // HEAD_DIM
                 == lax.broadcasted_iota(jnp.int32, (PAIR_W, PAIR_W), 1) // HEAD_DIM)
    head_sum = jnp.where(same_head, 1.0, 0.0).astype(BF16)

    def kv_prep(i, kmean):
        kraw = zk_ref[0, rows_of(i), :].astype(F32)
        sq_hi, sq_lo = _split_bf16(kraw * kraw)
        ss = _dot(sq_hi, head_sum) + _dot(sq_lo, head_sum)
        rinv = lax.rsqrt(ss * (1.0 / HEAD_DIM) + EPS)
        kn = kraw * rinv * gk_ref[...]
        kn_ref[rows_of(i), :] = kn.astype(BF16)
        vt = zv_ref[0, rows_of(i), :].astype(F32).T
        for hd in heads:
            vta_ref[i, hd, 0:HEAD_DIM, :] = vt[hd * HEAD_DIM:(hd + 1) * HEAD_DIM].astype(BF16)
            vta_ref[i, hd, HEAD_DIM:V_ROWS, :] = ones
        return jnp.where(blk_rows == i, jnp.mean(kn, axis=0, keepdims=True), kmean)

    kmean = lax.fori_loop(0, n_blocks, kv_prep, jnp.zeros((n_blocks, PAIR_W), F32), unroll=PREP_UNROLL)

    blk_ids = lax.broadcasted_iota(jnp.int32, (n_blocks, blk), 0)
    zeros_half = jnp.zeros((HEAD_DIM, blk), F32)

    def q_prep(i, _):
        qt = zq_ref[0, rows_of(i), :].astype(F32).T
        qsq = qt * qt
        for hd in heads:
            rows = slice(hd * HEAD_DIM, (hd + 1) * HEAD_DIM)
            ss = jnp.sum(qsq[rows], axis=0, keepdims=True)
            r = lax.rsqrt(ss * (1.0 / HEAD_DIM) + EPS)
            qn = qt[rows] * r * gq_ref[rows, :] * (HEAD_DIM ** -0.5 * LOG2E)
            parts = [zeros_half] * HEADS_PER_STEP
            parts[hd] = qn
            w = jnp.concatenate(parts, axis=0)
            w_ref[i, hd] = w.astype(BF16)
            gate = jnp.where(blk_ids < i, _dot3(kmean, w), -jnp.inf)
            sel = jnp.zeros((n_blocks, blk), jnp.bool_)
            for _ in range(MOBA_TOPK):
                mx = jnp.max(gate, axis=0, keepdims=True)
                cand = jnp.where(gate == mx, blk_ids, n_blocks)
                pick = (blk_ids == jnp.min(cand, axis=0, keepdims=True)) & (mx > -jnp.inf)
                sel = sel | pick
                gate = jnp.where(pick, -jnp.inf, gate)
            pen_ref[hd, i] = jnp.where(sel, 0.0, -MASK_BIG).astype(F32)
        return 0

    lax.fori_loop(0, n_blocks, q_prep, 0, unroll=PREP_UNROLL)

    def score(i, blocks, s_ref, biased):
        keys = [kn_ref[rows_of(j), :] for j, _ in blocks]
        gmax = []
        for hd in heads:
            w = w_ref[i, hd]
            tmax = []
            for u, (_, jpen) in enumerate(blocks):
                s = _dot(keys[u], w)
                if biased:
                    s = s + bias_ref[hd, u]
                s_ref[hd, u] = s
                t = jnp.max(s, axis=0, keepdims=True)
                tmax.append(t if jpen is None else t + pen_ref[hd, i, pl.ds(jpen, 1), :])
            gmax.append(functools.reduce(jnp.maximum, tmax))
        return tuple(gmax)

    def attend(i, blocks, s_ref, gmax, first_visit):
        for hd in heads:
            if first_visit:
                m_new = gmax[hd]
                acc = None
            else:
                m_old = m_ref[i, hd]
                m_new = jnp.maximum(m_old, gmax[hd])
                acc = acc_ref[i, hd] * jnp.exp2(m_old - m_new)
            for u, (j, jpen) in enumerate(blocks):
                shift = m_new if jpen is None else m_new - pen_ref[hd, i, pl.ds(jpen, 1), :]
                p = jnp.exp2(s_ref[hd, u] - shift).astype(BF16)
                pv = _dot(vta_ref[j, hd], p)
                acc = pv if acc is None else acc + pv
            acc_ref[i, hd] = acc
            m_ref[i, hd] = m_new

    def near_blocks(i):
        jp = jnp.maximum(i - 1, 0)
        return ((i, None), (jp, jp))

    slots = (s0_ref, s1_ref)

    def near_body(t, g):
        for u in range(PIPE_UNROLL):
            i_cur = PIPE_UNROLL * t + u
            i_nxt = jnp.minimum(i_cur + 1, n_blocks - 1)
            g_nxt = score(i_nxt, near_blocks(i_nxt), slots[(u + 1) % 2], True)
            attend(i_cur, near_blocks(i_cur), slots[u % 2], g, True)
            g = g_nxt
        return g

    lax.fori_loop(0, n_blocks // PIPE_UNROLL, near_body, score(0, near_blocks(0), s0_ref, True))

    def item(k):
        base = k * ITEM_FIELDS
        i = tbl_ref[base]
        return i, ((tbl_ref[base + 1], tbl_ref[base + 2]), (tbl_ref[base + 3], tbl_ref[base + 4]))

    def far_body(t, g):
        for u in range(PIPE_UNROLL):
            k = PIPE_UNROLL * t + u
            g_nxt = score(*item(k + 1), slots[(u + 1) % 2], False)
            attend(*item(k), slots[u % 2], g, False)
            g = g_nxt
        return g

    lax.fori_loop(0, n_items // PIPE_UNROLL, far_body, score(*item(0), s0_ref, False))

    def finish(i, _):
        outs = []
        for hd in heads:
            acc = acc_ref[i, hd]
            outs.append(acc[0:HEAD_DIM] / acc[HEAD_DIM:HEAD_DIM + 1])
        o_ref[0, rows_of(i), :] = jnp.concatenate(outs, axis=0).T.astype(BF16)
        return 0

    lax.fori_loop(0, n_blocks, finish, 0, unroll=PREP_UNROLL)


def _attention(z, gq_t, gk_row, bias, w_up, w_down):
    b, s, _ = z.shape
    n_blocks = s // MOBA_BLOCK
    assert n_blocks % PIPE_UNROLL == 0 and PIPE_UNROLL % 2 == 0
    blk = MOBA_BLOCK
    items = _far_items(n_blocks)
    n_items = items.shape[0] - 1
    n_steps = b * N_PAIRS
    up_rows, dn_rows = w_up.shape[0] // n_steps, w_down.shape[0] // n_steps
    assert up_rows * n_steps == w_up.shape[0] and dn_rows * n_steps == w_down.shape[0]
    assert up_rows % BF16_SUBLANES == 0 and dn_rows % BF16_SUBLANES == 0

    def slab(bi, hp, tbl):
        return (bi * N_PAIRS + hp, 0)

    grid_spec = pltpu.PrefetchScalarGridSpec(
        num_scalar_prefetch=1,
        grid=(b, N_PAIRS),
        in_specs=[
            pl.BlockSpec((1, s, PAIR_W), lambda bi, hp, tbl: (bi, 0, hp)),
            pl.BlockSpec((1, s, PAIR_W), lambda bi, hp, tbl: (bi, 0, N_PAIRS + hp)),
            pl.BlockSpec((1, s, PAIR_W), lambda bi, hp, tbl: (bi, 0, 2 * N_PAIRS + hp)),
            pl.BlockSpec((PAIR_W, blk), lambda bi, hp, tbl: (0, 0)),
            pl.BlockSpec((1, PAIR_W), lambda bi, hp, tbl: (0, 0)),
            pl.BlockSpec((HEADS_PER_STEP, 2, blk, blk), lambda bi, hp, tbl: (hp, 0, 0, 0)),
            pl.BlockSpec((up_rows, w_up.shape[1]), slab),
            pl.BlockSpec((dn_rows, w_down.shape[1]), slab),
        ],
        out_specs=[
            pl.BlockSpec((1, s, PAIR_W), lambda bi, hp, tbl: (bi, 0, hp)),
            pl.BlockSpec((up_rows, w_up.shape[1]), slab),
            pl.BlockSpec((dn_rows, w_down.shape[1]), slab),
        ],
        scratch_shapes=[
            pltpu.VMEM((s, PAIR_W), BF16),
            pltpu.VMEM((n_blocks, HEADS_PER_STEP, V_ROWS, blk), BF16),
            pltpu.VMEM((n_blocks, HEADS_PER_STEP, PAIR_W, blk), BF16),
            pltpu.VMEM((HEADS_PER_STEP, n_blocks, n_blocks, blk), F32),
            pltpu.VMEM((n_blocks, HEADS_PER_STEP, V_ROWS, blk), F32),
            pltpu.VMEM((n_blocks, HEADS_PER_STEP, 1, blk), F32),
            pltpu.VMEM((HEADS_PER_STEP, 2, blk, blk), F32),
            pltpu.VMEM((HEADS_PER_STEP, 2, blk, blk), F32),
        ],
    )
    return pl.pallas_call(
        functools.partial(_attn_body, n_blocks=n_blocks, n_items=n_items),
        grid_spec=grid_spec,
        out_shape=[
            jax.ShapeDtypeStruct((b, s, ATTN_W), BF16),
            jax.ShapeDtypeStruct(w_up.shape, BF16),
            jax.ShapeDtypeStruct(w_down.shape, BF16),
        ],
        compiler_params=pltpu.CompilerParams(
            dimension_semantics=("arbitrary", "arbitrary"), vmem_limit_bytes=VMEM_LIMIT),
        name="attn",
    )(jnp.asarray(items.reshape(-1)), z, z, z, gq_t, gk_row, bias, w_up, w_down)


def _mix_body(x_ref, attn_ref, p_ref, ga_ref, gp_ref, mod_ref, pw32_ref, ps_ref, wba32_ref, wbp32_ref,
              wo32_ref, g2_ref, x1_ref, h2_ref, pe_ref, pw_ref, wba_ref, wbp_ref, wo_ref):
    for w32, wbf in ((pw32_ref, pw_ref), (wba32_ref, wba_ref), (wbp32_ref, wbp_ref), (wo32_ref, wo_ref)):
        _cast_weight_once(w32, wbf)
    t = pl.program_id(1)
    tm = MIX_TILE
    sub = MIX_TILE // MIX_SPLIT

    @pl.when(t == 0)
    def _():
        pe_ref[0:POOL_HALO, :] = jnp.zeros((POOL_HALO, POOL_W), F32)

    @pl.when(t > 0)
    def _():
        pe_ref[0:POOL_HALO, :] = pe_ref[tm:tm + POOL_HALO, :]

    pe_ref[POOL_HALO:POOL_HALO + tm, :] = p_ref[0].astype(F32)

    for part in range(MIX_SPLIT):
        r0 = part * sub
        rows = slice(r0, r0 + sub)
        pos = t * tm + r0 + lax.broadcasted_iota(jnp.int32, (sub, POOL_GROUP_W), 0)
        pooled = []
        for g, win in enumerate(POOL_WINDOWS):
            cols = slice(g * POOL_GROUP_W, (g + 1) * POOL_GROUP_W)
            ext = pe_ref[r0:r0 + POOL_HALO + sub, cols]
            tot = ext
            span = 1
            while span < win:
                tot = tot + pltpu.roll(tot, span, 0)
                span *= 2
            count = jnp.minimum(pos + 1, win).astype(F32)
            pooled_g = (tot[POOL_HALO:] / count - ext[POOL_HALO:]).astype(BF16)
            pooled.append(_dot(pooled_g, pw_ref[g]) * ps_ref[:, cols])
        pool = jnp.concatenate(pooled, axis=-1).astype(BF16)

        a_proj = _dot(attn_ref[0, rows, :], wba_ref[...])
        p_proj = _dot(pool, wbp_ref[...])
        merged = (_sigmoid(ga_ref[0, rows, :].astype(F32)) * a_proj
                  + _sigmoid(gp_ref[0, rows, :].astype(F32)) * p_proj).astype(BF16)
        x1 = x_ref[0, rows, :] + mod_ref[0, 0:1, :] * _dot(merged, wo_ref[...])
        x1_ref[0, rows, :] = x1
        h2_ref[0, rows, :] = _norm_mod(x1, g2_ref[...], mod_ref[0, 2:3, :], mod_ref[0, 1:2, :]).astype(BF16)


def _mix(x, attn, z, mod, pool_w, pool_scale, wba, wbp, wo, g2):
    b, s, d = x.shape
    tm = MIX_TILE
    const2 = lambda bi, t: (0, 0)
    return pl.pallas_call(
        _mix_body,
        grid=(b, s // tm),
        in_specs=[
            pl.BlockSpec((1, tm, d), lambda bi, t: (bi, t, 0)),
            pl.BlockSpec((1, tm, ATTN_W), lambda bi, t: (bi, t, 0)),
            pl.BlockSpec((1, tm, POOL_W), lambda bi, t: (bi, t, 3 * ATTN_W // POOL_W)),
            pl.BlockSpec((1, tm, d), lambda bi, t: (bi, t, (3 * ATTN_W + POOL_W) // d)),
            pl.BlockSpec((1, tm, d), lambda bi, t: (bi, t, (3 * ATTN_W + POOL_W) // d + 1)),
            pl.BlockSpec((1, N_MOD - N_MOD_EARLY, d), lambda bi, t: (bi, 0, 0)),
            pl.BlockSpec((POOL_GROUPS, POOL_GROUP_W, POOL_GROUP_W), lambda bi, t: (0, 0, 0)),
            pl.BlockSpec((1, POOL_W), const2),
            pl.BlockSpec((ATTN_W, d), const2),
            pl.BlockSpec((POOL_W, d), const2),
            pl.BlockSpec((d, d), const2),
            pl.BlockSpec((1, d), const2),
        ],
        out_specs=[
            pl.BlockSpec((1, tm, d), lambda bi, t: (bi, t, 0)),
            pl.BlockSpec((1, tm, d), lambda bi, t: (bi, t, 0)),
        ],
        out_shape=[
            jax.ShapeDtypeStruct((b, s, d), F32),
            jax.ShapeDtypeStruct((b, s, d), BF16),
        ],
        scratch_shapes=[
            pltpu.VMEM((tm + POOL_HALO, POOL_W), F32),
            pltpu.VMEM(pool_w.shape, BF16),
            pltpu.VMEM(wba.shape, BF16),
            pltpu.VMEM(wbp.shape, BF16),
            pltpu.VMEM(wo.shape, BF16),
        ],
        compiler_params=pltpu.CompilerParams(
            dimension_semantics=("arbitrary", "arbitrary"), vmem_limit_bytes=VMEM_LIMIT),
        name="mix",
    )(x, attn, z, z, z, mod, pool_w, pool_scale, wba, wbp, wo, g2)


def _ffn_body(x1_ref, h2_ref, mod_ref, wup_ref, cw_ref, cb_ref, wdn_ref, o_ref, carry_ref, *hist):
    t = pl.program_id(1)
    sub = FFN_TILE // FFN_SPLIT
    fc = FFN_CHUNK
    n_chunks = D_FF // fc

    @pl.when(t == 0)
    def _():
        carry_ref[...] = jnp.zeros_like(carry_ref)

    units = [(part, c) for part in range(FFN_SPLIT) for c in range(n_chunks)]
    h2 = [h2_ref[0, part * sub:(part + 1) * sub, :] for part in range(FFN_SPLIT)]

    def up(k):
        part, c = units[k]
        out = []
        for half in range(2):
            cols = slice(half * D_FF + c * fc, half * D_FF + (c + 1) * fc)
            hcols = slice(half * fc, (half + 1) * fc)
            r = _dot(h2[part], wup_ref[:, cols])
            hist[k % len(hist)][0:CONV_HALO, hcols] = carry_ref[:, cols]
            hist[k % len(hist)][CONV_HALO:CONV_HALO + sub, hcols] = r
            carry_ref[:, cols] = r[sub - CONV_HALO:sub]
            out.append(r)
        return out

    def down(acc, k, act):
        part, c = units[k]
        acc[part] = acc[part] + _dot(act, wdn_ref[c * fc:(c + 1) * fc, :])

    acc = [jnp.zeros((sub, D_MODEL), F32) for _ in range(FFN_SPLIT)]
    ahead = [up(k) for k in range(FFN_AHEAD)]
    act_prev = None
    for k, (part, c) in enumerate(units):
        r_cur = ahead.pop(0)
        if k + FFN_AHEAD < len(units):
            ahead.append(up(k + FFN_AHEAD))
        if act_prev is not None:
            down(acc, k - 1, act_prev)
        halves = []
        for half in range(2):
            cols = slice(half * D_FF + c * fc, half * D_FF + (c + 1) * fc)
            hcols = slice(half * fc, (half + 1) * fc)
            u = cb_ref[:, cols] + cw_ref[CONV_W - 1:CONV_W, cols] * r_cur[half]
            for tap in range(CONV_W - 1):
                back = CONV_W - 1 - tap
                u = u + cw_ref[tap:tap + 1, cols] * hist[k % len(hist)][CONV_HALO - back:CONV_HALO - back + sub, hcols]
            halves.append(u)
        u_g, u_v = halves
        act_prev = (u_g * _sigmoid(u_g) * u_v).astype(BF16)
    down(acc, len(units) - 1, act_prev)
    for part in range(FFN_SPLIT):
        rows = slice(part * sub, (part + 1) * sub)
        o_ref[0, rows, :] = x1_ref[0, rows, :] + mod_ref[0, 3:4, :] * acc[part]


def _ffn(x1, h2, mod, wup_bf, conv_w, conv_b, wdn_bf):
    b, s, d = x1.shape
    tm = FFN_TILE
    const2 = lambda bi, t: (0, 0)
    return pl.pallas_call(
        _ffn_body,
        grid=(b, s // tm),
        in_specs=[
            pl.BlockSpec((1, tm, d), lambda bi, t: (bi, t, 0)),
            pl.BlockSpec((1, tm, d), lambda bi, t: (bi, t, 0)),
            pl.BlockSpec((1, N_MOD - N_MOD_EARLY, d), lambda bi, t: (bi, 0, 0)),
            pl.BlockSpec((d, 2 * D_FF), const2),
            pl.BlockSpec((CONV_W, 2 * D_FF), const2),
            pl.BlockSpec((1, 2 * D_FF), const2),
            pl.BlockSpec((D_FF, d), const2),
        ],
        out_specs=pl.BlockSpec((1, tm, d), lambda bi, t: (bi, t, 0)),
        out_shape=jax.ShapeDtypeStruct((b, s, d), F32),
        scratch_shapes=[
            pltpu.VMEM((CONV_HALO, 2 * D_FF), F32),
        ] + [pltpu.VMEM((tm // FFN_SPLIT + CONV_HALO, 2 * FFN_CHUNK), F32)
             for _ in range(FFN_AHEAD + 1)],
        compiler_params=pltpu.CompilerParams(
            dimension_semantics=("arbitrary", "arbitrary"), vmem_limit_bytes=VMEM_LIMIT),
        name="ffn",
    )(x1, h2, mod, wup_bf, conv_w, conv_b, wdn_bf)


def _layer(x, c_lanes, rel_bias, ada_w, ada_b, norm1_g, w_in, q_norm_g, k_norm_g, pool_w, pool_scale,
           w_branch_attn, w_branch_pool, w_out, norm2_g, w_up, conv_w, conv_b, w_down):
    b, s, d = x.shape
    mod_early = _ada_early(c_lanes, ada_w, ada_b[None, :])
    z, mod = _inproj(x, mod_early, norm1_g[None, :], w_in, c_lanes, ada_w, ada_b[None, :])
    gq_t = jnp.broadcast_to(jnp.tile(q_norm_g, HEADS_PER_STEP)[:, None], (PAIR_W, MOBA_BLOCK))
    gk_row = jnp.tile(k_norm_g, HEADS_PER_STEP)[None, :]
    attn, wup_bf, wdn_bf = _attention(z, gq_t, gk_row, _biasprep(rel_bias), w_up, w_down)
    x1, h2 = _mix(x, attn, z, mod, pool_w, pool_scale[None, :], w_branch_attn, w_branch_pool, w_out,
                  norm2_g[None, :])
    return _ffn(x1, h2, mod, wup_bf, conv_w, conv_b[None, :], wdn_bf)


def kernel(x, c, ada_w, ada_b, norm1_g, w_in, q_norm_g, k_norm_g, rel_bias, pool_w, pool_scale,
           w_branch_attn, w_branch_pool, w_out, norm2_g, w_up, conv_w, conv_b, w_down):
    b, s, d = x.shape
    assert d == D_MODEL and w_in.shape[-1] == IN_W
    assert all(s % tile == 0 for tile in (MOBA_BLOCK * PIPE_UNROLL, INPROJ_TILE, MIX_TILE, FFN_TILE))
    c_lanes = jnp.broadcast_to(c[:, :, None], (b, d, LANES))
    for l in range(ada_w.shape[0]):
        x = _layer(x, c_lanes, rel_bias, ada_w[l], ada_b[l], norm1_g[l], w_in[l], q_norm_g[l],
                   k_norm_g[l], pool_w[l], pool_scale[l], w_branch_attn[l], w_branch_pool[l],
                   w_out[l], norm2_g[l], w_up[l], conv_w[l], conv_b[l], w_down[l])
    return x
```

```python
import functools
import math

import numpy as np
import jax
import jax.numpy as jnp
from jax import lax
from jax.experimental import pallas as pl
from jax.experimental.pallas import tpu as pltpu

F32 = jnp.float32
BF16 = jnp.bfloat16

D_MODEL = 1024
ATTN_HEADS = 8
HEAD_DIM = 64
ATTN_W = ATTN_HEADS * HEAD_DIM
MOBA_BLOCK = 256
MOBA_TOPK = 3
POOL_GROUPS = 4
POOL_GROUP_W = 128
POOL_W = POOL_GROUPS * POOL_GROUP_W
POOL_WINDOWS = (2, 4, 8, 16)
NUM_BUCKETS = 32
MAX_DISTANCE = 128
D_FF = 2816
CONV_W = 3
EPS = 1e-6
IN_W = 3 * ATTN_W + POOL_W + 2 * D_MODEL

LANES = 128
BF16_SUBLANES = 16
HEADS_PER_STEP = LANES // HEAD_DIM
PAIR_W = HEADS_PER_STEP * HEAD_DIM
N_PAIRS = ATTN_HEADS // HEADS_PER_STEP
V_ROWS = HEAD_DIM + BF16_SUBLANES
LOG2E = math.log2(math.e)
MASK_BIG = 1e30
ADA_ROWS = 256
INPROJ_TILE = 1024
INPROJ_COLS = 1024
INPROJ_SPLIT = 4
POOL_HALO = 16
MIX_TILE = 512
MIX_SPLIT = 2
FFN_CHUNK = 256
FFN_TILE = 512
FFN_SPLIT = 2
FFN_AHEAD = 3
CONV_HALO = 8
VMEM_LIMIT = 56 * 1024 * 1024


def _split_bf16(a):
    hi = a.astype(BF16)
    lo = (a - hi.astype(F32)).astype(BF16)
    return hi, lo


def _dot(a, b):
    return jnp.dot(a, b, preferred_element_type=F32)


def _dot3(a, b):
    ah, al = _split_bf16(a)
    bh, bl = _split_bf16(b)
    return _dot(ah, bh) + _dot(al, bh) + _dot(ah, bl)


def _sigmoid(v):
    return 1.0 / (1.0 + jnp.exp2(v * (-LOG2E)))


def _cast_weight_once(src_ref, dst_ref, col_chunk=4 * LANES):
    first = functools.reduce(jnp.logical_and, [pl.program_id(a) == 0 for a in range(2)])

    @pl.when(first)
    def _():
        n = src_ref.shape[-1]
        for c0 in range(0, n, col_chunk):
            dst_ref[..., c0:c0 + col_chunk] = src_ref[..., c0:c0 + col_chunk].astype(BF16)


N_MOD = 6
N_MOD_EARLY = 2


def _ada_accumulate(c_ref, w_refs, b_refs, o_ref, first):
    n_batch, _, lanes = c_ref.shape
    dm = o_ref.shape[2]
    vectors = [(w_ref, b_ref, j) for w_ref, b_ref in zip(w_refs, b_refs) for j in range(w_ref.shape[1] // dm)]

    @pl.when(first)
    def _():
        for v, (_, b_ref, j) in enumerate(vectors):
            o_ref[:, v, :] = jnp.broadcast_to(b_ref[:, j * dm:(j + 1) * dm], (n_batch, dm))

    for bi in range(n_batch):
        c = c_ref[bi]
        act = c * _sigmoid(c)
        for v, (w_ref, _, j) in enumerate(vectors):
            cols = [jnp.sum(act * w_ref[:, j * dm + g * lanes:j * dm + (g + 1) * lanes], axis=0, keepdims=True)
                    for g in range(dm // lanes)]
            o_ref[bi, v:v + 1, :] += jnp.concatenate(cols, axis=-1)


def _ada_body(c_ref, w_ref, b_ref, o_ref):
    _ada_accumulate(c_ref, [w_ref], [b_ref], o_ref, pl.program_id(0) == 0)


def _ada_early(c_lanes, ada_w, ada_b):
    n_batch, d, lanes = c_lanes.shape
    n = N_MOD_EARLY * d
    tk = ADA_ROWS
    return pl.pallas_call(
        _ada_body,
        grid=(d // tk,),
        in_specs=[
            pl.BlockSpec((n_batch, tk, lanes), lambda k: (0, k, 0)),
            pl.BlockSpec((tk, n), lambda k: (k, 0)),
            pl.BlockSpec((1, n), lambda k: (0, 0)),
        ],
        out_specs=pl.BlockSpec((n_batch, N_MOD_EARLY, d), lambda k: (0, 0, 0)),
        out_shape=jax.ShapeDtypeStruct((n_batch, N_MOD_EARLY, d), F32),
        compiler_params=pltpu.CompilerParams(
            dimension_semantics=("arbitrary",), vmem_limit_bytes=VMEM_LIMIT),
        name="ada",
    )(c_lanes, ada_w, ada_b)


def _norm_mod(x, gain, scale, shift):
    ms = jnp.mean(x * x, axis=-1, keepdims=True)
    return x * lax.rsqrt(ms + EPS) * (gain * (1.0 + scale)) + shift


def _inproj_body(x_ref, mod_ref, g_ref, w32_ref, c_ref, aw0_ref, aw1_ref, ab0_ref, ab1_ref, z_ref, mod_late_ref,
                 w_ref):
    _cast_weight_once(w32_ref, w_ref)
    first = jnp.logical_and(pl.program_id(0) == 0, pl.program_id(1) == 0)
    _ada_accumulate(c_ref, [aw0_ref, aw1_ref], [ab0_ref, ab1_ref], mod_late_ref, first)
    sub = INPROJ_TILE // INPROJ_SPLIT
    n_chunk = INPROJ_COLS
    for part in range(INPROJ_SPLIT):
        rows = slice(part * sub, (part + 1) * sub)
        h = _norm_mod(x_ref[0, rows, :], g_ref[...], mod_ref[0, 1:2, :], mod_ref[0, 0:1, :]).astype(BF16)
        for n in range(IN_W // n_chunk):
            cols = slice(n * n_chunk, (n + 1) * n_chunk)
            z_ref[0, rows, cols] = _dot(h, w_ref[:, cols]).astype(BF16)


def _inproj(x, mod_early, g1, w_in, c_lanes, ada_w, ada_b):
    b, s, d = x.shape
    tiles = s // INPROJ_TILE
    n_late = N_MOD - N_MOD_EARLY
    pair = 2 * d
    assert N_MOD_EARLY * d == pair and n_late * d == 2 * pair
    slab_rows = d // (b * tiles)
    assert slab_rows * b * tiles == d and slab_rows % 8 == 0
    slab = lambda bi, t: bi * tiles + t
    return pl.pallas_call(
        _inproj_body,
        grid=(b, tiles),
        in_specs=[
            pl.BlockSpec((1, INPROJ_TILE, d), lambda bi, t: (bi, t, 0)),
            pl.BlockSpec((1, N_MOD_EARLY, d), lambda bi, t: (bi, 0, 0)),
            pl.BlockSpec((1, d), lambda bi, t: (0, 0)),
            pl.BlockSpec((d, IN_W), lambda bi, t: (0, 0)),
            pl.BlockSpec((b, slab_rows, c_lanes.shape[2]), lambda bi, t: (0, slab(bi, t), 0)),
            pl.BlockSpec((slab_rows, pair), lambda bi, t: (slab(bi, t), 1)),
            pl.BlockSpec((slab_rows, pair), lambda bi, t: (slab(bi, t), 2)),
            pl.BlockSpec((1, pair), lambda bi, t: (0, 1)),
            pl.BlockSpec((1, pair), lambda bi, t: (0, 2)),
        ],
        out_specs=[
            pl.BlockSpec((1, INPROJ_TILE, IN_W), lambda bi, t: (bi, t, 0)),
            pl.BlockSpec((b, n_late, d), lambda bi, t: (0, 0, 0)),
        ],
        out_shape=[
            jax.ShapeDtypeStruct((b, s, IN_W), BF16),
            jax.ShapeDtypeStruct((b, n_late, d), F32),
        ],
        scratch_shapes=[pltpu.VMEM((d, IN_W), BF16)],
        compiler_params=pltpu.CompilerParams(
            dimension_semantics=("arbitrary", "arbitrary"), vmem_limit_bytes=VMEM_LIMIT),
        name="inproj",
    )(x, mod_early, g1, w_in, c_lanes, ada_w, ada_w, ada_b, ada_b)


def _t5_bucket_np(dist):
    max_exact = NUM_BUCKETS // 2
    n = np.maximum(dist, 0)
    nf = np.maximum(n, 1).astype(np.float64)
    large = max_exact + (np.log(nf / max_exact) / math.log(MAX_DISTANCE / max_exact)
                         * (NUM_BUCKETS - max_exact)).astype(np.int32)
    large = np.minimum(large, NUM_BUCKETS - 1)
    return np.where(n < max_exact, n, large).astype(np.int32)


BIAS_ROW_W = 4 * MOBA_BLOCK


def _bucket_row():
    d = np.arange(BIAS_ROW_W) - MOBA_BLOCK
    return np.where(d >= 0, _t5_bucket_np(d), -1).astype(np.int32)[None, :]


def _biasprep_body(rb_ref, bkt_ref, o_ref):
    bkt = bkt_ref[...]
    for h in range(ATTN_HEADS):
        far = rb_ref[NUM_BUCKETS - 1, h]
        row = jnp.where(bkt < 0, -MASK_BIG, 0.0).astype(F32)
        for bk in range(NUM_BUCKETS):
            row = jnp.where(bkt == bk, (rb_ref[bk, h] - far) * LOG2E, row)
        rows = jnp.broadcast_to(row, (MOBA_BLOCK, BIAS_ROW_W))
        rolled = pltpu.roll(rows, 0, 1, stride=1, stride_axis=0)
        o_ref[h, 0] = rolled[:, MOBA_BLOCK:2 * MOBA_BLOCK]
        o_ref[h, 1] = rolled[:, 2 * MOBA_BLOCK:3 * MOBA_BLOCK]


def _biasprep(rel_bias):
    bkt = jnp.asarray(_bucket_row())
    return pl.pallas_call(
        _biasprep_body,
        grid=(1,),
        in_specs=[
            pl.BlockSpec(memory_space=pltpu.SMEM),
            pl.BlockSpec((1, BIAS_ROW_W), lambda n: (0, 0)),
        ],
        out_specs=pl.BlockSpec((ATTN_HEADS, 2, MOBA_BLOCK, MOBA_BLOCK), lambda n: (0, 0, 0, 0)),
        out_shape=jax.ShapeDtypeStruct((ATTN_HEADS, 2, MOBA_BLOCK, MOBA_BLOCK), F32),
        compiler_params=pltpu.CompilerParams(dimension_semantics=("arbitrary",)),
        name="biasprep",
    )(rel_bias, bkt)


PREP_UNROLL = 8
ITEM_FIELDS = 5
PIPE_UNROLL = 16
PIPE_DEPTH = 3
PIPE_SLOTS = 4


def _far_items(n_blocks):
    never = n_blocks - 1
    items = []
    for i in range(n_blocks):
        n_far = max(i - 1, 0)
        for j0 in range(0, n_far, 2):
            items.append((i, j0, j0, j0 + 1, j0 + 1) if j0 + 1 < n_far else (i, j0, j0, j0, never))
    dummy = (n_blocks - 1, 0, never, 0, never)
    items.extend([dummy] * (-len(items) % PIPE_UNROLL))
    items.extend([dummy] * PIPE_DEPTH)
    return np.asarray(items, np.int32)


def _attn_body(tbl_ref, zq_ref, zk_ref, zv_ref, gq_ref, gk_ref, bias_ref, wup32_ref, wdn32_ref,
               o_ref, wup_ref, wdn_ref,
               kn_ref, vta_ref, w_ref, pen_ref, acc_ref, m_ref, *slots, n_blocks, n_items):
    blk = MOBA_BLOCK
    heads = range(HEADS_PER_STEP)

    wup_ref[...] = wup32_ref[...].astype(BF16)
    wdn_ref[...] = wdn32_ref[...].astype(BF16)

    def rows_of(j):
        return pl.ds(pl.multiple_of(j * blk, blk), blk)

    blk_rows = lax.broadcasted_iota(jnp.int32, (n_blocks, PAIR_W), 0)
    ones = jnp.ones((V_ROWS - HEAD_DIM, blk), BF16)
    same_head = (lax.broadcasted_iota(jnp.int32, (PAIR_W, PAIR_W), 0) // HEAD_DIM
                 == lax.broadcasted_iota(jnp.int32, (PAIR_W, PAIR_W), 1) // HEAD_DIM)
    head_sum = jnp.where(same_head, 1.0, 0.0).astype(BF16)

    def kv_prep(i, kmean):
        kraw = zk_ref[0, rows_of(i), :].astype(F32)
        sq_hi, sq_lo = _split_bf16(kraw * kraw)
        ss = _dot(sq_hi, head_sum) + _dot(sq_lo, head_sum)
        rinv = lax.rsqrt(ss * (1.0 / HEAD_DIM) + EPS)
        kn = kraw * rinv * gk_ref[...]
        kn_ref[rows_of(i), :] = kn.astype(BF16)
        vt = zv_ref[0, rows_of(i), :].astype(F32).T
        for hd in heads:
            vta_ref[i, hd, 0:HEAD_DIM, :] = vt[hd * HEAD_DIM:(hd + 1) * HEAD_DIM].astype(BF16)
            vta_ref[i, hd, HEAD_DIM:V_ROWS, :] = ones
        return jnp.where(blk_rows == i, jnp.mean(kn, axis=0, keepdims=True), kmean)

    kmean = lax.fori_loop(0, n_blocks, kv_prep, jnp.zeros((n_blocks, PAIR_W), F32), unroll=PREP_UNROLL)

    blk_ids = lax.broadcasted_iota(jnp.int32, (n_blocks, blk), 0)
    zeros_half = jnp.zeros((HEAD_DIM, blk), F32)

    def q_prep(i, _):
        qt = zq_ref[0, rows_of(i), :].astype(F32).T
        qsq = qt * qt
        for hd in heads:
            rows = slice(hd * HEAD_DIM, (hd + 1) * HEAD_DIM)
            ss = jnp.sum(qsq[rows], axis=0, keepdims=True)
            r = lax.rsqrt(ss * (1.0 / HEAD_DIM) + EPS)
            qn = qt[rows] * r * gq_ref[rows, :] * (HEAD_DIM ** -0.5 * LOG2E)
            parts = [zeros_half] * HEADS_PER_STEP
            parts[hd] = qn
            w = jnp.concatenate(parts, axis=0)
            w_ref[i, hd] = w.astype(BF16)
            gate = jnp.where(blk_ids < i, _dot3(kmean, w), -jnp.inf)
            sel = jnp.zeros((n_blocks, blk), jnp.bool_)
            for _ in range(MOBA_TOPK):
                mx = jnp.max(gate, axis=0, keepdims=True)
                cand = jnp.where(gate == mx, blk_ids, n_blocks)
                pick = (blk_ids == jnp.min(cand, axis=0, keepdims=True)) & (mx > -jnp.inf)
                sel = sel | pick
                gate = jnp.where(pick, -jnp.inf, gate)
            pen_ref[hd, i] = jnp.where(sel, 0.0, -MASK_BIG).astype(F32)
        return 0

    lax.fori_loop(0, n_blocks, q_prep, 0, unroll=PREP_UNROLL)

    def score(i, blocks, s_ref, biased):
        keys = [kn_ref[rows_of(j), :] for j, _ in blocks]
        gmax = []
        for hd in heads:
            w = w_ref[i, hd]
            tmax = []
            for u, (_, jpen) in enumerate(blocks):
                s = _dot(keys[u], w)
                if biased:
                    s = s + bias_ref[hd, u]
                s_ref[hd, u] = s
                t = jnp.max(s, axis=0, keepdims=True)
                tmax.append(t if jpen is None else t + pen_ref[hd, i, pl.ds(jpen, 1), :])
            gmax.append(functools.reduce(jnp.maximum, tmax))
        return tuple(gmax)

    def attend(i, blocks, s_ref, gmax, first_visit):
        for hd in heads:
            if first_visit:
                m_new = gmax[hd]
                acc = None
            else:
                m_old = m_ref[i, hd]
                m_new = jnp.maximum(m_old, gmax[hd])
                acc = acc_ref[i, hd] * jnp.exp2(m_old - m_new)
            for u, (j, jpen) in enumerate(blocks):
                shift = m_new if jpen is None else m_new - pen_ref[hd, i, pl.ds(jpen, 1), :]
                p = jnp.exp2(s_ref[hd, u] - shift).astype(BF16)
                pv = _dot(vta_ref[j, hd], p)
                acc = pv if acc is None else acc + pv
            acc_ref[i, hd] = acc
            m_ref[i, hd] = m_new

    def near_blocks(i):
        jp = jnp.maximum(i - 1, 0)
        return ((i, None), (jp, jp))

    n_slots = len(slots)

    def near_body(t, g):
        for u in range(PIPE_UNROLL):
            i_cur = PIPE_UNROLL * t + u
            i_nxt = jnp.minimum(i_cur + PIPE_DEPTH, n_blocks - 1)
            g_nxt = score(i_nxt, near_blocks(i_nxt), slots[(u + PIPE_DEPTH) % n_slots], True)
            attend(i_cur, near_blocks(i_cur), slots[u % n_slots], g[0], True)
            g = g[1:] + (g_nxt,)
        return g

    lax.fori_loop(0, n_blocks // PIPE_UNROLL, near_body,
                  tuple(score(i, near_blocks(i), slots[i], True) for i in range(PIPE_DEPTH)))

    def item(k):
        base = k * ITEM_FIELDS
        i = tbl_ref[base]
        return i, ((tbl_ref[base + 1], tbl_ref[base + 2]), (tbl_ref[base + 3], tbl_ref[base + 4]))

    def far_body(t, g):
        for u in range(PIPE_UNROLL):
            k = PIPE_UNROLL * t + u
            g_nxt = score(*item(k + PIPE_DEPTH), slots[(u + PIPE_DEPTH) % n_slots], False)
            attend(*item(k), slots[u % n_slots], g[0], False)
            g = g[1:] + (g_nxt,)
        return g

    lax.fori_loop(0, n_items // PIPE_UNROLL, far_body,
                  tuple(score(*item(k), slots[k], False) for k in range(PIPE_DEPTH)))

    def finish(i, _):
        outs = []
        for hd in heads:
            acc = acc_ref[i, hd]
            outs.append(acc[0:HEAD_DIM] / acc[HEAD_DIM:HEAD_DIM + 1])
        o_ref[0, rows_of(i), :] = jnp.concatenate(outs, axis=0).T.astype(BF16)
        return 0

    lax.fori_loop(0, n_blocks, finish, 0, unroll=PREP_UNROLL)


def _attention(z, gq_t, gk_row, bias, w_up, w_down):
    b, s, _ = z.shape
    n_blocks = s // MOBA_BLOCK
    assert n_blocks % PIPE_UNROLL == 0 and PIPE_UNROLL % PIPE_SLOTS == 0 and PIPE_DEPTH < PIPE_SLOTS
    blk = MOBA_BLOCK
    items = _far_items(n_blocks)
    n_items = items.shape[0] - PIPE_DEPTH
    n_steps = b * N_PAIRS
    up_rows, dn_rows = w_up.shape[0] // n_steps, w_down.shape[0] // n_steps
    assert up_rows * n_steps == w_up.shape[0] and dn_rows * n_steps == w_down.shape[0]
    assert up_rows % BF16_SUBLANES == 0 and dn_rows % BF16_SUBLANES == 0

    def slab(bi, hp, tbl):
        return (bi * N_PAIRS + hp, 0)

    grid_spec = pltpu.PrefetchScalarGridSpec(
        num_scalar_prefetch=1,
        grid=(b, N_PAIRS),
        in_specs=[
            pl.BlockSpec((1, s, PAIR_W), lambda bi, hp, tbl: (bi, 0, hp)),
            pl.BlockSpec((1, s, PAIR_W), lambda bi, hp, tbl: (bi, 0, N_PAIRS + hp)),
            pl.BlockSpec((1, s, PAIR_W), lambda bi, hp, tbl: (bi, 0, 2 * N_PAIRS + hp)),
            pl.BlockSpec((PAIR_W, blk), lambda bi, hp, tbl: (0, 0)),
            pl.BlockSpec((1, PAIR_W), lambda bi, hp, tbl: (0, 0)),
            pl.BlockSpec((HEADS_PER_STEP, 2, blk, blk), lambda bi, hp, tbl: (hp, 0, 0, 0)),
            pl.BlockSpec((up_rows, w_up.shape[1]), slab),
            pl.BlockSpec((dn_rows, w_down.shape[1]), slab),
        ],
        out_specs=[
            pl.BlockSpec((1, s, PAIR_W), lambda bi, hp, tbl: (bi, 0, hp)),
            pl.BlockSpec((up_rows, w_up.shape[1]), slab),
            pl.BlockSpec((dn_rows, w_down.shape[1]), slab),
        ],
        scratch_shapes=[
            pltpu.VMEM((s, PAIR_W), BF16),
            pltpu.VMEM((n_blocks, HEADS_PER_STEP, V_ROWS, blk), BF16),
            pltpu.VMEM((n_blocks, HEADS_PER_STEP, PAIR_W, blk), BF16),
            pltpu.VMEM((HEADS_PER_STEP, n_blocks, n_blocks, blk), F32),
            pltpu.VMEM((n_blocks, HEADS_PER_STEP, V_ROWS, blk), F32),
            pltpu.VMEM((n_blocks, HEADS_PER_STEP, 1, blk), F32),
        ] + [pltpu.VMEM((HEADS_PER_STEP, 2, blk, blk), F32)
             for _ in range(PIPE_SLOTS)],
    )
    return pl.pallas_call(
        functools.partial(_attn_body, n_blocks=n_blocks, n_items=n_items),
        grid_spec=grid_spec,
        out_shape=[
            jax.ShapeDtypeStruct((b, s, ATTN_W), BF16),
            jax.ShapeDtypeStruct(w_up.shape, BF16),
            jax.ShapeDtypeStruct(w_down.shape, BF16),
        ],
        compiler_params=pltpu.CompilerParams(
            dimension_semantics=("arbitrary", "arbitrary"), vmem_limit_bytes=VMEM_LIMIT),
        name="attn",
    )(jnp.asarray(items.reshape(-1)), z, z, z, gq_t, gk_row, bias, w_up, w_down)


def _mix_body(x_ref, attn_ref, p_ref, ga_ref, gp_ref, mod_ref, pw32_ref, ps_ref, wba32_ref, wbp32_ref,
              wo32_ref, g2_ref, x1_ref, h2_ref, pe_ref, pw_ref, wba_ref, wbp_ref, wo_ref):
    for w32, wbf in ((pw32_ref, pw_ref), (wba32_ref, wba_ref), (wbp32_ref, wbp_ref), (wo32_ref, wo_ref)):
        _cast_weight_once(w32, wbf)
    t = pl.program_id(1)
    tm = MIX_TILE
    sub = MIX_TILE // MIX_SPLIT

    @pl.when(t == 0)
    def _():
        pe_ref[0:POOL_HALO, :] = jnp.zeros((POOL_HALO, POOL_W), F32)

    @pl.when(t > 0)
    def _():
        pe_ref[0:POOL_HALO, :] = pe_ref[tm:tm + POOL_HALO, :]

    pe_ref[POOL_HALO:POOL_HALO + tm, :] = p_ref[0].astype(F32)

    for part in range(MIX_SPLIT):
        r0 = part * sub
        rows = slice(r0, r0 + sub)
        pos = t * tm + r0 + lax.broadcasted_iota(jnp.int32, (sub, POOL_GROUP_W), 0)
        pooled = []
        for g, win in enumerate(POOL_WINDOWS):
            cols = slice(g * POOL_GROUP_W, (g + 1) * POOL_GROUP_W)
            ext = pe_ref[r0:r0 + POOL_HALO + sub, cols]
            tot = ext
            span = 1
            while span < win:
                tot = tot + pltpu.roll(tot, span, 0)
                span *= 2
            count = jnp.minimum(pos + 1, win).astype(F32)
            pooled_g = (tot[POOL_HALO:] / count - ext[POOL_HALO:]).astype(BF16)
            pooled.append(_dot(pooled_g, pw_ref[g]) * ps_ref[:, cols])
        pool = jnp.concatenate(pooled, axis=-1).astype(BF16)

        a_proj = _dot(attn_ref[0, rows, :], wba_ref[...])
        p_proj = _dot(pool, wbp_ref[...])
        merged = (_sigmoid(ga_ref[0, rows, :].astype(F32)) * a_proj
                  + _sigmoid(gp_ref[0, rows, :].astype(F32)) * p_proj).astype(BF16)
        x1 = x_ref[0, rows, :] + mod_ref[0, 0:1, :] * _dot(merged, wo_ref[...])
        x1_ref[0, rows, :] = x1
        h2_ref[0, rows, :] = _norm_mod(x1, g2_ref[...], mod_ref[0, 2:3, :], mod_ref[0, 1:2, :]).astype(BF16)


def _mix(x, attn, z, mod, pool_w, pool_scale, wba, wbp, wo, g2):
    b, s, d = x.shape
    tm = MIX_TILE
    const2 = lambda bi, t: (0, 0)
    return pl.pallas_call(
        _mix_body,
        grid=(b, s // tm),
        in_specs=[
            pl.BlockSpec((1, tm, d), lambda bi, t: (bi, t, 0)),
            pl.BlockSpec((1, tm, ATTN_W), lambda bi, t: (bi, t, 0)),
            pl.BlockSpec((1, tm, POOL_W), lambda bi, t: (bi, t, 3 * ATTN_W // POOL_W)),
            pl.BlockSpec((1, tm, d), lambda bi, t: (bi, t, (3 * ATTN_W + POOL_W) // d)),
            pl.BlockSpec((1, tm, d), lambda bi, t: (bi, t, (3 * ATTN_W + POOL_W) // d + 1)),
            pl.BlockSpec((1, N_MOD - N_MOD_EARLY, d), lambda bi, t: (bi, 0, 0)),
            pl.BlockSpec((POOL_GROUPS, POOL_GROUP_W, POOL_GROUP_W), lambda bi, t: (0, 0, 0)),
            pl.BlockSpec((1, POOL_W), const2),
            pl.BlockSpec((ATTN_W, d), const2),
            pl.BlockSpec((POOL_W, d), const2),
            pl.BlockSpec((d, d), const2),
            pl.BlockSpec((1, d), const2),
        ],
        out_specs=[
            pl.BlockSpec((1, tm, d), lambda bi, t: (bi, t, 0)),
            pl.BlockSpec((1, tm, d), lambda bi, t: (bi, t, 0)),
        ],
        out_shape=[
            jax.ShapeDtypeStruct((b, s, d), F32),
            jax.ShapeDtypeStruct((b, s, d), BF16),
        ],
        scratch_shapes=[
            pltpu.VMEM((tm + POOL_HALO, POOL_W), F32),
            pltpu.VMEM(pool_w.shape, BF16),
            pltpu.VMEM(wba.shape, BF16),
            pltpu.VMEM(wbp.shape, BF16),
            pltpu.VMEM(wo.shape, BF16),
        ],
        compiler_params=pltpu.CompilerParams(
            dimension_semantics=("arbitrary", "arbitrary"), vmem_limit_bytes=VMEM_LIMIT),
        name="mix",
    )(x, attn, z, z, z, mod, pool_w, pool_scale, wba, wbp, wo, g2)


def _ffn_body(x1_ref, h2_ref, mod_ref, wup_ref, cw_ref, cb_ref, wdn_ref, o_ref, carry_ref, *hist):
    t = pl.program_id(1)
    sub = FFN_TILE // FFN_SPLIT
    fc = FFN_CHUNK
    n_chunks = D_FF // fc

    @pl.when(t == 0)
    def _():
        carry_ref[...] = jnp.zeros_like(carry_ref)

    units = [(part, c) for part in range(FFN_SPLIT) for c in range(n_chunks)]
    h2 = [h2_ref[0, part * sub:(part + 1) * sub, :] for part in range(FFN_SPLIT)]

    def up(k):
        part, c = units[k]
        out = []
        for half in range(2):
            cols = slice(half * D_FF + c * fc, half * D_FF + (c + 1) * fc)
            hcols = slice(half * fc, (half + 1) * fc)
            r = _dot(h2[part], wup_ref[:, cols])
            hist[k % len(hist)][0:CONV_HALO, hcols] = carry_ref[:, cols]
            hist[k % len(hist)][CONV_HALO:CONV_HALO + sub, hcols] = r
            carry_ref[:, cols] = r[sub - CONV_HALO:sub]
            out.append(r)
        return out

    def down(acc, k, act):
        part, c = units[k]
        acc[part] = acc[part] + _dot(act, wdn_ref[c * fc:(c + 1) * fc, :])

    acc = [jnp.zeros((sub, D_MODEL), F32) for _ in range(FFN_SPLIT)]
    ahead = [up(k) for k in range(FFN_AHEAD)]
    act_prev = None
    for k, (part, c) in enumerate(units):
        r_cur = ahead.pop(0)
        if k + FFN_AHEAD < len(units):
            ahead.append(up(k + FFN_AHEAD))
        if act_prev is not None:
            down(acc, k - 1, act_prev)
        halves = []
        for half in range(2):
            cols = slice(half * D_FF + c * fc, half * D_FF + (c + 1) * fc)
            hcols = slice(half * fc, (half + 1) * fc)
            u = cb_ref[:, cols] + cw_ref[CONV_W - 1:CONV_W, cols] * r_cur[half]
            for tap in range(CONV_W - 1):
                back = CONV_W - 1 - tap
                u = u + cw_ref[tap:tap + 1, cols] * hist[k % len(hist)][CONV_HALO - back:CONV_HALO - back + sub, hcols]
            halves.append(u)
        u_g, u_v = halves
        act_prev = (u_g * _sigmoid(u_g) * u_v).astype(BF16)
    down(acc, len(units) - 1, act_prev)
    for part in range(FFN_SPLIT):
        rows = slice(part * sub, (part + 1) * sub)
        o_ref[0, rows, :] = x1_ref[0, rows, :] + mod_ref[0, 3:4, :] * acc[part]


def _ffn(x1, h2, mod, wup_bf, conv_w, conv_b, wdn_bf):
    b, s, d = x1.shape
    tm = FFN_TILE
    const2 = lambda bi, t: (0, 0)
    return pl.pallas_call(
        _ffn_body,
        grid=(b, s // tm),
        in_specs=[
            pl.BlockSpec((1, tm, d), lambda bi, t: (bi, t, 0)),
            pl.BlockSpec((1, tm, d), lambda bi, t: (bi, t, 0)),
            pl.BlockSpec((1, N_MOD - N_MOD_EARLY, d), lambda bi, t: (bi, 0, 0)),
            pl.BlockSpec((d, 2 * D_FF), const2),
            pl.BlockSpec((CONV_W, 2 * D_FF), const2),
            pl.BlockSpec((1, 2 * D_FF), const2),
            pl.BlockSpec((D_FF, d), const2),
        ],
        out_specs=pl.BlockSpec((1, tm, d), lambda bi, t: (bi, t, 0)),
        out_shape=jax.ShapeDtypeStruct((b, s, d), F32),
        scratch_shapes=[
            pltpu.VMEM((CONV_HALO, 2 * D_FF), F32),
        ] + [pltpu.VMEM((tm // FFN_SPLIT + CONV_HALO, 2 * FFN_CHUNK), F32)
             for _ in range(FFN_AHEAD + 1)],
        compiler_params=pltpu.CompilerParams(
            dimension_semantics=("arbitrary", "arbitrary"), vmem_limit_bytes=VMEM_LIMIT),
        name="ffn",
    )(x1, h2, mod, wup_bf, conv_w, conv_b, wdn_bf)


def _layer(x, c_lanes, rel_bias, ada_w, ada_b, norm1_g, w_in, q_norm_g, k_norm_g, pool_w, pool_scale,
           w_branch_attn, w_branch_pool, w_out, norm2_g, w_up, conv_w, conv_b, w_down):
    b, s, d = x.shape
    mod_early = _ada_early(c_lanes, ada_w, ada_b[None, :])
    z, mod = _inproj(x, mod_early, norm1_g[None, :], w_in, c_lanes, ada_w, ada_b[None, :])
    gq_t = jnp.broadcast_to(jnp.tile(q_norm_g, HEADS_PER_STEP)[:, None], (PAIR_W, MOBA_BLOCK))
    gk_row = jnp.tile(k_norm_g, HEADS_PER_STEP)[None, :]
    attn, wup_bf, wdn_bf = _attention(z, gq_t, gk_row, _biasprep(rel_bias), w_up, w_down)
    x1, h2 = _mix(x, attn, z, mod, pool_w, pool_scale[None, :], w_branch_attn, w_branch_pool, w_out,
                  norm2_g[None, :])
    return _ffn(x1, h2, mod, wup_bf, conv_w, conv_b[None, :], wdn_bf)


def kernel(x, c, ada_w, ada_b, norm1_g, w_in, q_norm_g, k_norm_g, rel_bias, pool_w, pool_scale,
           w_branch_attn, w_branch_pool, w_out, norm2_g, w_up, conv_w, conv_b, w_down):
    b, s, d = x.shape
    assert d == D_MODEL and w_in.shape[-1] == IN_W
    assert all(s % tile == 0 for tile in (MOBA_BLOCK * PIPE_UNROLL, INPROJ_TILE, MIX_TILE, FFN_TILE))
    c_lanes = jnp.broadcast_to(c[:, :, None], (b, d, LANES))
    for l in range(ada_w.shape[0]):
        x = _layer(x, c_lanes, rel_bias, ada_w[l], ada_b[l], norm1_g[l], w_in[l], q_norm_g[l],
                   k_norm_g[l], pool_w[l], pool_scale[l], w_branch_attn[l], w_branch_pool[l],
                   w_out[l], norm2_g[l], w_up[l], conv_w[l], conv_b[l], w_down[l])
    return x
```

```python
import functools
import math

import numpy as np
import jax
import jax.numpy as jnp
from jax import lax
from jax.experimental import pallas as pl
from jax.experimental.pallas import tpu as pltpu

F32 = jnp.float32
BF16 = jnp.bfloat16

D_MODEL = 1024
ATTN_HEADS = 8
HEAD_DIM = 64
ATTN_W = ATTN_HEADS * HEAD_DIM
MOBA_BLOCK = 256
MOBA_TOPK = 3
POOL_GROUPS = 4
POOL_GROUP_W = 128
POOL_W = POOL_GROUPS * POOL_GROUP_W
POOL_WINDOWS = (2, 4, 8, 16)
NUM_BUCKETS = 32
MAX_DISTANCE = 128
D_FF = 2816
CONV_W = 3
EPS = 1e-6
IN_W = 3 * ATTN_W + POOL_W + 2 * D_MODEL

LANES = 128
BF16_SUBLANES = 16
HEADS_PER_STEP = LANES // HEAD_DIM
PAIR_W = HEADS_PER_STEP * HEAD_DIM
N_PAIRS = ATTN_HEADS // HEADS_PER_STEP
V_ROWS = HEAD_DIM + BF16_SUBLANES
LOG2E = math.log2(math.e)
MASK_BIG = 1e30
ADA_ROWS = 256
INPROJ_TILE = 1024
INPROJ_COLS = 1024
INPROJ_SPLIT = 4
POOL_HALO = 16
MIX_TILE = 512
MIX_SPLIT = 2
FFN_CHUNK = 256
FFN_TILE = 512
FFN_SPLIT = 2
FFN_AHEAD = 3
CONV_HALO = 8
VMEM_LIMIT = 56 * 1024 * 1024


def _split_bf16(a):
    hi = a.astype(BF16)
    lo = (a - hi.astype(F32)).astype(BF16)
    return hi, lo


def _dot(a, b):
    return jnp.dot(a, b, preferred_element_type=F32)


def _dot3(a, b):
    ah, al = _split_bf16(a)
    bh, bl = _split_bf16(b)
    return _dot(ah, bh) + _dot(al, bh) + _dot(ah, bl)


def _sigmoid(v):
    return 1.0 / (1.0 + jnp.exp2(v * (-LOG2E)))


def _cast_weight_once(src_ref, dst_ref, col_chunk=4 * LANES):
    first = functools.reduce(jnp.logical_and, [pl.program_id(a) == 0 for a in range(2)])

    @pl.when(first)
    def _():
        n = src_ref.shape[-1]
        for c0 in range(0, n, col_chunk):
            dst_ref[..., c0:c0 + col_chunk] = src_ref[..., c0:c0 + col_chunk].astype(BF16)


N_MOD = 6
N_MOD_EARLY = 2


def _ada_accumulate(c_ref, w_refs, b_refs, o_ref, first):
    n_batch, _, lanes = c_ref.shape
    dm = o_ref.shape[2]
    vectors = [(w_ref, b_ref, j) for w_ref, b_ref in zip(w_refs, b_refs) for j in range(w_ref.shape[1] // dm)]

    @pl.when(first)
    def _():
        for v, (_, b_ref, j) in enumerate(vectors):
            o_ref[:, v, :] = jnp.broadcast_to(b_ref[:, j * dm:(j + 1) * dm], (n_batch, dm))

    for bi in range(n_batch):
        c = c_ref[bi]
        act = c * _sigmoid(c)
        for v, (w_ref, _, j) in enumerate(vectors):
            cols = [jnp.sum(act * w_ref[:, j * dm + g * lanes:j * dm + (g + 1) * lanes], axis=0, keepdims=True)
                    for g in range(dm // lanes)]
            o_ref[bi, v:v + 1, :] += jnp.concatenate(cols, axis=-1)


def _ada_body(c_ref, w_ref, b_ref, o_ref):
    _ada_accumulate(c_ref, [w_ref], [b_ref], o_ref, pl.program_id(0) == 0)


def _ada_early(c_lanes, ada_w, ada_b):
    n_batch, d, lanes = c_lanes.shape
    n = N_MOD_EARLY * d
    tk = ADA_ROWS
    return pl.pallas_call(
        _ada_body,
        grid=(d // tk,),
        in_specs=[
            pl.BlockSpec((n_batch, tk, lanes), lambda k: (0, k, 0)),
            pl.BlockSpec((tk, n), lambda k: (k, 0)),
            pl.BlockSpec((1, n), lambda k: (0, 0)),
        ],
        out_specs=pl.BlockSpec((n_batch, N_MOD_EARLY, d), lambda k: (0, 0, 0)),
        out_shape=jax.ShapeDtypeStruct((n_batch, N_MOD_EARLY, d), F32),
        compiler_params=pltpu.CompilerParams(
            dimension_semantics=("arbitrary",), vmem_limit_bytes=VMEM_LIMIT),
        name="ada",
    )(c_lanes, ada_w, ada_b)


def _norm_mod(x, gain, scale, shift):
    ms = jnp.mean(x * x, axis=-1, keepdims=True)
    return x * lax.rsqrt(ms + EPS) * (gain * (1.0 + scale)) + shift


def _inproj_body(x_ref, mod_ref, g_ref, w32_ref, c_ref, aw0_ref, aw1_ref, ab0_ref, ab1_ref, z_ref, mod_late_ref,
                 w_ref):
    _cast_weight_once(w32_ref, w_ref)
    first = jnp.logical_and(pl.program_id(0) == 0, pl.program_id(1) == 0)
    _ada_accumulate(c_ref, [aw0_ref, aw1_ref], [ab0_ref, ab1_ref], mod_late_ref, first)
    sub = INPROJ_TILE // INPROJ_SPLIT
    n_chunk = INPROJ_COLS
    for part in range(INPROJ_SPLIT):
        rows = slice(part * sub, (part + 1) * sub)
        h = _norm_mod(x_ref[0, rows, :], g_ref[...], mod_ref[0, 1:2, :], mod_ref[0, 0:1, :]).astype(BF16)
        for n in range(IN_W // n_chunk):
            cols = slice(n * n_chunk, (n + 1) * n_chunk)
            z_ref[0, rows, cols] = _dot(h, w_ref[:, cols]).astype(BF16)


def _inproj(x, mod_early, g1, w_in, c_lanes, ada_w, ada_b):
    b, s, d = x.shape
    tiles = s // INPROJ_TILE
    n_late = N_MOD - N_MOD_EARLY
    pair = 2 * d
    assert N_MOD_EARLY * d == pair and n_late * d == 2 * pair
    slab_rows = d // (b * tiles)
    assert slab_rows * b * tiles == d and slab_rows % 8 == 0
    slab = lambda bi, t: bi * tiles + t
    return pl.pallas_call(
        _inproj_body,
        grid=(b, tiles),
        in_specs=[
            pl.BlockSpec((1, INPROJ_TILE, d), lambda bi, t: (bi, t, 0)),
            pl.BlockSpec((1, N_MOD_EARLY, d), lambda bi, t: (bi, 0, 0)),
            pl.BlockSpec((1, d), lambda bi, t: (0, 0)),
            pl.BlockSpec((d, IN_W), lambda bi, t: (0, 0)),
            pl.BlockSpec((b, slab_rows, c_lanes.shape[2]), lambda bi, t: (0, slab(bi, t), 0)),
            pl.BlockSpec((slab_rows, pair), lambda bi, t: (slab(bi, t), 1)),
            pl.BlockSpec((slab_rows, pair), lambda bi, t: (slab(bi, t), 2)),
            pl.BlockSpec((1, pair), lambda bi, t: (0, 1)),
            pl.BlockSpec((1, pair), lambda bi, t: (0, 2)),
        ],
        out_specs=[
            pl.BlockSpec((1, INPROJ_TILE, IN_W), lambda bi, t: (bi, t, 0)),
            pl.BlockSpec((b, n_late, d), lambda bi, t: (0, 0, 0)),
        ],
        out_shape=[
            jax.ShapeDtypeStruct((b, s, IN_W), BF16),
            jax.ShapeDtypeStruct((b, n_late, d), F32),
        ],
        scratch_shapes=[pltpu.VMEM((d, IN_W), BF16)],
        compiler_params=pltpu.CompilerParams(
            dimension_semantics=("arbitrary", "arbitrary"), vmem_limit_bytes=VMEM_LIMIT),
        name="inproj",
    )(x, mod_early, g1, w_in, c_lanes, ada_w, ada_w, ada_b, ada_b)


def _t5_bucket_np(dist):
    max_exact = NUM_BUCKETS // 2
    n = np.maximum(dist, 0)
    nf = np.maximum(n, 1).astype(np.float64)
    large = max_exact + (np.log(nf / max_exact) / math.log(MAX_DISTANCE / max_exact)
                         * (NUM_BUCKETS - max_exact)).astype(np.int32)
    large = np.minimum(large, NUM_BUCKETS - 1)
    return np.where(n < max_exact, n, large).astype(np.int32)


BIAS_ROW_W = 4 * MOBA_BLOCK


def _bucket_row():
    d = np.arange(BIAS_ROW_W) - MOBA_BLOCK
    return np.where(d >= 0, _t5_bucket_np(d), -1).astype(np.int32)[None, :]


def _biasprep_body(rb_ref, bkt_ref, o_ref):
    bkt = bkt_ref[...]
    for h in range(ATTN_HEADS):
        far = rb_ref[NUM_BUCKETS - 1, h]
        row = jnp.where(bkt < 0, -MASK_BIG, 0.0).astype(F32)
        for bk in range(NUM_BUCKETS):
            row = jnp.where(bkt == bk, (rb_ref[bk, h] - far) * LOG2E, row)
        rows = jnp.broadcast_to(row, (MOBA_BLOCK, BIAS_ROW_W))
        rolled = pltpu.roll(rows, 0, 1, stride=1, stride_axis=0)
        o_ref[h, 0] = rolled[:, MOBA_BLOCK:2 * MOBA_BLOCK]
        o_ref[h, 1] = rolled[:, 2 * MOBA_BLOCK:3 * MOBA_BLOCK]


def _biasprep(rel_bias):
    bkt = jnp.asarray(_bucket_row())
    return pl.pallas_call(
        _biasprep_body,
        grid=(1,),
        in_specs=[
            pl.BlockSpec(memory_space=pltpu.SMEM),
            pl.BlockSpec((1, BIAS_ROW_W), lambda n: (0, 0)),
        ],
        out_specs=pl.BlockSpec((ATTN_HEADS, 2, MOBA_BLOCK, MOBA_BLOCK), lambda n: (0, 0, 0, 0)),
        out_shape=jax.ShapeDtypeStruct((ATTN_HEADS, 2, MOBA_BLOCK, MOBA_BLOCK), F32),
        compiler_params=pltpu.CompilerParams(dimension_semantics=("arbitrary",)),
        name="biasprep",
    )(rel_bias, bkt)


PREP_UNROLL = 8
ITEM_FIELDS = 5
PIPE_UNROLL = 16
PIPE_DEPTH = 3
PIPE_SLOTS = 4


def _far_items(n_blocks):
    never = n_blocks - 1
    items = []
    for i in range(n_blocks):
        n_far = max(i - 1, 0)
        for j0 in range(0, n_far, 2):
            items.append((i, j0, j0, j0 + 1, j0 + 1) if j0 + 1 < n_far else (i, j0, j0, j0, never))
    dummy = (n_blocks - 1, 0, never, 0, never)
    items.extend([dummy] * (-len(items) % PIPE_UNROLL))
    items.extend([dummy] * PIPE_DEPTH)
    return np.asarray(items, np.int32)


def _attn_body(tbl_ref, zq_ref, zk_ref, zv_ref, gq_ref, gk_ref, bias_ref, wup32_ref, wdn32_ref,
               o_ref, wup_ref, wdn_ref,
               kn_ref, vta_ref, w_ref, pen_ref, acc_ref, m_ref, *slots, n_blocks, n_items):
    blk = MOBA_BLOCK
    heads = range(HEADS_PER_STEP)

    wup_ref[...] = wup32_ref[...].astype(BF16)
    wdn_ref[...] = wdn32_ref[...].astype(BF16)

    def rows_of(j):
        return pl.ds(pl.multiple_of(j * blk, blk), blk)

    blk_rows = lax.broadcasted_iota(jnp.int32, (n_blocks, PAIR_W), 0)
    ones = jnp.ones((V_ROWS - HEAD_DIM, blk), BF16)
    same_head = (lax.broadcasted_iota(jnp.int32, (PAIR_W, PAIR_W), 0) // HEAD_DIM
                 == lax.broadcasted_iota(jnp.int32, (PAIR_W, PAIR_W), 1) // HEAD_DIM)
    head_sum = jnp.where(same_head, 1.0, 0.0).astype(BF16)

    def kv_prep(i, kmean):
        kraw = zk_ref[0, rows_of(i), :].astype(F32)
        sq_hi, sq_lo = _split_bf16(kraw * kraw)
        ss = _dot(sq_hi, head_sum) + _dot(sq_lo, head_sum)
        rinv = lax.rsqrt(ss * (1.0 / HEAD_DIM) + EPS)
        kn = kraw * rinv * gk_ref[...]
        kn_ref[rows_of(i), :] = kn.astype(BF16)
        vt = zv_ref[0, rows_of(i), :].astype(F32).T
        for hd in heads:
            vta_ref[i, hd, 0:HEAD_DIM, :] = vt[hd * HEAD_DIM:(hd + 1) * HEAD_DIM].astype(BF16)
            vta_ref[i, hd, HEAD_DIM:V_ROWS, :] = ones
        return jnp.where(blk_rows == i, jnp.mean(kn, axis=0, keepdims=True), kmean)

    kmean = lax.fori_loop(0, n_blocks, kv_prep, jnp.zeros((n_blocks, PAIR_W), F32), unroll=PREP_UNROLL)

    blk_ids = lax.broadcasted_iota(jnp.int32, (n_blocks, blk), 0)
    zeros_half = jnp.zeros((HEAD_DIM, blk), F32)

    def q_prep(i, _):
        qt = zq_ref[0, rows_of(i), :].astype(F32).T
        qsq = qt * qt
        for hd in heads:
            rows = slice(hd * HEAD_DIM, (hd + 1) * HEAD_DIM)
            ss = jnp.sum(qsq[rows], axis=0, keepdims=True)
            r = lax.rsqrt(ss * (1.0 / HEAD_DIM) + EPS)
            qn = qt[rows] * r * gq_ref[rows, :] * (HEAD_DIM ** -0.5 * LOG2E)
            parts = [zeros_half] * HEADS_PER_STEP
            parts[hd] = qn
            w = jnp.concatenate(parts, axis=0)
            w_ref[i, hd] = w.astype(BF16)
            gate = jnp.where(blk_ids < i, _dot3(kmean, w), -jnp.inf)
            sel = jnp.zeros((n_blocks, blk), jnp.bool_)
            for _ in range(MOBA_TOPK):
                mx = jnp.max(gate, axis=0, keepdims=True)
                cand = jnp.where(gate == mx, blk_ids, n_blocks)
                pick = (blk_ids == jnp.min(cand, axis=0, keepdims=True)) & (mx > -jnp.inf)
                sel = sel | pick
                gate = jnp.where(pick, -jnp.inf, gate)
            pen_ref[hd, i] = jnp.where(sel, 0.0, -MASK_BIG).astype(F32)
        return 0

    lax.fori_loop(0, n_blocks, q_prep, 0, unroll=PREP_UNROLL)

    def score(i, blocks, s_ref, biased):
        keys = [kn_ref[rows_of(j), :] for j, _ in blocks]
        gmax = []
        for hd in heads:
            w = w_ref[i, hd]
            tmax = []
            for u, (_, jpen) in enumerate(blocks):
                s = _dot(keys[u], w)
                if biased:
                    s = s + bias_ref[hd, u]
                s_ref[hd, u] = s
                t = jnp.max(s, axis=0, keepdims=True)
                tmax.append(t if jpen is None else t + pen_ref[hd, i, pl.ds(jpen, 1), :])
            gmax.append(functools.reduce(jnp.maximum, tmax))
        return tuple(gmax)

    def attend(i, blocks, s_ref, gmax, first_visit):
        for hd in heads:
            if first_visit:
                m_new = gmax[hd]
                acc = None
            else:
                m_old = m_ref[i, hd]
                m_new = jnp.maximum(m_old, gmax[hd])
                acc = acc_ref[i, hd] * jnp.exp2(m_old - m_new)
            for u, (j, jpen) in enumerate(blocks):
                shift = m_new if jpen is None else m_new - pen_ref[hd, i, pl.ds(jpen, 1), :]
                p = jnp.exp2(s_ref[hd, u] - shift).astype(BF16)
                pv = _dot(vta_ref[j, hd], p)
                acc = pv if acc is None else acc + pv
            acc_ref[i, hd] = acc
            m_ref[i, hd] = m_new

    def near_blocks(i):
        jp = jnp.maximum(i - 1, 0)
        return ((i, None), (jp, jp))

    n_slots = len(slots)

    def near_body(t, g):
        for u in range(PIPE_UNROLL):
            i_cur = PIPE_UNROLL * t + u
            i_nxt = jnp.minimum(i_cur + PIPE_DEPTH, n_blocks - 1)
            g_nxt = score(i_nxt, near_blocks(i_nxt), slots[(u + PIPE_DEPTH) % n_slots], True)
            attend(i_cur, near_blocks(i_cur), slots[u % n_slots], g[0], True)
            g = g[1:] + (g_nxt,)
        return g

    lax.fori_loop(0, n_blocks // PIPE_UNROLL, near_body,
                  tuple(score(i, near_blocks(i), slots[i], True) for i in range(PIPE_DEPTH)))

    def item(k):
        base = k * ITEM_FIELDS
        i = tbl_ref[base]
        return i, ((tbl_ref[base + 1], tbl_ref[base + 2]), (tbl_ref[base + 3], tbl_ref[base + 4]))

    def far_body(t, g):
        for u in range(PIPE_UNROLL):
            k = PIPE_UNROLL * t + u
            g_nxt = score(*item(k + PIPE_DEPTH), slots[(u + PIPE_DEPTH) % n_slots], False)
            attend(*item(k), slots[u % n_slots], g[0], False)
            g = g[1:] + (g_nxt,)
        return g

    lax.fori_loop(0, n_items // PIPE_UNROLL, far_body,
                  tuple(score(*item(k), slots[k], False) for k in range(PIPE_DEPTH)))

    def finish(i, _):
        outs = []
        for hd in heads:
            acc = acc_ref[i, hd]
            outs.append(acc[0:HEAD_DIM] / acc[HEAD_DIM:HEAD_DIM + 1])
        o_ref[0, rows_of(i), :] = jnp.concatenate(outs, axis=0).T.astype(BF16)
        return 0

    lax.fori_loop(0, n_blocks, finish, 0, unroll=PREP_UNROLL)


def _attention(z, gq_t, gk_row, bias, w_up, w_down):
    b, s, _ = z.shape
    n_blocks = s // MOBA_BLOCK
    assert n_blocks % PIPE_UNROLL == 0 and PIPE_UNROLL % PIPE_SLOTS == 0 and PIPE_DEPTH < PIPE_SLOTS
    blk = MOBA_BLOCK
    items = _far_items(n_blocks)
    n_items = items.shape[0] - PIPE_DEPTH
    n_steps = b * N_PAIRS
    up_rows, dn_rows = w_up.shape[0] // n_steps, w_down.shape[0] // n_steps
    assert up_rows * n_steps == w_up.shape[0] and dn_rows * n_steps == w_down.shape[0]
    assert up_rows % BF16_SUBLANES == 0 and dn_rows % BF16_SUBLANES == 0

    def slab(bi, hp, tbl):
        return (bi * N_PAIRS + hp, 0)

    grid_spec = pltpu.PrefetchScalarGridSpec(
        num_scalar_prefetch=1,
        grid=(b, N_PAIRS),
        in_specs=[
            pl.BlockSpec((1, s, PAIR_W), lambda bi, hp, tbl: (bi, 0, hp)),
            pl.BlockSpec((1, s, PAIR_W), lambda bi, hp, tbl: (bi, 0, N_PAIRS + hp)),
            pl.BlockSpec((1, s, PAIR_W), lambda bi, hp, tbl: (bi, 0, 2 * N_PAIRS + hp)),
            pl.BlockSpec((PAIR_W, blk), lambda bi, hp, tbl: (0, 0)),
            pl.BlockSpec((1, PAIR_W), lambda bi, hp, tbl: (0, 0)),
            pl.BlockSpec((HEADS_PER_STEP, 2, blk, blk), lambda bi, hp, tbl: (hp, 0, 0, 0)),
            pl.BlockSpec((up_rows, w_up.shape[1]), slab),
            pl.BlockSpec((dn_rows, w_down.shape[1]), slab),
        ],
        out_specs=[
            pl.BlockSpec((1, s, PAIR_W), lambda bi, hp, tbl: (bi, 0, hp)),
            pl.BlockSpec((up_rows, w_up.shape[1]), slab),
            pl.BlockSpec((dn_rows, w_down.shape[1]), slab),
        ],
        scratch_shapes=[
            pltpu.VMEM((s, PAIR_W), BF16),
            pltpu.VMEM((n_blocks, HEADS_PER_STEP, V_ROWS, blk), BF16),
            pltpu.VMEM((n_blocks, HEADS_PER_STEP, PAIR_W, blk), BF16),
            pltpu.VMEM((HEADS_PER_STEP, n_blocks, n_blocks, blk), F32),
            pltpu.VMEM((n_blocks, HEADS_PER_STEP, V_ROWS, blk), F32),
            pltpu.VMEM((n_blocks, HEADS_PER_STEP, 1, blk), F32),
        ] + [pltpu.VMEM((HEADS_PER_STEP, 2, blk, blk), F32)
             for _ in range(PIPE_SLOTS)],
    )
    return pl.pallas_call(
        functools.partial(_attn_body, n_blocks=n_blocks, n_items=n_items),
        grid_spec=grid_spec,
        out_shape=[
            jax.ShapeDtypeStruct((b, s, ATTN_W), BF16),
            jax.ShapeDtypeStruct(w_up.shape, BF16),
            jax.ShapeDtypeStruct(w_down.shape, BF16),
        ],
        compiler_params=pltpu.CompilerParams(
            dimension_semantics=("arbitrary", "arbitrary"), vmem_limit_bytes=VMEM_LIMIT),
        name="attn",
    )(jnp.asarray(items.reshape(-1)), z, z, z, gq_t, gk_row, bias, w_up, w_down)


def _mix_body(x_ref, attn_ref, p_ref, ga_ref, gp_ref, mod_ref, pw32_ref, ps_ref, wba32_ref, wbp32_ref,
              wo32_ref, g2_ref, x1_ref, h2_ref, pe_ref, pw_ref, wba_ref, wbp_ref, wo_ref):
    for w32, wbf in ((pw32_ref, pw_ref), (wba32_ref, wba_ref), (wbp32_ref, wbp_ref), (wo32_ref, wo_ref)):
        _cast_weight_once(w32, wbf)
    t = pl.program_id(1)
    tm = MIX_TILE
    sub = MIX_TILE // MIX_SPLIT

    @pl.when(t == 0)
    def _():
        pe_ref[0:POOL_HALO, :] = jnp.zeros((POOL_HALO, POOL_W), F32)

    @pl.when(t > 0)
    def _():
        pe_ref[0:POOL_HALO, :] = pe_ref[tm:tm + POOL_HALO, :]

    pe_ref[POOL_HALO:POOL_HALO + tm, :] = p_ref[0].astype(F32)

    parts = range(MIX_SPLIT)
    rows = [slice(part * sub, (part + 1) * sub) for part in parts]

    def pool_inputs(part):
        r0 = part * sub
        pos = t * tm + r0 + lax.broadcasted_iota(jnp.int32, (sub, POOL_GROUP_W), 0)
        out = []
        for g, win in enumerate(POOL_WINDOWS):
            cols = slice(g * POOL_GROUP_W, (g + 1) * POOL_GROUP_W)
            ext = pe_ref[r0:r0 + POOL_HALO + sub, cols]
            tot = ext
            span = 1
            while span < win:
                tot = tot + pltpu.roll(tot, span, 0)
                span *= 2
            count = jnp.minimum(pos + 1, win).astype(F32)
            out.append((tot[POOL_HALO:] / count - ext[POOL_HALO:]).astype(BF16))
        return out

    a_proj = [_dot(attn_ref[0, rows[part], :], wba_ref[...]) for part in parts]
    pooled_in = [pool_inputs(part) for part in parts]
    pool = []
    for part in parts:
        pooled = [_dot(pooled_in[part][g], pw_ref[g]) * ps_ref[:, g * POOL_GROUP_W:(g + 1) * POOL_GROUP_W]
                  for g in range(POOL_GROUPS)]
        pool.append(jnp.concatenate(pooled, axis=-1).astype(BF16))
    merged = []
    for part in parts:
        p_proj = _dot(pool[part], wbp_ref[...])
        merged.append((_sigmoid(ga_ref[0, rows[part], :].astype(F32)) * a_proj[part]
                       + _sigmoid(gp_ref[0, rows[part], :].astype(F32)) * p_proj).astype(BF16))
    for part in parts:
        x1 = x_ref[0, rows[part], :] + mod_ref[0, 0:1, :] * _dot(merged[part], wo_ref[...])
        x1_ref[0, rows[part], :] = x1
        h2_ref[0, rows[part], :] = _norm_mod(x1, g2_ref[...], mod_ref[0, 2:3, :],
                                             mod_ref[0, 1:2, :]).astype(BF16)


def _mix(x, attn, z, mod, pool_w, pool_scale, wba, wbp, wo, g2):
    b, s, d = x.shape
    tm = MIX_TILE
    const2 = lambda bi, t: (0, 0)
    return pl.pallas_call(
        _mix_body,
        grid=(b, s // tm),
        in_specs=[
            pl.BlockSpec((1, tm, d), lambda bi, t: (bi, t, 0)),
            pl.BlockSpec((1, tm, ATTN_W), lambda bi, t: (bi, t, 0)),
            pl.BlockSpec((1, tm, POOL_W), lambda bi, t: (bi, t, 3 * ATTN_W // POOL_W)),
            pl.BlockSpec((1, tm, d), lambda bi, t: (bi, t, (3 * ATTN_W + POOL_W) // d)),
            pl.BlockSpec((1, tm, d), lambda bi, t: (bi, t, (3 * ATTN_W + POOL_W) // d + 1)),
            pl.BlockSpec((1, N_MOD - N_MOD_EARLY, d), lambda bi, t: (bi, 0, 0)),
            pl.BlockSpec((POOL_GROUPS, POOL_GROUP_W, POOL_GROUP_W), lambda bi, t: (0, 0, 0)),
            pl.BlockSpec((1, POOL_W), const2),
            pl.BlockSpec((ATTN_W, d), const2),
            pl.BlockSpec((POOL_W, d), const2),
            pl.BlockSpec((d, d), const2),
            pl.BlockSpec((1, d), const2),
        ],
        out_specs=[
            pl.BlockSpec((1, tm, d), lambda bi, t: (bi, t, 0)),
            pl.BlockSpec((1, tm, d), lambda bi, t: (bi, t, 0)),
        ],
        out_shape=[
            jax.ShapeDtypeStruct((b, s, d), F32),
            jax.ShapeDtypeStruct((b, s, d), BF16),
        ],
        scratch_shapes=[
            pltpu.VMEM((tm + POOL_HALO, POOL_W), F32),
            pltpu.VMEM(pool_w.shape, BF16),
            pltpu.VMEM(wba.shape, BF16),
            pltpu.VMEM(wbp.shape, BF16),
            pltpu.VMEM(wo.shape, BF16),
        ],
        compiler_params=pltpu.CompilerParams(
            dimension_semantics=("arbitrary", "arbitrary"), vmem_limit_bytes=VMEM_LIMIT),
        name="mix",
    )(x, attn, z, z, z, mod, pool_w, pool_scale, wba, wbp, wo, g2)


def _ffn_body(x1_ref, h2_ref, mod_ref, wup_ref, cw_ref, cb_ref, wdn_ref, o_ref, carry_ref, *hist):
    t = pl.program_id(1)
    sub = FFN_TILE // FFN_SPLIT
    fc = FFN_CHUNK
    n_chunks = D_FF // fc

    @pl.when(t == 0)
    def _():
        carry_ref[...] = jnp.zeros_like(carry_ref)

    units = [(part, c) for part in range(FFN_SPLIT) for c in range(n_chunks)]
    h2 = [h2_ref[0, part * sub:(part + 1) * sub, :] for part in range(FFN_SPLIT)]

    def up(k):
        part, c = units[k]
        out = []
        for half in range(2):
            cols = slice(half * D_FF + c * fc, half * D_FF + (c + 1) * fc)
            hcols = slice(half * fc, (half + 1) * fc)
            r = _dot(h2[part], wup_ref[:, cols])
            hist[k % len(hist)][0:CONV_HALO, hcols] = carry_ref[:, cols]
            hist[k % len(hist)][CONV_HALO:CONV_HALO + sub, hcols] = r
            carry_ref[:, cols] = r[sub - CONV_HALO:sub]
            out.append(r)
        return out

    def down(acc, k, act):
        part, c = units[k]
        acc[part] = acc[part] + _dot(act, wdn_ref[c * fc:(c + 1) * fc, :])

    acc = [jnp.zeros((sub, D_MODEL), F32) for _ in range(FFN_SPLIT)]
    ahead = [up(k) for k in range(FFN_AHEAD)]
    act_prev = None
    for k, (part, c) in enumerate(units):
        r_cur = ahead.pop(0)
        if k + FFN_AHEAD < len(units):
            ahead.append(up(k + FFN_AHEAD))
        if act_prev is not None:
            down(acc, k - 1, act_prev)
        halves = []
        for half in range(2):
            cols = slice(half * D_FF + c * fc, half * D_FF + (c + 1) * fc)
            hcols = slice(half * fc, (half + 1) * fc)
            u = cb_ref[:, cols] + cw_ref[CONV_W - 1:CONV_W, cols] * r_cur[half]
            for tap in range(CONV_W - 1):
                back = CONV_W - 1 - tap
                u = u + cw_ref[tap:tap + 1, cols] * hist[k % len(hist)][CONV_HALO - back:CONV_HALO - back + sub, hcols]
            halves.append(u)
        u_g, u_v = halves
        act_prev = (u_g * _sigmoid(u_g) * u_v).astype(BF16)
    down(acc, len(units) - 1, act_prev)
    for part in range(FFN_SPLIT):
        rows = slice(part * sub, (part + 1) * sub)
        o_ref[0, rows, :] = x1_ref[0, rows, :] + mod_ref[0, 3:4, :] * acc[part]


def _ffn(x1, h2, mod, wup_bf, conv_w, conv_b, wdn_bf):
    b, s, d = x1.shape
    tm = FFN_TILE
    const2 = lambda bi, t: (0, 0)
    return pl.pallas_call(
        _ffn_body,
        grid=(b, s // tm),
        in_specs=[
            pl.BlockSpec((1, tm, d), lambda bi, t: (bi, t, 0)),
            pl.BlockSpec((1, tm, d), lambda bi, t: (bi, t, 0)),
            pl.BlockSpec((1, N_MOD - N_MOD_EARLY, d), lambda bi, t: (bi, 0, 0)),
            pl.BlockSpec((d, 2 * D_FF), const2),
            pl.BlockSpec((CONV_W, 2 * D_FF), const2),
            pl.BlockSpec((1, 2 * D_FF), const2),
            pl.BlockSpec((D_FF, d), const2),
        ],
        out_specs=pl.BlockSpec((1, tm, d), lambda bi, t: (bi, t, 0)),
        out_shape=jax.ShapeDtypeStruct((b, s, d), F32),
        scratch_shapes=[
            pltpu.VMEM((CONV_HALO, 2 * D_FF), F32),
        ] + [pltpu.VMEM((tm // FFN_SPLIT + CONV_HALO, 2 * FFN_CHUNK), F32)
             for _ in range(FFN_AHEAD + 1)],
        compiler_params=pltpu.CompilerParams(
            dimension_semantics=("arbitrary", "arbitrary"), vmem_limit_bytes=VMEM_LIMIT),
        name="ffn",
    )(x1, h2, mod, wup_bf, conv_w, conv_b, wdn_bf)


def _layer(x, c_lanes, rel_bias, ada_w, ada_b, norm1_g, w_in, q_norm_g, k_norm_g, pool_w, pool_scale,
           w_branch_attn, w_branch_pool, w_out, norm2_g, w_up, conv_w, conv_b, w_down):
    b, s, d = x.shape
    mod_early = _ada_early(c_lanes, ada_w, ada_b[None, :])
    z, mod = _inproj(x, mod_early, norm1_g[None, :], w_in, c_lanes, ada_w, ada_b[None, :])
    gq_t = jnp.broadcast_to(jnp.tile(q_norm_g, HEADS_PER_STEP)[:, None], (PAIR_W, MOBA_BLOCK))
    gk_row = jnp.tile(k_norm_g, HEADS_PER_STEP)[None, :]
    attn, wup_bf, wdn_bf = _attention(z, gq_t, gk_row, _biasprep(rel_bias), w_up, w_down)
    x1, h2 = _mix(x, attn, z, mod, pool_w, pool_scale[None, :], w_branch_attn, w_branch_pool, w_out,
                  norm2_g[None, :])
    return _ffn(x1, h2, mod, wup_bf, conv_w, conv_b[None, :], wdn_bf)


def kernel(x, c, ada_w, ada_b, norm1_g, w_in, q_norm_g, k_norm_g, rel_bias, pool_w, pool_scale,
           w_branch_attn, w_branch_pool, w_out, norm2_g, w_up, conv_w, conv_b, w_down):
    b, s, d = x.shape
    assert d == D_MODEL and w_in.shape[-1] == IN_W
    assert all(s % tile == 0 for tile in (MOBA_BLOCK * PIPE_UNROLL, INPROJ_TILE, MIX_TILE, FFN_TILE))
    c_lanes = jnp.broadcast_to(c[:, :, None], (b, d, LANES))
    for l in range(ada_w.shape[0]):
        x = _layer(x, c_lanes, rel_bias, ada_w[l], ada_b[l], norm1_g[l], w_in[l], q_norm_g[l],
                   k_norm_g[l], pool_w[l], pool_scale[l], w_branch_attn[l], w_branch_pool[l],
                   w_out[l], norm2_g[l], w_up[l], conv_w[l], conv_b[l], w_down[l])
    return x
```

```python
import functools
import math

import numpy as np
import jax
import jax.numpy as jnp
from jax import lax
from jax.experimental import pallas as pl
from jax.experimental.pallas import tpu as pltpu

F32 = jnp.float32
BF16 = jnp.bfloat16

D_MODEL = 1024
ATTN_HEADS = 8
HEAD_DIM = 64
ATTN_W = ATTN_HEADS * HEAD_DIM
MOBA_BLOCK = 256
MOBA_TOPK = 3
POOL_GROUPS = 4
POOL_GROUP_W = 128
POOL_W = POOL_GROUPS * POOL_GROUP_W
POOL_WINDOWS = (2, 4, 8, 16)
NUM_BUCKETS = 32
MAX_DISTANCE = 128
D_FF = 2816
CONV_W = 3
EPS = 1e-6
IN_W = 3 * ATTN_W + POOL_W + 2 * D_MODEL

LANES = 128
BF16_SUBLANES = 16
HEADS_PER_STEP = LANES // HEAD_DIM
PAIR_W = HEADS_PER_STEP * HEAD_DIM
N_PAIRS = ATTN_HEADS // HEADS_PER_STEP
V_ROWS = HEAD_DIM + BF16_SUBLANES
LOG2E = math.log2(math.e)
MASK_BIG = 1e30
ADA_ROWS = 256
INPROJ_TILE = 1024
INPROJ_COLS = 1024
INPROJ_SPLIT = 4
POOL_HALO = 16
MIX_TILE = 512
MIX_SPLIT = 2
FFN_CHUNK = 256
FFN_TILE = 512
FFN_SPLIT = 2
FFN_AHEAD = 3
CONV_HALO = 8
VMEM_LIMIT = 56 * 1024 * 1024


def _split_bf16(a):
    hi = a.astype(BF16)
    lo = (a - hi.astype(F32)).astype(BF16)
    return hi, lo


def _dot(a, b):
    return jnp.dot(a, b, preferred_element_type=F32)


def _dot3(a, b):
    ah, al = _split_bf16(a)
    bh, bl = _split_bf16(b)
    return _dot(ah, bh) + _dot(al, bh) + _dot(ah, bl)


def _sigmoid(v):
    return 1.0 / (1.0 + jnp.exp2(v * (-LOG2E)))


def _cast_weight_once(src_ref, dst_ref, col_chunk=4 * LANES):
    first = functools.reduce(jnp.logical_and, [pl.program_id(a) == 0 for a in range(2)])

    @pl.when(first)
    def _():
        n = src_ref.shape[-1]
        for c0 in range(0, n, col_chunk):
            dst_ref[..., c0:c0 + col_chunk] = src_ref[..., c0:c0 + col_chunk].astype(BF16)


N_MOD = 6
N_MOD_EARLY = 2


def _ada_accumulate(c_ref, w_refs, b_refs, o_ref, first):
    n_batch, _, lanes = c_ref.shape
    dm = o_ref.shape[2]
    vectors = [(w_ref, b_ref, j) for w_ref, b_ref in zip(w_refs, b_refs) for j in range(w_ref.shape[1] // dm)]

    @pl.when(first)
    def _():
        for v, (_, b_ref, j) in enumerate(vectors):
            o_ref[:, v, :] = jnp.broadcast_to(b_ref[:, j * dm:(j + 1) * dm], (n_batch, dm))

    for bi in range(n_batch):
        c = c_ref[bi]
        act = c * _sigmoid(c)
        for v, (w_ref, _, j) in enumerate(vectors):
            cols = [jnp.sum(act * w_ref[:, j * dm + g * lanes:j * dm + (g + 1) * lanes], axis=0, keepdims=True)
                    for g in range(dm // lanes)]
            o_ref[bi, v:v + 1, :] += jnp.concatenate(cols, axis=-1)


def _ada_body(c_ref, w_ref, b_ref, o_ref):
    _ada_accumulate(c_ref, [w_ref], [b_ref], o_ref, pl.program_id(0) == 0)


def _ada_early(c_lanes, ada_w, ada_b):
    n_batch, d, lanes = c_lanes.shape
    n = N_MOD_EARLY * d
    tk = ADA_ROWS
    return pl.pallas_call(
        _ada_body,
        grid=(d // tk,),
        in_specs=[
            pl.BlockSpec((n_batch, tk, lanes), lambda k: (0, k, 0)),
            pl.BlockSpec((tk, n), lambda k: (k, 0)),
            pl.BlockSpec((1, n), lambda k: (0, 0)),
        ],
        out_specs=pl.BlockSpec((n_batch, N_MOD_EARLY, d), lambda k: (0, 0, 0)),
        out_shape=jax.ShapeDtypeStruct((n_batch, N_MOD_EARLY, d), F32),
        compiler_params=pltpu.CompilerParams(
            dimension_semantics=("arbitrary",), vmem_limit_bytes=VMEM_LIMIT),
        name="ada",
    )(c_lanes, ada_w, ada_b)


def _norm_mod(x, gain, scale, shift):
    ms = jnp.mean(x * x, axis=-1, keepdims=True)
    return x * lax.rsqrt(ms + EPS) * (gain * (1.0 + scale)) + shift


def _inproj_body(x_ref, mod_ref, g_ref, w32_ref, c_ref, aw0_ref, aw1_ref, ab0_ref, ab1_ref, z_ref, mod_late_ref,
                 w_ref):
    _cast_weight_once(w32_ref, w_ref)
    first = jnp.logical_and(pl.program_id(0) == 0, pl.program_id(1) == 0)
    _ada_accumulate(c_ref, [aw0_ref, aw1_ref], [ab0_ref, ab1_ref], mod_late_ref, first)
    sub = INPROJ_TILE // INPROJ_SPLIT
    n_chunk = INPROJ_COLS
    for part in range(INPROJ_SPLIT):
        rows = slice(part * sub, (part + 1) * sub)
        h = _norm_mod(x_ref[0, rows, :], g_ref[...], mod_ref[0, 1:2, :], mod_ref[0, 0:1, :]).astype(BF16)
        for n in range(IN_W // n_chunk):
            cols = slice(n * n_chunk, (n + 1) * n_chunk)
            z_ref[0, rows, cols] = _dot(h, w_ref[:, cols]).astype(BF16)


def _inproj(x, mod_early, g1, w_in, c_lanes, ada_w, ada_b):
    b, s, d = x.shape
    tiles = s // INPROJ_TILE
    n_late = N_MOD - N_MOD_EARLY
    pair = 2 * d
    assert N_MOD_EARLY * d == pair and n_late * d == 2 * pair
    slab_rows = d // (b * tiles)
    assert slab_rows * b * tiles == d and slab_rows % 8 == 0
    slab = lambda bi, t: bi * tiles + t
    return pl.pallas_call(
        _inproj_body,
        grid=(b, tiles),
        in_specs=[
            pl.BlockSpec((1, INPROJ_TILE, d), lambda bi, t: (bi, t, 0)),
            pl.BlockSpec((1, N_MOD_EARLY, d), lambda bi, t: (bi, 0, 0)),
            pl.BlockSpec((1, d), lambda bi, t: (0, 0)),
            pl.BlockSpec((d, IN_W), lambda bi, t: (0, 0)),
            pl.BlockSpec((b, slab_rows, c_lanes.shape[2]), lambda bi, t: (0, slab(bi, t), 0)),
            pl.BlockSpec((slab_rows, pair), lambda bi, t: (slab(bi, t), 1)),
            pl.BlockSpec((slab_rows, pair), lambda bi, t: (slab(bi, t), 2)),
            pl.BlockSpec((1, pair), lambda bi, t: (0, 1)),
            pl.BlockSpec((1, pair), lambda bi, t: (0, 2)),
        ],
        out_specs=[
            pl.BlockSpec((1, INPROJ_TILE, IN_W), lambda bi, t: (bi, t, 0)),
            pl.BlockSpec((b, n_late, d), lambda bi, t: (0, 0, 0)),
        ],
        out_shape=[
            jax.ShapeDtypeStruct((b, s, IN_W), BF16),
            jax.ShapeDtypeStruct((b, n_late, d), F32),
        ],
        scratch_shapes=[pltpu.VMEM((d, IN_W), BF16)],
        compiler_params=pltpu.CompilerParams(
            dimension_semantics=("arbitrary", "arbitrary"), vmem_limit_bytes=VMEM_LIMIT),
        name="inproj",
    )(x, mod_early, g1, w_in, c_lanes, ada_w, ada_w, ada_b, ada_b)


def _t5_bucket_np(dist):
    max_exact = NUM_BUCKETS // 2
    n = np.maximum(dist, 0)
    nf = np.maximum(n, 1).astype(np.float64)
    large = max_exact + (np.log(nf / max_exact) / math.log(MAX_DISTANCE / max_exact)
                         * (NUM_BUCKETS - max_exact)).astype(np.int32)
    large = np.minimum(large, NUM_BUCKETS - 1)
    return np.where(n < max_exact, n, large).astype(np.int32)


BIAS_ROW_W = 4 * MOBA_BLOCK


def _bucket_row():
    d = np.arange(BIAS_ROW_W) - MOBA_BLOCK
    return np.where(d >= 0, _t5_bucket_np(d), -1).astype(np.int32)[None, :]


def _biasprep_body(rb_ref, bkt_ref, o_ref):
    bkt = bkt_ref[...]
    for h in range(ATTN_HEADS):
        far = rb_ref[NUM_BUCKETS - 1, h]
        row = jnp.where(bkt < 0, -MASK_BIG, 0.0).astype(F32)
        for bk in range(NUM_BUCKETS):
            row = jnp.where(bkt == bk, (rb_ref[bk, h] - far) * LOG2E, row)
        rows = jnp.broadcast_to(row, (MOBA_BLOCK, BIAS_ROW_W))
        rolled = pltpu.roll(rows, 0, 1, stride=1, stride_axis=0)
        o_ref[h, 0] = rolled[:, MOBA_BLOCK:2 * MOBA_BLOCK]
        o_ref[h, 1] = rolled[:, 2 * MOBA_BLOCK:3 * MOBA_BLOCK]


def _biasprep(rel_bias):
    bkt = jnp.asarray(_bucket_row())
    return pl.pallas_call(
        _biasprep_body,
        grid=(1,),
        in_specs=[
            pl.BlockSpec(memory_space=pltpu.SMEM),
            pl.BlockSpec((1, BIAS_ROW_W), lambda n: (0, 0)),
        ],
        out_specs=pl.BlockSpec((ATTN_HEADS, 2, MOBA_BLOCK, MOBA_BLOCK), lambda n: (0, 0, 0, 0)),
        out_shape=jax.ShapeDtypeStruct((ATTN_HEADS, 2, MOBA_BLOCK, MOBA_BLOCK), F32),
        compiler_params=pltpu.CompilerParams(dimension_semantics=("arbitrary",)),
        name="biasprep",
    )(rel_bias, bkt)


PREP_UNROLL = 8
ITEM_FIELDS = 5
PIPE_UNROLL = 16
PIPE_DEPTH = 3
PIPE_SLOTS = 4


def _far_items(n_blocks):
    never = n_blocks - 1
    items = []
    for i in range(n_blocks):
        n_far = max(i - 1, 0)
        for j0 in range(0, n_far, 2):
            items.append((i, j0, j0, j0 + 1, j0 + 1) if j0 + 1 < n_far else (i, j0, j0, j0, never))
    dummy = (n_blocks - 1, 0, never, 0, never)
    items.extend([dummy] * (-len(items) % PIPE_UNROLL))
    items.extend([dummy] * PIPE_DEPTH)
    return np.asarray(items, np.int32)


def _attn_body(tbl_ref, zq_ref, zk_ref, zv_ref, gq_ref, gk_ref, bias_ref, wup32_ref, wdn32_ref,
               o_ref, wup_ref, wdn_ref,
               kn_ref, vta_ref, w_ref, pen_ref, acc_ref, m_ref, *slots, n_blocks, n_items):
    blk = MOBA_BLOCK
    heads = range(HEADS_PER_STEP)

    wup_ref[...] = wup32_ref[...].astype(BF16)
    wdn_ref[...] = wdn32_ref[...].astype(BF16)

    def rows_of(j):
        return pl.ds(pl.multiple_of(j * blk, blk), blk)

    blk_rows = lax.broadcasted_iota(jnp.int32, (n_blocks, PAIR_W), 0)
    ones = jnp.ones((V_ROWS - HEAD_DIM, blk), BF16)
    same_head = (lax.broadcasted_iota(jnp.int32, (PAIR_W, PAIR_W), 0) // HEAD_DIM
                 == lax.broadcasted_iota(jnp.int32, (PAIR_W, PAIR_W), 1) // HEAD_DIM)
    head_sum = jnp.where(same_head, 1.0, 0.0).astype(BF16)

    def kv_prep(i, kmean):
        kraw = zk_ref[0, rows_of(i), :].astype(F32)
        sq_hi, sq_lo = _split_bf16(kraw * kraw)
        ss = _dot(sq_hi, head_sum) + _dot(sq_lo, head_sum)
        rinv = lax.rsqrt(ss * (1.0 / HEAD_DIM) + EPS)
        kn = kraw * rinv * gk_ref[...]
        kn_ref[rows_of(i), :] = kn.astype(BF16)
        vt = zv_ref[0, rows_of(i), :].astype(F32).T
        for hd in heads:
            vta_ref[i, hd, 0:HEAD_DIM, :] = vt[hd * HEAD_DIM:(hd + 1) * HEAD_DIM].astype(BF16)
            vta_ref[i, hd, HEAD_DIM:V_ROWS, :] = ones
        return jnp.where(blk_rows == i, jnp.mean(kn, axis=0, keepdims=True), kmean)

    kmean = lax.fori_loop(0, n_blocks, kv_prep, jnp.zeros((n_blocks, PAIR_W), F32), unroll=PREP_UNROLL)

    blk_ids = lax.broadcasted_iota(jnp.int32, (n_blocks, blk), 0)
    zeros_half = jnp.zeros((HEAD_DIM, blk), F32)

    def q_prep(i, _):
        qt = zq_ref[0, rows_of(i), :].astype(F32).T
        qsq = qt * qt
        for hd in heads:
            rows = slice(hd * HEAD_DIM, (hd + 1) * HEAD_DIM)
            ss = jnp.sum(qsq[rows], axis=0, keepdims=True)
            r = lax.rsqrt(ss * (1.0 / HEAD_DIM) + EPS)
            qn = qt[rows] * r * gq_ref[rows, :] * (HEAD_DIM ** -0.5 * LOG2E)
            parts = [zeros_half] * HEADS_PER_STEP
            parts[hd] = qn
            w = jnp.concatenate(parts, axis=0)
            w_ref[i, hd] = w.astype(BF16)
            gate = jnp.where(blk_ids < i, _dot3(kmean, w), -jnp.inf)
            sel = jnp.zeros((n_blocks, blk), jnp.bool_)
            for _ in range(MOBA_TOPK):
                mx = jnp.max(gate, axis=0, keepdims=True)
                cand = jnp.where(gate == mx, blk_ids, n_blocks)
                pick = (blk_ids == jnp.min(cand, axis=0, keepdims=True)) & (mx > -jnp.inf)
                sel = sel | pick
                gate = jnp.where(pick, -jnp.inf, gate)
            pen_ref[hd, i] = jnp.where(sel, 0.0, -MASK_BIG).astype(F32)
        return 0

    lax.fori_loop(0, n_blocks, q_prep, 0, unroll=PREP_UNROLL)

    def score(i, blocks, s_ref, biased):
        keys = [kn_ref[rows_of(j), :] for j, _ in blocks]
        gmax = []
        for hd in heads:
            w = w_ref[i, hd]
            tmax = []
            for u, (_, jpen) in enumerate(blocks):
                s = _dot(keys[u], w)
                if biased:
                    s = s + bias_ref[hd, u]
                s_ref[hd, u] = s
                t = jnp.max(s, axis=0, keepdims=True)
                tmax.append(t if jpen is None else t + pen_ref[hd, i, pl.ds(jpen, 1), :])
            gmax.append(functools.reduce(jnp.maximum, tmax))
        return tuple(gmax)

    def attend(i, blocks, s_ref, gmax, first_visit):
        for hd in heads:
            if first_visit:
                m_new = gmax[hd]
                acc = None
            else:
                m_old = m_ref[i, hd]
                m_new = jnp.maximum(m_old, gmax[hd])
                acc = acc_ref[i, hd] * jnp.exp2(m_old - m_new)
            for u, (j, jpen) in enumerate(blocks):
                shift = m_new if jpen is None else m_new - pen_ref[hd, i, pl.ds(jpen, 1), :]
                p = jnp.exp2(s_ref[hd, u] - shift).astype(BF16)
                pv = _dot(vta_ref[j, hd], p)
                acc = pv if acc is None else acc + pv
            acc_ref[i, hd] = acc
            m_ref[i, hd] = m_new

    def near_blocks(i):
        jp = jnp.maximum(i - 1, 0)
        return ((i, None), (jp, jp))

    n_slots = len(slots)

    def near_body(t, g):
        for u in range(PIPE_UNROLL):
            i_cur = PIPE_UNROLL * t + u
            i_nxt = jnp.minimum(i_cur + PIPE_DEPTH, n_blocks - 1)
            g_nxt = score(i_nxt, near_blocks(i_nxt), slots[(u + PIPE_DEPTH) % n_slots], True)
            attend(i_cur, near_blocks(i_cur), slots[u % n_slots], g[0], True)
            g = g[1:] + (g_nxt,)
        return g

    lax.fori_loop(0, n_blocks // PIPE_UNROLL, near_body,
                  tuple(score(i, near_blocks(i), slots[i], True) for i in range(PIPE_DEPTH)))

    def item(k):
        base = k * ITEM_FIELDS
        i = tbl_ref[base]
        return i, ((tbl_ref[base + 1], tbl_ref[base + 2]), (tbl_ref[base + 3], tbl_ref[base + 4]))

    def far_body(t, g):
        for u in range(PIPE_UNROLL):
            k = PIPE_UNROLL * t + u
            g_nxt = score(*item(k + PIPE_DEPTH), slots[(u + PIPE_DEPTH) % n_slots], False)
            attend(*item(k), slots[u % n_slots], g[0], False)
            g = g[1:] + (g_nxt,)
        return g

    lax.fori_loop(0, n_items // PIPE_UNROLL, far_body,
                  tuple(score(*item(k), slots[k], False) for k in range(PIPE_DEPTH)))

    for i in range(n_blocks):
        outs = []
        for hd in heads:
            acc = acc_ref[i, hd]
            outs.append(acc[0:HEAD_DIM] / acc[HEAD_DIM:HEAD_DIM + 1])
        o_ref[0, :, i * blk:(i + 1) * blk] = jnp.concatenate(outs, axis=0).astype(BF16)


def _attention(z, gq_t, gk_row, bias, w_up, w_down):
    b, s, _ = z.shape
    n_blocks = s // MOBA_BLOCK
    assert n_blocks % PIPE_UNROLL == 0 and PIPE_UNROLL % PIPE_SLOTS == 0 and PIPE_DEPTH < PIPE_SLOTS
    blk = MOBA_BLOCK
    items = _far_items(n_blocks)
    n_items = items.shape[0] - PIPE_DEPTH
    n_steps = b * N_PAIRS
    up_rows, dn_rows = w_up.shape[0] // n_steps, w_down.shape[0] // n_steps
    assert up_rows * n_steps == w_up.shape[0] and dn_rows * n_steps == w_down.shape[0]
    assert up_rows % BF16_SUBLANES == 0 and dn_rows % BF16_SUBLANES == 0

    def slab(bi, hp, tbl):
        return (bi * N_PAIRS + hp, 0)

    grid_spec = pltpu.PrefetchScalarGridSpec(
        num_scalar_prefetch=1,
        grid=(b, N_PAIRS),
        in_specs=[
            pl.BlockSpec((1, s, PAIR_W), lambda bi, hp, tbl: (bi, 0, hp)),
            pl.BlockSpec((1, s, PAIR_W), lambda bi, hp, tbl: (bi, 0, N_PAIRS + hp)),
            pl.BlockSpec((1, s, PAIR_W), lambda bi, hp, tbl: (bi, 0, 2 * N_PAIRS + hp)),
            pl.BlockSpec((PAIR_W, blk), lambda bi, hp, tbl: (0, 0)),
            pl.BlockSpec((1, PAIR_W), lambda bi, hp, tbl: (0, 0)),
            pl.BlockSpec((HEADS_PER_STEP, 2, blk, blk), lambda bi, hp, tbl: (hp, 0, 0, 0)),
            pl.BlockSpec((up_rows, w_up.shape[1]), slab),
            pl.BlockSpec((dn_rows, w_down.shape[1]), slab),
        ],
        out_specs=[
            pl.BlockSpec((1, PAIR_W, s), lambda bi, hp, tbl: (bi, hp, 0)),
            pl.BlockSpec((up_rows, w_up.shape[1]), slab),
            pl.BlockSpec((dn_rows, w_down.shape[1]), slab),
        ],
        scratch_shapes=[
            pltpu.VMEM((s, PAIR_W), BF16),
            pltpu.VMEM((n_blocks, HEADS_PER_STEP, V_ROWS, blk), BF16),
            pltpu.VMEM((n_blocks, HEADS_PER_STEP, PAIR_W, blk), BF16),
            pltpu.VMEM((HEADS_PER_STEP, n_blocks, n_blocks, blk), F32),
            pltpu.VMEM((n_blocks, HEADS_PER_STEP, V_ROWS, blk), F32),
            pltpu.VMEM((n_blocks, HEADS_PER_STEP, 1, blk), F32),
        ] + [pltpu.VMEM((HEADS_PER_STEP, 2, blk, blk), F32)
             for _ in range(PIPE_SLOTS)],
    )
    return pl.pallas_call(
        functools.partial(_attn_body, n_blocks=n_blocks, n_items=n_items),
        grid_spec=grid_spec,
        out_shape=[
            jax.ShapeDtypeStruct((b, ATTN_W, s), BF16),
            jax.ShapeDtypeStruct(w_up.shape, BF16),
            jax.ShapeDtypeStruct(w_down.shape, BF16),
        ],
        compiler_params=pltpu.CompilerParams(
            dimension_semantics=("arbitrary", "arbitrary"), vmem_limit_bytes=VMEM_LIMIT),
        name="attn",
    )(jnp.asarray(items.reshape(-1)), z, z, z, gq_t, gk_row, bias, w_up, w_down)


def _mix_body(x_ref, attn_ref, p_ref, ga_ref, gp_ref, mod_ref, pw32_ref, ps_ref, wba32_ref, wbp32_ref,
              wo32_ref, g2_ref, x1_ref, h2_ref, pe_ref, pw_ref, wba_ref, wbp_ref, wo_ref):
    for w32, wbf in ((pw32_ref, pw_ref), (wba32_ref, wba_ref), (wbp32_ref, wbp_ref), (wo32_ref, wo_ref)):
        _cast_weight_once(w32, wbf)
    t = pl.program_id(1)
    tm = MIX_TILE
    sub = MIX_TILE // MIX_SPLIT

    @pl.when(t == 0)
    def _():
        pe_ref[0:POOL_HALO, :] = jnp.zeros((POOL_HALO, POOL_W), F32)

    @pl.when(t > 0)
    def _():
        pe_ref[0:POOL_HALO, :] = pe_ref[tm:tm + POOL_HALO, :]

    pe_ref[POOL_HALO:POOL_HALO + tm, :] = p_ref[0].astype(F32)

    parts = range(MIX_SPLIT)
    rows = [slice(part * sub, (part + 1) * sub) for part in parts]

    def pool_inputs(part):
        r0 = part * sub
        pos = t * tm + r0 + lax.broadcasted_iota(jnp.int32, (sub, POOL_GROUP_W), 0)
        out = []
        for g, win in enumerate(POOL_WINDOWS):
            cols = slice(g * POOL_GROUP_W, (g + 1) * POOL_GROUP_W)
            ext = pe_ref[r0:r0 + POOL_HALO + sub, cols]
            tot = ext
            span = 1
            while span < win:
                tot = tot + pltpu.roll(tot, span, 0)
                span *= 2
            count = jnp.minimum(pos + 1, win).astype(F32)
            out.append((tot[POOL_HALO:] / count - ext[POOL_HALO:]).astype(BF16))
        return out

    a_proj = [lax.dot_general(attn_ref[0, :, rows[part]], wba_ref[...], (((0,), (0,)), ((), ())),
                              preferred_element_type=F32) for part in parts]
    pooled_in = [pool_inputs(part) for part in parts]
    pool = []
    for part in parts:
        pooled = [_dot(pooled_in[part][g], pw_ref[g]) * ps_ref[:, g * POOL_GROUP_W:(g + 1) * POOL_GROUP_W]
                  for g in range(POOL_GROUPS)]
        pool.append(jnp.concatenate(pooled, axis=-1).astype(BF16))
    merged = []
    for part in parts:
        p_proj = _dot(pool[part], wbp_ref[...])
        merged.append((_sigmoid(ga_ref[0, rows[part], :].astype(F32)) * a_proj[part]
                       + _sigmoid(gp_ref[0, rows[part], :].astype(F32)) * p_proj).astype(BF16))
    for part in parts:
        x1 = x_ref[0, rows[part], :] + mod_ref[0, 0:1, :] * _dot(merged[part], wo_ref[...])
        x1_ref[0, rows[part], :] = x1
        h2_ref[0, rows[part], :] = _norm_mod(x1, g2_ref[...], mod_ref[0, 2:3, :],
                                             mod_ref[0, 1:2, :]).astype(BF16)


def _mix(x, attn, z, mod, pool_w, pool_scale, wba, wbp, wo, g2):
    b, s, d = x.shape
    tm = MIX_TILE
    const2 = lambda bi, t: (0, 0)
    return pl.pallas_call(
        _mix_body,
        grid=(b, s // tm),
        in_specs=[
            pl.BlockSpec((1, tm, d), lambda bi, t: (bi, t, 0)),
            pl.BlockSpec((1, ATTN_W, tm), lambda bi, t: (bi, 0, t)),
            pl.BlockSpec((1, tm, POOL_W), lambda bi, t: (bi, t, 3 * ATTN_W // POOL_W)),
            pl.BlockSpec((1, tm, d), lambda bi, t: (bi, t, (3 * ATTN_W + POOL_W) // d)),
            pl.BlockSpec((1, tm, d), lambda bi, t: (bi, t, (3 * ATTN_W + POOL_W) // d + 1)),
            pl.BlockSpec((1, N_MOD - N_MOD_EARLY, d), lambda bi, t: (bi, 0, 0)),
            pl.BlockSpec((POOL_GROUPS, POOL_GROUP_W, POOL_GROUP_W), lambda bi, t: (0, 0, 0)),
            pl.BlockSpec((1, POOL_W), const2),
            pl.BlockSpec((ATTN_W, d), const2),
            pl.BlockSpec((POOL_W, d), const2),
            pl.BlockSpec((d, d), const2),
            pl.BlockSpec((1, d), const2),
        ],
        out_specs=[
            pl.BlockSpec((1, tm, d), lambda bi, t: (bi, t, 0)),
            pl.BlockSpec((1, tm, d), lambda bi, t: (bi, t, 0)),
        ],
        out_shape=[
            jax.ShapeDtypeStruct((b, s, d), F32),
            jax.ShapeDtypeStruct((b, s, d), BF16),
        ],
        scratch_shapes=[
            pltpu.VMEM((tm + POOL_HALO, POOL_W), F32),
            pltpu.VMEM(pool_w.shape, BF16),
            pltpu.VMEM(wba.shape, BF16),
            pltpu.VMEM(wbp.shape, BF16),
            pltpu.VMEM(wo.shape, BF16),
        ],
        compiler_params=pltpu.CompilerParams(
            dimension_semantics=("arbitrary", "arbitrary"), vmem_limit_bytes=VMEM_LIMIT),
        name="mix",
    )(x, attn, z, z, z, mod, pool_w, pool_scale, wba, wbp, wo, g2)


def _ffn_body(x1_ref, h2_ref, mod_ref, wup_ref, cw_ref, cb_ref, wdn_ref, o_ref, carry_ref, *hist):
    t = pl.program_id(1)
    sub = FFN_TILE // FFN_SPLIT
    fc = FFN_CHUNK
    n_chunks = D_FF // fc

    @pl.when(t == 0)
    def _():
        carry_ref[...] = jnp.zeros_like(carry_ref)

    units = [(part, c) for part in range(FFN_SPLIT) for c in range(n_chunks)]
    h2 = [h2_ref[0, part * sub:(part + 1) * sub, :] for part in range(FFN_SPLIT)]

    def up(k):
        part, c = units[k]
        out = []
        for half in range(2):
            cols = slice(half * D_FF + c * fc, half * D_FF + (c + 1) * fc)
            hcols = slice(half * fc, (half + 1) * fc)
            r = _dot(h2[part], wup_ref[:, cols])
            hist[k % len(hist)][0:CONV_HALO, hcols] = carry_ref[:, cols]
            hist[k % len(hist)][CONV_HALO:CONV_HALO + sub, hcols] = r
            carry_ref[:, cols] = r[sub - CONV_HALO:sub]
            out.append(r)
        return out

    def down(acc, k, act):
        part, c = units[k]
        acc[part] = acc[part] + _dot(act, wdn_ref[c * fc:(c + 1) * fc, :])

    acc = [jnp.zeros((sub, D_MODEL), F32) for _ in range(FFN_SPLIT)]
    ahead = [up(k) for k in range(FFN_AHEAD)]
    act_prev = None
    for k, (part, c) in enumerate(units):
        r_cur = ahead.pop(0)
        if k + FFN_AHEAD < len(units):
            ahead.append(up(k + FFN_AHEAD))
        if act_prev is not None:
            down(acc, k - 1, act_prev)
        halves = []
        for half in range(2):
            cols = slice(half * D_FF + c * fc, half * D_FF + (c + 1) * fc)
            hcols = slice(half * fc, (half + 1) * fc)
            u = cb_ref[:, cols] + cw_ref[CONV_W - 1:CONV_W, cols] * r_cur[half]
            for tap in range(CONV_W - 1):
                back = CONV_W - 1 - tap
                u = u + cw_ref[tap:tap + 1, cols] * hist[k % len(hist)][CONV_HALO - back:CONV_HALO - back + sub, hcols]
            halves.append(u)
        u_g, u_v = halves
        act_prev = (u_g * _sigmoid(u_g) * u_v).astype(BF16)
    down(acc, len(units) - 1, act_prev)
    for part in range(FFN_SPLIT):
        rows = slice(part * sub, (part + 1) * sub)
        o_ref[0, rows, :] = x1_ref[0, rows, :] + mod_ref[0, 3:4, :] * acc[part]


def _ffn(x1, h2, mod, wup_bf, conv_w, conv_b, wdn_bf):
    b, s, d = x1.shape
    tm = FFN_TILE
    const2 = lambda bi, t: (0, 0)
    return pl.pallas_call(
        _ffn_body,
        grid=(b, s // tm),
        in_specs=[
            pl.BlockSpec((1, tm, d), lambda bi, t: (bi, t, 0)),
            pl.BlockSpec((1, tm, d), lambda bi, t: (bi, t, 0)),
            pl.BlockSpec((1, N_MOD - N_MOD_EARLY, d), lambda bi, t: (bi, 0, 0)),
            pl.BlockSpec((d, 2 * D_FF), const2),
            pl.BlockSpec((CONV_W, 2 * D_FF), const2),
            pl.BlockSpec((1, 2 * D_FF), const2),
            pl.BlockSpec((D_FF, d), const2),
        ],
        out_specs=pl.BlockSpec((1, tm, d), lambda bi, t: (bi, t, 0)),
        out_shape=jax.ShapeDtypeStruct((b, s, d), F32),
        scratch_shapes=[
            pltpu.VMEM((CONV_HALO, 2 * D_FF), F32),
        ] + [pltpu.VMEM((tm // FFN_SPLIT + CONV_HALO, 2 * FFN_CHUNK), F32)
             for _ in range(FFN_AHEAD + 1)],
        compiler_params=pltpu.CompilerParams(
            dimension_semantics=("arbitrary", "arbitrary"), vmem_limit_bytes=VMEM_LIMIT),
        name="ffn",
    )(x1, h2, mod, wup_bf, conv_w, conv_b, wdn_bf)


def _layer(x, c_lanes, rel_bias, ada_w, ada_b, norm1_g, w_in, q_norm_g, k_norm_g, pool_w, pool_scale,
           w_branch_attn, w_branch_pool, w_out, norm2_g, w_up, conv_w, conv_b, w_down):
    b, s, d = x.shape
    mod_early = _ada_early(c_lanes, ada_w, ada_b[None, :])
    z, mod = _inproj(x, mod_early, norm1_g[None, :], w_in, c_lanes, ada_w, ada_b[None, :])
    gq_t = jnp.broadcast_to(jnp.tile(q_norm_g, HEADS_PER_STEP)[:, None], (PAIR_W, MOBA_BLOCK))
    gk_row = jnp.tile(k_norm_g, HEADS_PER_STEP)[None, :]
    attn, wup_bf, wdn_bf = _attention(z, gq_t, gk_row, _biasprep(rel_bias), w_up, w_down)
    x1, h2 = _mix(x, attn, z, mod, pool_w, pool_scale[None, :], w_branch_attn, w_branch_pool, w_out,
                  norm2_g[None, :])
    return _ffn(x1, h2, mod, wup_bf, conv_w, conv_b[None, :], wdn_bf)


def kernel(x, c, ada_w, ada_b, norm1_g, w_in, q_norm_g, k_norm_g, rel_bias, pool_w, pool_scale,
           w_branch_attn, w_branch_pool, w_out, norm2_g, w_up, conv_w, conv_b, w_down):
    b, s, d = x.shape
    assert d == D_MODEL and w_in.shape[-1] == IN_W
    assert all(s % tile == 0 for tile in (MOBA_BLOCK * PIPE_UNROLL, INPROJ_TILE, MIX_TILE, FFN_TILE))
    c_lanes = jnp.broadcast_to(c[:, :, None], (b, d, LANES))
    for l in range(ada_w.shape[0]):
        x = _layer(x, c_lanes, rel_bias, ada_w[l], ada_b[l], norm1_g[l], w_in[l], q_norm_g[l],
                   k_norm_g[l], pool_w[l], pool_scale[l], w_branch_attn[l], w_branch_pool[l],
                   w_out[l], norm2_g[l], w_up[l], conv_w[l], conv_b[l], w_down[l])
    return x
```
